```python
import jax
import jax.numpy as jnp
from jax import lax
import numpy as np

D_MODEL = 1024
BATCH = 2
SEQ = 8192
DEPTH = 2
DEC_BATCH = 128
DEC_SEQ = 4
PAST_LEN = 2048
PAGE_SIZE = 128

HEAD_DIM = 64
MIX_WIDTH = D_MODEL
NSA_WIDTH = MIX_WIDTH // 2
GMLP_WIDTH = MIX_WIDTH - NSA_WIDTH
NSA_HEADS = NSA_WIDTH // HEAD_DIM
NSA_KV_HEADS = 2
NSA_GROUP = NSA_HEADS // NSA_KV_HEADS
KV_WIDTH = NSA_KV_HEADS * HEAD_DIM
CMP_BLOCK = 32
CMP_HIDDEN = 4 * HEAD_DIM
SLC_BLOCK = 64
SLC_PER_CMP = SLC_BLOCK // CMP_BLOCK
N_SELECT = 16
WINDOW = 512
QUERY_BLOCK = 128
FORCE_BONUS = 1.0e4
GMLP_GROUPS = 8
GMLP_GROUP_DIM = GMLP_WIDTH // GMLP_GROUPS
GMLP_CHUNK = 128
FFN_HIDDEN = -(-8 * D_MODEL // (3 * 256)) * 256
ROPE_THETA = 10000.0
LN_EPS = 1e-5
DEEPNORM_ALPHA = (2 * DEPTH) ** 0.25
DEEPNORM_BETA = (8 * DEPTH) ** -0.25
NEG_INF = -1e30

O_CMP = NSA_WIDTH
O_SLC = O_CMP + 2 * KV_WIDTH
O_WIN = O_SLC + 2 * KV_WIDTH
O_GATE = O_WIN + 2 * KV_WIDTH
O_GMLP = O_GATE + 3 * NSA_HEADS
P_IN = O_GMLP + 2 * GMLP_WIDTH

kernel_name = 'nsa_gmlp_hybrid_step'


def layer_norm(x, g, b):
    xf = x.astype(jnp.float32)
    mu = jnp.mean(xf, axis=-1, keepdims=True)
    var = jnp.mean(jnp.square(xf - mu), axis=-1, keepdims=True)
    return ((xf - mu) * lax.rsqrt(var + LN_EPS) * g + b).astype(x.dtype)


def rope(x, pos):
    half = HEAD_DIM // 2
    inv_freq = ROPE_THETA ** (-jnp.arange(half, dtype=jnp.float32) / half)
    ang = pos.astype(jnp.float32)[:, None] * inv_freq[None, :]
    cos = jnp.cos(ang)[:, None, :]
    sin = jnp.sin(ang)[:, None, :]
    xf = x.astype(jnp.float32)
    x1, x2 = xf[..., :half], xf[..., half:]
    return jnp.concatenate([x1 * cos - x2 * sin, x2 * cos + x1 * sin], axis=-1).astype(x.dtype)


def rope_k(kv, pos):
    return jnp.stack([rope(kv[:, :, 0], pos), kv[:, :, 1]], axis=2)


def masked_softmax(s, mask):
    s = jnp.where(mask, s, NEG_INF)
    m = jnp.max(s, axis=-1, keepdims=True)
    e = jnp.where(mask, jnp.exp(s - m), 0.0)
    return e / jnp.maximum(jnp.sum(e, axis=-1, keepdims=True), 1e-30)


def project(x, pos, w_in):
    B, T, _ = x.shape
    z = jnp.einsum('btd,dp->btp', x, w_in)
    q = z[..., :O_CMP].reshape(B, T, NSA_HEADS, HEAD_DIM)
    q_raw = q.reshape(B, T, NSA_KV_HEADS, NSA_GROUP, HEAD_DIM)
    q_rot = rope(q, pos).reshape(B, T, NSA_KV_HEADS, NSA_GROUP, HEAD_DIM)
    kv_shape = (B, T, 2, NSA_KV_HEADS, HEAD_DIM)
    cmp_kv = z[..., O_CMP:O_SLC].reshape(kv_shape)
    slc_kv = rope_k(z[..., O_SLC:O_WIN].reshape(kv_shape), pos)
    win_kv = rope_k(z[..., O_WIN:O_GATE].reshape(kv_shape), pos)
    gates = jax.nn.sigmoid(z[..., O_GATE:O_GMLP].astype(jnp.float32)).astype(z.dtype)
    gates = gates.reshape(B, T, 3, NSA_KV_HEADS, NSA_GROUP, 1)
    uv = z[..., O_GMLP:]
    return q_raw, q_rot, cmp_kv, slc_kv, win_kv, gates, uv


def compress_blocks(kv, pe, w1, b1, w2):
    B, L = kv.shape[:2]
    nc = L // CMP_BLOCK
    blk = kv[:, :nc * CMP_BLOCK].reshape(B, nc, CMP_BLOCK, 2, NSA_KV_HEADS, HEAD_DIM)
    blk = blk + jnp.transpose(pe, (1, 0, 2))[None, None, :, :, None, :]
    h = jax.nn.silu(jnp.einsum('bnlkgd,kldh->bnkgh', blk, w1) + b1[:, None, :])
    return jnp.einsum('bnkgh,khd->bnkgd', h, w2)


def selection_blocks(kv):
    B, L = kv.shape[:2]
    ns = -(-L // SLC_BLOCK)
    kv = jnp.pad(kv, ((0, 0), (0, ns * SLC_BLOCK - L), (0, 0), (0, 0), (0, 0)))
    kv = kv.reshape(B, ns, SLC_BLOCK, 2, NSA_KV_HEADS, HEAD_DIM)
    return jnp.transpose(kv, (0, 3, 4, 1, 2, 5))


def nsa_attend(q_raw, q_rot, qpos, ckv, sel_kv, wkv, wpos, gates):
    B, T = q_raw.shape[:2]
    scale = HEAD_DIM ** -0.5
    ck, cv = ckv[:, :, 0], ckv[:, :, 1]
    nc = ck.shape[1]
    s_c = jnp.einsum('btgrd,bngd->btgrn', q_raw, ck).astype(jnp.float32) * scale
    cmp_end = (jnp.arange(nc, dtype=jnp.int32) + 1) * CMP_BLOCK - 1
    c_mask = cmp_end[None, :] <= qpos[:, None]
    p_c = masked_softmax(s_c, c_mask[None, :, None, None, :])
    o_cmp = jnp.einsum('btgrn,bngd->btgrd', p_c.astype(cv.dtype), cv)
    sk, sv = sel_kv[:, 0], sel_kv[:, 1]
    ns = sk.shape[2]
    imp = jnp.sum(p_c, axis=3)
    imp = jnp.pad(imp, ((0, 0), (0, 0), (0, 0), (0, ns * SLC_PER_CMP - nc)))
    imp = imp.reshape(B, T, NSA_KV_HEADS, ns, SLC_PER_CMP).sum(-1)
    blk = jnp.arange(ns, dtype=jnp.int32)[None, :]
    cur = (qpos // SLC_BLOCK)[:, None]
    forced = ((blk == 0) | (blk == cur) | (blk == cur - 1)).astype(jnp.float32)
    valid = blk <= cur
    score = jnp.where(valid[None, :, None, :], imp + FORCE_BONUS * forced[None, :, None, :], -1.0)
    _, idx = lax.top_k(score, min(N_SELECT, ns))
    n_sel = idx.shape[-1]
    sel_ok = idx <= cur[None, :, :, None]
    idx_g = jnp.transpose(idx, (0, 2, 1, 3))
    bi = jnp.arange(B)[:, None, None, None]
    gi = jnp.arange(NSA_KV_HEADS)[None, :, None, None]
    ks = sk[bi, gi, idx_g]
    vs = sv[bi, gi, idx_g]
    s_s = jnp.einsum('btgrd,bgtknd->btgrkn', q_rot, ks).astype(jnp.float32) * scale
    kpos = idx[..., None] * SLC_BLOCK + jnp.arange(SLC_BLOCK, dtype=jnp.int32)
    s_mask = (kpos <= qpos[None, :, None, None, None]) & sel_ok[..., None]
    s_mask = s_mask.reshape(B, T, NSA_KV_HEADS, 1, n_sel * SLC_BLOCK)
    p_s = masked_softmax(s_s.reshape(B, T, NSA_KV_HEADS, NSA_GROUP, n_sel * SLC_BLOCK), s_mask)
    p_s = p_s.reshape(B, T, NSA_KV_HEADS, NSA_GROUP, n_sel, SLC_BLOCK)
    o_slc = jnp.einsum('btgrkn,bgtknd->btgrd', p_s.astype(vs.dtype), vs)
    s_w = jnp.einsum('btgrd,bkgd->btgrk', q_rot, wkv[:, :, 0]).astype(jnp.float32) * scale
    diff = qpos[:, None] - wpos[None, :]
    w_mask = (diff >= 0) & (diff < WINDOW) & (wpos[None, :] >= 0)
    p_w = masked_softmax(s_w, w_mask[None, :, None, None, :])
    o_win = jnp.einsum('btgrk,bkgd->btgrd', p_w.astype(wkv.dtype), wkv[:, :, 1])
    o = gates[:, :, 0] * o_cmp + gates[:, :, 1] * o_slc + gates[:, :, 2] * o_win
    return o.reshape(B, T, NSA_WIDTH)


def gmlp_mix(uv, ln_g, ln_b, ws, bs):
    B, T, _ = uv.shape
    uv = jax.nn.gelu(uv)
    u, v = uv[..., :GMLP_WIDTH], uv[..., GMLP_WIDTH:]
    v = layer_norm(v, ln_g, ln_b)
    c = min(T, GMLP_CHUNK)
    vc = v.reshape(B, T // c, c, GMLP_GROUPS, GMLP_GROUP_DIM)
    w = jnp.tril(ws[:, :c, :c])
    s = jnp.einsum('hts,bnshd->bnthd', w, vc) + jnp.transpose(bs[:, :c])[None, None, :, :, None]
    return u * s.reshape(B, T, GMLP_WIDTH), v


def prompt_mixer(x, w_in, cmp_pe, cmp_w1, cmp_b1, cmp_w2, gmlp_ln_g, gmlp_ln_b, gmlp_ws, gmlp_bs):
    B, T, _ = x.shape
    pos = jnp.arange(T, dtype=jnp.int32)
    q_raw, q_rot, cmp_kv, slc_kv, win_kv, gates, uv = project(x, pos, w_in)
    ckv = compress_blocks(cmp_kv, cmp_pe, cmp_w1, cmp_b1, cmp_w2)
    sel_kv = selection_blocks(slc_kv)
    win_pad = jnp.pad(win_kv, ((0, 0), (WINDOW, 0), (0, 0), (0, 0), (0, 0)))
    nb = T // QUERY_BLOCK

    def to_blocks(a):
        return jnp.moveaxis(a.reshape((B, nb, QUERY_BLOCK) + a.shape[2:]), 1, 0)

    def attend_block(args):
        i, qr, qo, gt = args
        start = i * QUERY_BLOCK
        qpos = start + jnp.arange(QUERY_BLOCK, dtype=jnp.int32)
        wkv = lax.dynamic_slice_in_dim(win_pad, start, WINDOW + QUERY_BLOCK, axis=1)
        wpos = start - WINDOW + jnp.arange(WINDOW + QUERY_BLOCK, dtype=jnp.int32)
        return nsa_attend(qr, qo, qpos, ckv, sel_kv, wkv, wpos, gt)

    o = lax.map(attend_block, (jnp.arange(nb, dtype=jnp.int32), to_blocks(q_raw), to_blocks(q_rot), to_blocks(gates)))
    o_nsa = jnp.moveaxis(o, 0, 1).reshape(B, T, NSA_WIDTH)
    o_gmlp, _ = gmlp_mix(uv, gmlp_ln_g, gmlp_ln_b, gmlp_ws, gmlp_bs)
    n_keep = min(WINDOW, T)
    return jnp.concatenate([o_nsa, o_gmlp], axis=-1), cmp_kv, slc_kv, win_kv[:, T - n_keep:]


def sample_mixer(x, pool_cmp, pool_slc, win_buf, page_table, w_in, cmp_pe, cmp_w1, cmp_b1, cmp_w2,
                 gmlp_ln_g, gmlp_ln_b, gmlp_ws, gmlp_bs):
    B, S, _ = x.shape
    past = page_table.shape[1] * PAGE_SIZE
    pos = past + jnp.arange(S, dtype=jnp.int32)
    q_raw, q_rot, cmp_kv, slc_kv, win_kv, gates, uv = project(x, pos, w_in)
    kv_tail = (2, NSA_KV_HEADS, HEAD_DIM)
    past_cmp = pool_cmp[page_table].reshape((B, past) + kv_tail)
    past_slc = pool_slc[page_table].reshape((B, past) + kv_tail)
    ckv = compress_blocks(jnp.concatenate([past_cmp, cmp_kv], axis=1), cmp_pe, cmp_w1, cmp_b1, cmp_w2)
    sel_kv = selection_blocks(jnp.concatenate([past_slc, slc_kv], axis=1))
    n_buf = win_buf.shape[1]
    wkv = jnp.concatenate([win_buf, win_kv], axis=1)
    wpos = past - n_buf + jnp.arange(n_buf + S, dtype=jnp.int32)
    o_nsa = nsa_attend(q_raw, q_rot, pos, ckv, sel_kv, wkv, wpos, gates)
    o_gmlp, v_rows = gmlp_mix(uv, gmlp_ln_g, gmlp_ln_b, gmlp_ws, gmlp_bs)
    return jnp.concatenate([o_nsa, o_gmlp], axis=-1), cmp_kv, slc_kv, wkv[:, S:], v_rows


def residual_ffn(x, mix, w_out, ln1_g, ln1_b, w_gate, w_up, w_down, ln2_g, ln2_b):
    h = jnp.einsum('btm,md->btd', mix, w_out)
    x = layer_norm(DEEPNORM_ALPHA * x + h, ln1_g, ln1_b)
    f = jax.nn.silu(jnp.einsum('btd,df->btf', x, w_gate)) * jnp.einsum('btd,df->btf', x, w_up)
    f = jnp.einsum('btf,fd->btd', f, w_down)
    return layer_norm(DEEPNORM_ALPHA * x + f, ln2_g, ln2_b)


def setup_inputs(seed: int = 0) -> dict:
    key = jax.random.key(seed)
    ks = jax.random.split(key, 24)
    n_pages = PAST_LEN // PAGE_SIZE
    n_used = DEC_BATCH * n_pages
    n_pool = n_used + n_used // 4
    w_buf = min(WINDOW, PAST_LEN)

    def nrm(k, shape, scale):
        return jax.random.normal(k, shape, jnp.float32) * scale

    col_scale = np.ones((P_IN,), np.float32)
    for off in (O_CMP, O_SLC, O_WIN):
        col_scale[off + KV_WIDTH: off + 2 * KV_WIDTH] = DEEPNORM_BETA
    kv_page = (DEPTH, n_pool, PAGE_SIZE, 2, NSA_KV_HEADS, HEAD_DIM)
    return {
        'x_prompt': nrm(ks[0], (BATCH, SEQ, D_MODEL), 1.0),
        'x_sample': nrm(ks[1], (DEC_BATCH, DEC_SEQ, D_MODEL), 1.0),
        'cache_cmp_kv': nrm(ks[2], kv_page, 1.0),
        'cache_slc_kv': nrm(ks[3], kv_page, 1.0),
        'state_win_kv': nrm(ks[4], (DEPTH, DEC_BATCH, w_buf, 2, NSA_KV_HEADS, HEAD_DIM), 1.0),
        'page_table': jax.random.permutation(ks[5], n_pool)[:n_used].reshape(DEC_BATCH, n_pages).astype(jnp.int32),
        'w_in': nrm(ks[6], (DEPTH, D_MODEL, P_IN), D_MODEL ** -0.5) * jnp.asarray(col_scale),
        'cmp_pe': nrm(ks[7], (DEPTH, 2, CMP_BLOCK, HEAD_DIM), 0.1),
        'cmp_w1': nrm(ks[8], (DEPTH, 2, CMP_BLOCK, HEAD_DIM, CMP_HIDDEN), (CMP_BLOCK * HEAD_DIM) ** -0.5),
        'cmp_b1': nrm(ks[9], (DEPTH, 2, CMP_HIDDEN), 0.01),
        'cmp_w2': nrm(ks[10], (DEPTH, 2, CMP_HIDDEN, HEAD_DIM), CMP_HIDDEN ** -0.5),
        'gmlp_ln_g': 1.0 + nrm(ks[11], (DEPTH, GMLP_WIDTH), 0.01),
        'gmlp_ln_b': nrm(ks[12], (DEPTH, GMLP_WIDTH), 0.01),
        'gmlp_ws': nrm(ks[13], (DEPTH, GMLP_GROUPS, GMLP_CHUNK, GMLP_CHUNK), GMLP_CHUNK ** -0.5),
        'gmlp_bs': 1.0 + nrm(ks[14], (DEPTH, GMLP_GROUPS, GMLP_CHUNK), 0.01),
        'w_out': nrm(ks[15], (DEPTH, MIX_WIDTH, D_MODEL), MIX_WIDTH ** -0.5 * DEEPNORM_BETA),
        'ln1_g': 1.0 + nrm(ks[16], (DEPTH, D_MODEL), 0.01),
        'ln1_b': nrm(ks[17], (DEPTH, D_MODEL), 0.01),
        'w_gate': nrm(ks[18], (DEPTH, D_MODEL, FFN_HIDDEN), D_MODEL ** -0.5),
        'w_up': nrm(ks[19], (DEPTH, D_MODEL, FFN_HIDDEN), D_MODEL ** -0.5),
        'w_down': nrm(ks[20], (DEPTH, FFN_HIDDEN, D_MODEL), FFN_HIDDEN ** -0.5 * DEEPNORM_BETA),
        'ln2_g': 1.0 + nrm(ks[21], (DEPTH, D_MODEL), 0.01),
        'ln2_b': nrm(ks[22], (DEPTH, D_MODEL), 0.01),
    }


def reference(x_prompt, x_sample, cache_cmp_kv, cache_slc_kv, state_win_kv, page_table,
              w_in, cmp_pe, cmp_w1, cmp_b1, cmp_w2, gmlp_ln_g, gmlp_ln_b, gmlp_ws, gmlp_bs,
              w_out, ln1_g, ln1_b, w_gate, w_up, w_down, ln2_g, ln2_b):
    yp, ys = x_prompt, x_sample
    p_cmp, p_slc, p_win = [], [], []
    s_cmp, s_slc, s_win, s_v = [], [], [], []
    for l in range(DEPTH):
        mix_p, pc, psl, pw = prompt_mixer(yp, w_in[l], cmp_pe[l], cmp_w1[l], cmp_b1[l], cmp_w2[l],
                                          gmlp_ln_g[l], gmlp_ln_b[l], gmlp_ws[l], gmlp_bs[l])
        mix_s, sc, ssl, sw, sv = sample_mixer(ys, cache_cmp_kv[l], cache_slc_kv[l], state_win_kv[l], page_table,
                                              w_in[l], cmp_pe[l], cmp_w1[l], cmp_b1[l], cmp_w2[l],
                                              gmlp_ln_g[l], gmlp_ln_b[l], gmlp_ws[l], gmlp_bs[l])
        yp = residual_ffn(yp, mix_p, w_out[l], ln1_g[l], ln1_b[l], w_gate[l], w_up[l], w_down[l], ln2_g[l], ln2_b[l])
        ys = residual_ffn(ys, mix_s, w_out[l], ln1_g[l], ln1_b[l], w_gate[l], w_up[l], w_down[l], ln2_g[l], ln2_b[l])
        p_cmp.append(pc)
        p_slc.append(psl)
        p_win.append(pw)
        s_cmp.append(sc)
        s_slc.append(ssl)
        s_win.append(sw)
        s_v.append(sv)
    return (yp, ys, jnp.stack(p_cmp), jnp.stack(p_slc), jnp.stack(p_win),
            jnp.stack(s_cmp), jnp.stack(s_slc), jnp.stack(s_win), jnp.stack(s_v))
```

```python
import functools

import numpy as np
import jax
import jax.numpy as jnp
from jax import lax
from jax.experimental import pallas as pl
from jax.experimental.pallas import tpu as pltpu

F32 = jnp.float32
BF16 = jnp.bfloat16

LANES = 128
SUBLANES = 8
HEAD_DIM = 64
NSA_HEADS = 8
KV_HEADS = 2
GROUP = NSA_HEADS // KV_HEADS
NSA_WIDTH = NSA_HEADS * HEAD_DIM
KV_WIDTH = KV_HEADS * HEAD_DIM
GMLP_WIDTH = 512
GMLP_GROUPS = 8
GMLP_CHUNK = 128
CMP_BLOCK = 32
CMP_HIDDEN = 256
SLC_BLOCK = 64
N_SELECT = 16
WINDOW = 512
QUERY_BLOCK = 128
PAGE_SIZE = 128
FORCE_BONUS = 1.0e4
ROPE_THETA = 10000.0
LN_EPS = 1e-5
NEG_INF = -1e30
MAX_SLC_BLOCKS = LANES
SAMPLE_ROWS = SUBLANES
SLC_KEY_TILE = 512
VMEM_LIMIT = 56 * 1024 * 1024

C_Q = 0
C_CMP = C_Q + NSA_WIDTH
C_SLC = C_CMP + 2 * KV_WIDTH
C_WIN = C_SLC + 2 * KV_WIDTH
C_GATE = C_WIN + 2 * KV_WIDTH
C_U = C_GATE + LANES
C_V = C_U + GMLP_WIDTH
C_END = C_V + GMLP_WIDTH


def _dot(a, b):
    return jnp.dot(a, b, preferred_element_type=F32)


def _dot_nt(a, b):
    return lax.dot_general(a, b, (((1,), (1,)), ((), ())), preferred_element_type=F32)


def _layer_norm(x, g, b):
    mu = jnp.mean(x, axis=-1, keepdims=True)
    xc = x - mu
    var = jnp.mean(xc * xc, axis=-1, keepdims=True)
    return xc * lax.rsqrt(var + LN_EPS) * g + b


def _softmax_masked(s, mask):
    s = jnp.where(mask, s, NEG_INF)
    m = jnp.max(s, axis=-1, keepdims=True)
    e = jnp.where(mask, jnp.exp(s - m), 0.0)
    return e * (1.0 / jnp.maximum(jnp.sum(e, axis=-1, keepdims=True), 1e-30))


def _inproj_kernel(x_ref, w_ref, cos_ref, sin_ref, lng_ref, lnb_ref,
                   qraw_ref, qrot_ref, cmp_ref, slc_ref, win_ref, slcb_ref, winb_ref,
                   gate_ref, u_ref, v_ref):
    xb = x_ref[...].astype(BF16)
    cos = cos_ref[...]
    sin = sin_ref[...]
    lane = lax.broadcasted_iota(jnp.int32, cos.shape, 1)
    first_half = (lane % HEAD_DIM) < (HEAD_DIM // 2)

    def rope(c):
        partner = jnp.where(first_half, pltpu.roll(c, LANES - HEAD_DIM // 2, 1),
                            pltpu.roll(c, HEAD_DIM // 2, 1))
        return c * cos + partner * sin

    scale = HEAD_DIM ** -0.5
    zq = _dot(xb, w_ref[:, C_Q:C_CMP])
    qraw_ref[...] = zq * scale
    for c in range(NSA_WIDTH // LANES):
        qrot_ref[:, c * LANES:(c + 1) * LANES] = rope(zq[:, c * LANES:(c + 1) * LANES]) * scale

    cmp_ref[...] = _dot(xb, w_ref[:, C_CMP:C_SLC])

    for col, f_ref, b_ref in ((C_SLC, slc_ref, slcb_ref), (C_WIN, win_ref, winb_ref)):
        z = _dot(xb, w_ref[:, col:col + 2 * KV_WIDTH])
        k = rope(z[:, :KV_WIDTH])
        v = z[:, KV_WIDTH:]
        f_ref[:, :KV_WIDTH] = k
        f_ref[:, KV_WIDTH:] = v
        b_ref[:, :KV_WIDTH] = k.astype(BF16)
        b_ref[:, KV_WIDTH:] = v.astype(BF16)

    gate_ref[...] = jax.nn.sigmoid(_dot(xb, w_ref[:, C_GATE:C_U]))
    u_ref[...] = jax.nn.gelu(_dot(xb, w_ref[:, C_U:C_V]))
    zv = jax.nn.gelu(_dot(xb, w_ref[:, C_V:C_END]))
    v_ref[...] = _layer_norm(zv, lng_ref[...], lnb_ref[...])


def _inproj(x, w, cos, sin, ln_g, ln_b, tm):
    n, d = x.shape
    n_tab = cos.shape[0] // tm
    row = lambda width: pl.BlockSpec((tm, width), lambda i: (i, 0))
    const = lambda a: pl.BlockSpec(a.shape, lambda i: (0,) * a.ndim)
    tab = pl.BlockSpec((tm, LANES), lambda i: (i % n_tab, 0))
    widths = (NSA_WIDTH, NSA_WIDTH, 2 * KV_WIDTH, 2 * KV_WIDTH, 2 * KV_WIDTH,
              2 * KV_WIDTH, 2 * KV_WIDTH, LANES, GMLP_WIDTH, GMLP_WIDTH)
    dtypes = (F32, F32, F32, F32, F32, BF16, BF16, F32, F32, F32)
    return pl.pallas_call(
        _inproj_kernel,
        grid=(n // tm,),
        in_specs=[row(d), const(w), tab, tab, const(ln_g), const(ln_b)],
        out_specs=[row(wd) for wd in widths],
        out_shape=[jax.ShapeDtypeStruct((n, wd), dt) for wd, dt in zip(widths, dtypes)],
        compiler_params=pltpu.CompilerParams(dimension_semantics=("parallel",),
                                             vmem_limit_bytes=VMEM_LIMIT),
        name="inproj",
    )(x, w, cos, sin, ln_g, ln_b)


def _compress_rows(xs, pe_ref, w1_ref, b1_ref, w2_ref):
    outs = []
    row_w = 2 * KV_WIDTH
    for k in range(2):
        a = jnp.concatenate(
            [xs[:, l * row_w + k * KV_WIDTH:l * row_w + (k + 1) * KV_WIDTH] for l in range(CMP_BLOCK)],
            axis=1)
        a = (a + pe_ref[k]).astype(BF16)
        h = jax.nn.silu(_dot(a, w1_ref[k]) + b1_ref[k])
        outs.append(_dot(h.astype(BF16), w2_ref[k]))
    return jnp.concatenate(outs, axis=1)


def _compress_prompt_kernel(x_ref, pe_ref, w1_ref, b1_ref, w2_ref, o_ref):
    o_ref[...] = _compress_rows(x_ref[...], pe_ref, w1_ref, b1_ref, w2_ref)


def _compress_prompt(x, pe, w1, b1, w2, tm):
    n = x.shape[0]
    const = lambda a: pl.BlockSpec(a.shape, lambda i: (0,) * a.ndim)
    return pl.pallas_call(
        _compress_prompt_kernel,
        grid=(n // tm,),
        in_specs=[pl.BlockSpec((tm, x.shape[1]), lambda i: (i, 0)),
                  const(pe), const(w1), const(b1), const(w2)],
        out_specs=pl.BlockSpec((tm, 2 * KV_WIDTH), lambda i: (i, 0)),
        out_shape=jax.ShapeDtypeStruct((n, 2 * KV_WIDTH), F32),
        compiler_params=pltpu.CompilerParams(dimension_semantics=("parallel",),
                                             vmem_limit_bytes=VMEM_LIMIT),
        name="compress_prompt",
    )(x, pe, w1, b1, w2)


def _compress_sample_kernel(n_seq, n_pages, pt_ref, *refs):
    page_refs = refs[:n_seq * n_pages]
    pe_ref, w1_ref, b1_ref, w2_ref, o_ref, rows_ref = refs[n_seq * n_pages:]
    blocks_per_page = PAGE_SIZE // CMP_BLOCK
    for j, page_ref in enumerate(page_refs):
        rows_ref[j * blocks_per_page:(j + 1) * blocks_per_page, :] = page_ref[...]
    o_ref[...] = _compress_rows(rows_ref[...], pe_ref, w1_ref, b1_ref, w2_ref)


def _compress_sample(pool, page_table, pe, w1, b1, w2, n_seq):
    db, n_pages = page_table.shape
    blocks_per_page = PAGE_SIZE // CMP_BLOCK
    rows = n_seq * n_pages * blocks_per_page
    const = lambda a: pl.BlockSpec(a.shape, lambda i, pt: (0,) * a.ndim)

    def page_spec(s, p):
        return pl.BlockSpec((None,) + pool.shape[1:], lambda i, pt: (pt[i * n_seq + s, p], 0, 0))

    grid_spec = pltpu.PrefetchScalarGridSpec(
        num_scalar_prefetch=1,
        grid=(db // n_seq,),
        in_specs=[page_spec(s, p) for s in range(n_seq) for p in range(n_pages)]
        + [const(pe), const(w1), const(b1), const(w2)],
        out_specs=pl.BlockSpec((rows, 2 * KV_WIDTH), lambda i, pt: (i, 0)),
        scratch_shapes=[pltpu.VMEM((rows, pool.shape[2]), F32)],
    )
    return pl.pallas_call(
        functools.partial(_compress_sample_kernel, n_seq, n_pages),
        grid_spec=grid_spec,
        out_shape=jax.ShapeDtypeStruct((db * n_pages * blocks_per_page, 2 * KV_WIDTH), F32),
        compiler_params=pltpu.CompilerParams(dimension_semantics=("parallel",),
                                             vmem_limit_bytes=VMEM_LIMIT),
        name="compress_sample",
    )(page_table, *([pool] * (n_seq * n_pages)), pe, w1, b1, w2)


def _stack_heads(q, tq):
    lane_lo = lax.broadcasted_iota(jnp.int32, (tq, LANES), 1) < HEAD_DIM
    pieces = []
    for hh in range(NSA_HEADS):
        g = hh // GROUP
        c = hh // 2
        chunk = q[:, c * LANES:(c + 1) * LANES]
        if hh % 2 != g:
            chunk = pltpu.roll(chunk, HEAD_DIM, 1)
        pieces.append(jnp.where(lane_lo if g == 0 else jnp.logical_not(lane_lo), chunk, 0.0))
    return jnp.concatenate(pieces, axis=0).astype(BF16)


def _cmp_branch(qraw, ckv, qpos, tq):
    nb = MAX_SLC_BLOCKS
    kc = jnp.concatenate([ckv[:, 0:KV_WIDTH], ckv[:, 2 * KV_WIDTH:3 * KV_WIDTH]], axis=0).astype(BF16)
    vc = jnp.concatenate([ckv[:, KV_WIDTH:2 * KV_WIDTH], ckv[:, 3 * KV_WIDTH:]], axis=0).astype(BF16)
    s = _dot_nt(qraw, kc).reshape(NSA_HEADS, tq, 2 * nb)
    lane = lax.broadcasted_iota(jnp.int32, (tq, 2 * nb), 1)
    blk = jnp.where(lane < nb, 2 * lane, 2 * (lane - nb) + 1)
    mask = ((blk + 1) * CMP_BLOCK - 1) <= qpos
    p = _softmax_masked(s, mask[None])
    o_cmp = _dot(p.reshape(NSA_HEADS * tq, 2 * nb).astype(BF16), vc)
    scores = []
    sblk = lax.broadcasted_iota(jnp.int32, (tq, nb), 1)
    cur = qpos // SLC_BLOCK
    valid = sblk <= cur
    forced = ((sblk == 0) | (sblk == cur) | (sblk == cur - 1)).astype(F32)
    for g in range(KV_HEADS):
        imp = p[g * GROUP]
        for r in range(1, GROUP):
            imp = imp + p[g * GROUP + r]
        imp = imp[:, :nb] + imp[:, nb:]
        scores.append(jnp.where(valid, imp + FORCE_BONUS * forced, -1.0))
    return o_cmp, scores, valid


def _rank_blocks_wide(score, st_ref, cnt_ref, n_chunks):
    st_ref[...] = score.T
    cnt_ref[...] = jnp.zeros_like(cnt_ref)
    sub = lax.broadcasted_iota(jnp.int32, (SUBLANES, LANES), 0)
    n_vregs = MAX_SLC_BLOCKS // SUBLANES
    for c in range(n_vregs):
        @pl.when(c < n_chunks)
        def _():
            tiles = [st_ref[v * SUBLANES:(v + 1) * SUBLANES, :] for v in range(n_vregs)]
            cnts = [cnt_ref[v * SUBLANES:(v + 1) * SUBLANES, :] for v in range(n_vregs)]
            for ii in range(SUBLANES):
                row = tiles[c][ii:ii + 1, :]
                for v in range(n_vregs):
                    if v < c:
                        inc = jnp.where(row > tiles[v], 1.0, 0.0)
                    elif v > c:
                        inc = jnp.where(row >= tiles[v], 1.0, 0.0)
                    else:
                        inc = jnp.where(sub > ii, jnp.where(row >= tiles[v], 1.0, 0.0),
                                        jnp.where(row > tiles[v], 1.0, 0.0))
                    cnts[v] = cnts[v] + inc
            for v in range(n_vregs):
                cnt_ref[v * SUBLANES:(v + 1) * SUBLANES, :] = cnts[v]
    return cnt_ref[...].T


def _rank_blocks_narrow(score, n_blocks):
    lane = lax.broadcasted_iota(jnp.int32, score.shape, 1)
    cnt = jnp.zeros(score.shape, F32)
    for i in range(n_blocks):
        col = score[:, i:i + 1]
        cnt = cnt + jnp.where(lane > i, jnp.where(col >= score, 1.0, 0.0),
                              jnp.where(col > score, 1.0, 0.0))
    return cnt


def _slc_branch(q_aug, kv_ref, et_ref, n_tiles, qpos, tq, tk):
    rows = NSA_HEADS * tq

    def body(kt, carry):
        m, l, acc = carry
        off = pl.multiple_of(kt * tk, tk)
        kv = kv_ref[pl.ds(off, tk), :]
        ka = jnp.concatenate([et_ref[pl.ds(off, tk), :], kv[:, :KV_WIDTH]], axis=1)
        s = _dot_nt(q_aug, ka).reshape(NSA_HEADS, tq, tk)
        kpos = off + lax.broadcasted_iota(jnp.int32, (tq, tk), 1)
        s = jnp.where((kpos <= qpos)[None], s, NEG_INF)
        m_new = jnp.maximum(m, jnp.max(s, axis=-1, keepdims=True))
        alpha = jnp.exp(m - m_new)
        p = jnp.exp(s - m_new)
        l = alpha * l + jnp.sum(p, axis=-1, keepdims=True)
        pv = _dot(p.reshape(rows, tk).astype(BF16), kv[:, KV_WIDTH:])
        acc = alpha.reshape(rows, 1) * acc + pv
        return m_new, l, acc

    init = (jnp.full((NSA_HEADS, tq, 1), NEG_INF, F32), jnp.zeros((NSA_HEADS, tq, 1), F32),
            jnp.zeros((rows, KV_WIDTH), F32))
    m, l, acc = lax.fori_loop(0, n_tiles, body, init)
    return acc * (1.0 / jnp.maximum(l, 1e-30)).reshape(rows, 1)


def _win_branch(qrot, kw, kpos, qpos, tq):
    nk = kw.shape[0]
    s = _dot_nt(qrot, kw[:, :KV_WIDTH]).reshape(NSA_HEADS, tq, nk)
    diff = qpos - kpos
    mask = (diff >= 0) & (diff < WINDOW)
    p = _softmax_masked(s, mask[None])
    return _dot(p.reshape(NSA_HEADS * tq, nk).astype(BF16), kw[:, KV_WIDTH:])


def _combine(o_cmp, o_slc, o_win, gates, tq):
    lane_lo = lax.broadcasted_iota(jnp.int32, (tq, LANES), 1) < HEAD_DIM
    heads = []
    for hh in range(NSA_HEADS):
        tot = None
        for br, o in enumerate((o_cmp, o_slc, o_win)):
            col = br * NSA_HEADS + hh
            term = gates[:, col:col + 1] * o[hh * tq:(hh + 1) * tq, :]
            tot = term if tot is None else tot + term
        heads.append(tot)
    chunks = []
    for c in range(NSA_HEADS // 2):
        a, b = heads[2 * c], heads[2 * c + 1]
        if c // 2 == 0:
            chunks.append(jnp.where(lane_lo, a, pltpu.roll(b, HEAD_DIM, 1)))
        else:
            chunks.append(jnp.where(lane_lo, pltpu.roll(a, HEAD_DIM, 1), b))
    return jnp.concatenate(chunks, axis=1)


def _selection_bias(rank, valid):
    return jnp.where((rank < float(N_SELECT)) & valid, 0.0, NEG_INF).astype(BF16)


def _pad_ckv(ckv):
    n = ckv.shape[0]
    if n == MAX_SLC_BLOCKS:
        return ckv
    return jnp.concatenate([ckv, jnp.zeros((MAX_SLC_BLOCKS - n, ckv.shape[1]), F32)], axis=0)


def _nsa_prompt_kernel(qraw_ref, qrot_ref, gate_ref, ckv_ref, slc_ref, win_ref, et_ref,
                       o_ref, st_ref, cnt_ref):
    tq = QUERY_BLOCK
    i = pl.program_id(1)
    qpos0 = i * tq
    qpos = qpos0 + lax.broadcasted_iota(jnp.int32, (tq, 1), 0)
    qraw = _stack_heads(qraw_ref[...], tq)
    qrot = _stack_heads(qrot_ref[...], tq)

    o_cmp, scores, valid = _cmp_branch(qraw, _pad_ckv(ckv_ref[...]), qpos, tq)
    n_chunks = (qpos0 + tq - 1) // (SLC_BLOCK * SUBLANES) + 1
    bias = []
    for g in range(KV_HEADS):
        rank = _rank_blocks_wide(scores[g], st_ref, cnt_ref, n_chunks)
        bias.append(_selection_bias(rank, valid))
    bias_rows = jnp.concatenate([bias[0]] * GROUP + [bias[1]] * GROUP, axis=0)
    q_aug = jnp.concatenate([bias_rows, qrot], axis=1)
    n_tiles = (qpos0 + tq - 1) // SLC_KEY_TILE + 1
    o_slc = _slc_branch(q_aug, slc_ref, et_ref, n_tiles, qpos, tq, SLC_KEY_TILE)

    nk = WINDOW + tq
    start = pl.multiple_of(jnp.maximum(qpos0 - WINDOW, 0), tq)
    kpos = start + lax.broadcasted_iota(jnp.int32, (1, nk), 1)
    o_win = _win_branch(qrot, win_ref[pl.ds(start, nk), :], kpos, qpos, tq)

    o_ref[...] = _combine(o_cmp, o_slc, o_win, gate_ref[...], tq)


def _nsa_prompt(qraw, qrot, gates, ckv, slcb, winb, et):
    b, t, _ = qraw.shape
    tq = QUERY_BLOCK
    blk = lambda width: pl.BlockSpec((None, tq, width), lambda bi, i: (bi, i, 0))
    per_batch = lambda a: pl.BlockSpec((None,) + a.shape[1:], lambda bi, i: (bi, 0, 0))
    return pl.pallas_call(
        _nsa_prompt_kernel,
        grid=(b, t // tq),
        in_specs=[blk(NSA_WIDTH), blk(NSA_WIDTH), blk(LANES), per_batch(ckv), per_batch(slcb),
                  per_batch(winb), pl.BlockSpec(et.shape, lambda bi, i: (0, 0))],
        out_specs=blk(NSA_WIDTH),
        out_shape=jax.ShapeDtypeStruct((b, t, NSA_WIDTH), F32),
        scratch_shapes=[pltpu.VMEM((MAX_SLC_BLOCKS, tq), F32), pltpu.VMEM((MAX_SLC_BLOCKS, tq), F32)],
        compiler_params=pltpu.CompilerParams(dimension_semantics=("parallel", "arbitrary"),
                                             vmem_limit_bytes=VMEM_LIMIT),
        name="nsa_prompt",
    )(qraw, qrot, gates, ckv, slcb, winb, et)


def _nsa_sample_kernel(n_seq, n_pages, past, pt_ref, *refs):
    page_refs = refs[:n_seq * n_pages]
    (qraw_ref, qrot_ref, gate_ref, ckv_ref, slcn_ref, wins_ref, winn_ref, et_ref,
     o_ref) = refs[n_seq * n_pages:n_seq * n_pages + 9]
    bufs = refs[n_seq * n_pages + 9:]
    tq = SAMPLE_ROWS
    qpos = past + lax.broadcasted_iota(jnp.int32, (tq, 1), 0)
    n_buf = wins_ref.shape[1]
    n_blocks = (past + tq - 1) // SLC_BLOCK + 1
    n_tiles = (past + tq - 1) // SLC_KEY_TILE + 1
    new_rows = 2 * tq
    n_keys, width = bufs[0].shape
    n_win = bufs[1].shape[0]
    wpos = past - n_buf + lax.broadcasted_iota(jnp.int32, (1, n_win), 1)
    zero_rows = jnp.zeros((new_rows - tq, width), F32)
    for s in range(n_seq):
        slc_buf, win_buf = bufs[2 * s], bufs[2 * s + 1]
        for p in range(n_pages):
            slc_buf[p * PAGE_SIZE:(p + 1) * PAGE_SIZE, :] = page_refs[s * n_pages + p][...].astype(BF16)
        slc_buf[past:past + new_rows, :] = jnp.concatenate([slcn_ref[s], zero_rows], axis=0).astype(BF16)
        slc_buf[past + new_rows:, :] = jnp.zeros((n_keys - past - new_rows, width), BF16)
        win_buf[:n_buf, :] = wins_ref[s].astype(BF16)
        win_buf[n_buf:n_buf + new_rows, :] = jnp.concatenate([winn_ref[s], zero_rows], axis=0).astype(BF16)
        win_buf[n_buf + new_rows:, :] = jnp.zeros((n_win - n_buf - new_rows, width), BF16)

        qraw = _stack_heads(qraw_ref[s], tq)
        qrot = _stack_heads(qrot_ref[s], tq)
        o_cmp, scores, valid = _cmp_branch(qraw, _pad_ckv(ckv_ref[s]), qpos, tq)
        bias = [_selection_bias(_rank_blocks_narrow(scores[g], n_blocks), valid)
                for g in range(KV_HEADS)]
        bias_rows = jnp.concatenate([bias[0]] * GROUP + [bias[1]] * GROUP, axis=0)
        q_aug = jnp.concatenate([bias_rows, qrot], axis=1)
        o_slc = _slc_branch(q_aug, slc_buf, et_ref, n_tiles, qpos, tq, SLC_KEY_TILE)
        o_win = _win_branch(qrot, win_buf[...], wpos, qpos, tq)
        o_ref[s] = _combine(o_cmp, o_slc, o_win, gate_ref[s], tq)


def _nsa_sample(pool, page_table, qraw, qrot, gates, ckv, slc_new, win_state, win_new, et, n_seq):
    db, n_pages = page_table.shape
    past = n_pages * PAGE_SIZE
    tq = SAMPLE_ROWS
    n_buf = win_state.shape[1]
    n_keys = ((past + tq - 1) // SLC_KEY_TILE + 1) * SLC_KEY_TILE
    seq = lambda a: pl.BlockSpec((n_seq,) + a.shape[1:], lambda i, pt: (i, 0, 0))

    def page_spec(s, p):
        return pl.BlockSpec((None,) + pool.shape[1:], lambda i, pt: (pt[i * n_seq + s, p], 0, 0))

    per_seq = (qraw, qrot, gates, ckv, slc_new, win_state, win_new)
    grid_spec = pltpu.PrefetchScalarGridSpec(
        num_scalar_prefetch=1,
        grid=(db // n_seq,),
        in_specs=[page_spec(s, p) for s in range(n_seq) for p in range(n_pages)]
        + [seq(a) for a in per_seq] + [pl.BlockSpec(et.shape, lambda i, pt: (0, 0))],
        out_specs=pl.BlockSpec((n_seq, tq, NSA_WIDTH), lambda i, pt: (i, 0, 0)),
        scratch_shapes=[pltpu.VMEM((n_keys, 2 * KV_WIDTH), BF16),
                        pltpu.VMEM((n_buf + LANES, 2 * KV_WIDTH), BF16)] * n_seq,
    )
    return pl.pallas_call(
        functools.partial(_nsa_sample_kernel, n_seq, n_pages, past),
        grid_spec=grid_spec,
        out_shape=jax.ShapeDtypeStruct((db, tq, NSA_WIDTH), F32),
        compiler_params=pltpu.CompilerParams(dimension_semantics=("parallel",),
                                             vmem_limit_bytes=VMEM_LIMIT),
        name="nsa_sample",
    )(page_table, *([pool] * (n_seq * n_pages)), *per_seq, et)


def _post_kernel(chunk, alpha, n_ffn_chunks, x_ref, o_ref, u_ref, v_ref, gw_ref, gb_ref,
                 wout_ref, ln1g_ref, ln1b_ref, wg_ref, wu_ref, wd_ref, ln2g_ref, ln2b_ref, y_ref):
    tm = x_ref.shape[0]
    c = GMLP_CHUNK
    row = lax.broadcasted_iota(jnp.int32, (c, c), 0)
    col = lax.broadcasted_iota(jnp.int32, (c, c), 1)
    causal = (col <= row) & ((row // chunk) == (col // chunk))
    w_cat = jnp.concatenate([jnp.where(causal, gw_ref[h], 0.0) for h in range(GMLP_GROUPS)],
                            axis=1).astype(BF16)
    lane_group = lax.broadcasted_iota(jnp.int32, (c, GMLP_WIDTH), 1) // (GMLP_WIDTH // GMLP_GROUPS)
    mixed = []
    for sub in range(tm // c):
        v = v_ref[sub * c:(sub + 1) * c, :]
        v_diag = jnp.concatenate([jnp.where(lane_group == h, v, 0.0) for h in range(GMLP_GROUPS)],
                                 axis=0).astype(BF16)
        s = _dot(w_cat, v_diag) + gb_ref[...]
        mixed.append(u_ref[sub * c:(sub + 1) * c, :] * s)
    o_gmlp = jnp.concatenate(mixed, axis=0)
    mix = jnp.concatenate([o_ref[...], o_gmlp], axis=1).astype(BF16)
    h = _dot(mix, wout_ref[...])
    x1 = _layer_norm(alpha * x_ref[...] + h, ln1g_ref[...], ln1b_ref[...])
    x1b = x1.astype(BF16)
    f = None
    fc = wg_ref.shape[1] // n_ffn_chunks
    for j in range(n_ffn_chunks):
        gate = _dot(x1b, wg_ref[:, j * fc:(j + 1) * fc])
        up = _dot(x1b, wu_ref[:, j * fc:(j + 1) * fc])
        a = (jax.nn.silu(gate) * up).astype(BF16)
        part = _dot(a, wd_ref[j * fc:(j + 1) * fc, :])
        f = part if f is None else f + part
    y_ref[...] = _layer_norm(alpha * x1 + f, ln2g_ref[...], ln2b_ref[...])


def _post(x, o_nsa, u, v, gw, gb, wout, ln1g, ln1b, wg, wu, wd, ln2g, ln2b, chunk, alpha, tm):
    n, d = x.shape
    ffn = wg.shape[1]
    n_ffn_chunks = next(k for k in (4, 2, 1) if ffn % (k * LANES) == 0)
    row = lambda width: pl.BlockSpec((tm, width), lambda i: (i, 0))
    const = lambda a: pl.BlockSpec(a.shape, lambda i: (0,) * a.ndim, pipeline_mode=pl.Buffered(1))
    consts = (gw, gb, wout, ln1g, ln1b, wg, wu, wd, ln2g, ln2b)
    return pl.pallas_call(
        functools.partial(_post_kernel, chunk, alpha, n_ffn_chunks),
        grid=(n // tm,),
        in_specs=[row(d), row(NSA_WIDTH), row(GMLP_WIDTH), row(GMLP_WIDTH)] + [const(a) for a in consts],
        out_specs=row(d),
        out_shape=jax.ShapeDtypeStruct((n, d), F32),
        compiler_params=pltpu.CompilerParams(dimension_semantics=("parallel",),
                                             vmem_limit_bytes=VMEM_LIMIT),
        name="post",
    )(x, o_nsa, u, v, *consts)


def _pack_w_in(w_in):
    d = w_in.shape[0]
    o_cmp = NSA_WIDTH
    o_gate = o_cmp + 3 * 2 * KV_WIDTH
    o_gmlp = o_gate + 3 * NSA_HEADS
    gate = jnp.pad(w_in[:, o_gate:o_gmlp], ((0, 0), (0, LANES - 3 * NSA_HEADS)))
    return jnp.concatenate([w_in[:, :o_gate], gate, w_in[:, o_gmlp:]], axis=1).astype(BF16)


def _pack_compress(pe, w1, b1, w2):
    eye = jnp.eye(KV_HEADS, dtype=F32)
    w1p = jnp.einsum('kldh,gG->klgdGh', w1, eye).reshape(2, CMP_BLOCK * KV_WIDTH, KV_HEADS * CMP_HIDDEN)
    pep = jnp.broadcast_to(pe[:, :, None, :], (2, CMP_BLOCK, KV_HEADS, HEAD_DIM)).reshape(2, 1, -1)
    b1p = jnp.tile(b1, (1, KV_HEADS))[:, None, :]
    w2p = jnp.einsum('khd,gG->kghGd', w2, eye).reshape(2, KV_HEADS * CMP_HIDDEN, KV_WIDTH)
    return pep, w1p.astype(BF16), b1p, w2p.astype(BF16)


def _pack_gmlp(ws, bs, chunk):
    reps = GMLP_CHUNK // chunk
    gw = jnp.tile(ws[:, :chunk, :chunk], (1, reps, reps))
    gb = jnp.tile(jnp.repeat(bs[:, :chunk].T, GMLP_WIDTH // GMLP_GROUPS, axis=1), (reps, 1))
    return gw, gb


def _rope_tables(pos):
    half = HEAD_DIM // 2
    inv_freq = ROPE_THETA ** (-jnp.arange(half, dtype=F32) / half)
    ang = pos.astype(F32)[:, None] * inv_freq[None, :]
    cos, sin = jnp.cos(ang), jnp.sin(ang)
    reps = LANES // HEAD_DIM
    return jnp.tile(jnp.concatenate([cos, cos], axis=1), (1, reps)), \
        jnp.tile(jnp.concatenate([-sin, sin], axis=1), (1, reps))


def _block_indicator(n_keys):
    key_block = np.arange(n_keys)[:, None] // SLC_BLOCK
    return jnp.asarray(key_block == np.arange(MAX_SLC_BLOCKS)[None, :], dtype=BF16)


def kernel(x_prompt, x_sample, cache_cmp_kv, cache_slc_kv, state_win_kv, page_table,
           w_in, cmp_pe, cmp_w1, cmp_b1, cmp_w2, gmlp_ln_g, gmlp_ln_b, gmlp_ws, gmlp_bs,
           w_out, ln1_g, ln1_b, w_gate, w_up, w_down, ln2_g, ln2_b):
    depth = w_in.shape[0]
    b, t, d = x_prompt.shape
    db, s_new, _ = x_sample.shape
    n_pages = page_table.shape[1]
    past = n_pages * PAGE_SIZE
    n_pool = cache_cmp_kv.shape[1]
    n_buf = state_win_kv.shape[2]
    kv_tail = (2, KV_HEADS, HEAD_DIM)
    rows_s = SAMPLE_ROWS
    assert t % SLC_KEY_TILE == 0 and t >= WINDOW + QUERY_BLOCK and t // SLC_BLOCK <= MAX_SLC_BLOCKS
    assert s_new <= rows_s and (past + rows_s - 1) // SLC_BLOCK < MAX_SLC_BLOCKS
    assert past % CMP_BLOCK == 0 and past % CMP_BLOCK + s_new < CMP_BLOCK
    alpha = (2 * depth) ** 0.25

    tm_p = 512
    tm_s = min(512, db * rows_s)
    seq_per_step = 2 if db % 2 == 0 else 1
    cos_p, sin_p = _rope_tables(jnp.arange(t, dtype=jnp.int32))
    cos_s, sin_s = _rope_tables(past + jnp.arange(rows_s, dtype=jnp.int32))
    cos_s = jnp.tile(cos_s, (tm_s // rows_s, 1))
    sin_s = jnp.tile(sin_s, (tm_s // rows_s, 1))
    et_p = _block_indicator(t)
    et_s = _block_indicator(((past + rows_s - 1) // SLC_KEY_TILE + 1) * SLC_KEY_TILE)

    yp = x_prompt.reshape(b * t, d)
    ys = jnp.pad(x_sample, ((0, 0), (0, rows_s - s_new), (0, 0))).reshape(db * rows_s, d)
    outs = {k: [] for k in ("p_cmp", "p_slc", "p_win", "s_cmp", "s_slc", "s_win", "s_v")}
    n_keep = min(WINDOW, t)
    cmp_tile = min(128, b * t // CMP_BLOCK)

    for l in range(depth):
        w_in_p = _pack_w_in(w_in[l])
        pe_p, w1_p, b1_p, w2_p = _pack_compress(cmp_pe[l], cmp_w1[l], cmp_b1[l], cmp_w2[l])
        ln_g = gmlp_ln_g[l][None, :]
        ln_b = gmlp_ln_b[l][None, :]
        tail = (w_out[l].astype(BF16), ln1_g[l][None, :], ln1_b[l][None, :], w_gate[l].astype(BF16),
                w_up[l].astype(BF16), w_down[l].astype(BF16), ln2_g[l][None, :], ln2_b[l][None, :])

        qraw, qrot, cmp_kv, slc_kv, win_kv, slcb, winb, gates, u, v = _inproj(
            yp, w_in_p, cos_p, sin_p, ln_g, ln_b, tm_p)
        ckv = _compress_prompt(cmp_kv.reshape(b * t // CMP_BLOCK, CMP_BLOCK * 2 * KV_WIDTH),
                               pe_p, w1_p, b1_p, w2_p, cmp_tile)
        r3 = lambda a: a.reshape(b, t, a.shape[-1])
        o_nsa = _nsa_prompt(r3(qraw), r3(qrot), r3(gates), ckv.reshape(b, t // SLC_BLOCK, 4 * KV_WIDTH),
                            r3(slcb), r3(winb), et_p)
        gw, gb = _pack_gmlp(gmlp_ws[l], gmlp_bs[l], GMLP_CHUNK)
        yp = _post(yp, o_nsa.reshape(b * t, NSA_WIDTH), u, v, gw, gb, *tail, GMLP_CHUNK, alpha, tm_p)
        outs["p_cmp"].append(cmp_kv.reshape((b, t) + kv_tail))
        outs["p_slc"].append(slc_kv.reshape((b, t) + kv_tail))
        outs["p_win"].append(win_kv.reshape((b, t) + kv_tail)[:, t - n_keep:])

        qraw, qrot, cmp_kv, slc_kv, win_kv, _, _, gates, u, v = _inproj(
            ys, w_in_p, cos_s, sin_s, ln_g, ln_b, tm_s)
        ckv = _compress_sample(
            cache_cmp_kv[l].reshape(n_pool, PAGE_SIZE // CMP_BLOCK, CMP_BLOCK * 2 * KV_WIDTH),
            page_table, pe_p, w1_p, b1_p, w2_p, seq_per_step)
        s3 = lambda a: a.reshape(db, rows_s, a.shape[-1])
        win_state = state_win_kv[l].reshape(db, n_buf, 2 * KV_WIDTH)
        o_nsa = _nsa_sample(cache_slc_kv[l].reshape(n_pool, PAGE_SIZE, 2 * KV_WIDTH), page_table,
                            s3(qraw), s3(qrot), s3(gates),
                            ckv.reshape(db, past // SLC_BLOCK, 4 * KV_WIDTH),
                            s3(slc_kv), win_state, s3(win_kv), et_s, seq_per_step)
        gw, gb = _pack_gmlp(gmlp_ws[l], gmlp_bs[l], rows_s)
        ys = _post(ys, o_nsa.reshape(db * rows_s, NSA_WIDTH), u, v, gw, gb, *tail, rows_s, alpha, tm_s)
        new = lambda a: a.reshape((db, rows_s) + a.shape[1:])[:, :s_new]
        outs["s_cmp"].append(new(cmp_kv).reshape((db, s_new) + kv_tail))
        outs["s_slc"].append(new(slc_kv).reshape((db, s_new) + kv_tail))
        wkv = jnp.concatenate([win_state, new(win_kv)], axis=1)[:, s_new:]
        outs["s_win"].append(wkv.reshape((db, n_buf) + kv_tail))
        outs["s_v"].append(new(v))

    y_sample = ys.reshape(db, rows_s, d)[:, :s_new]
    return (yp.reshape(b, t, d), y_sample, jnp.stack(outs["p_cmp"]), jnp.stack(outs["p_slc"]),
            jnp.stack(outs["p_win"]), jnp.stack(outs["s_cmp"]), jnp.stack(outs["s_slc"]),
            jnp.stack(outs["s_win"]), jnp.stack(outs["s_v"]))
```

```python
import functools
import math

import numpy as np
import jax
import jax.numpy as jnp
from jax import lax
from jax.experimental import pallas as pl
from jax.experimental.pallas import tpu as pltpu

F32 = jnp.float32
BF16 = jnp.bfloat16

LANES = 128
SUBLANES = 8
HEAD_DIM = 64
NSA_HEADS = 8
KV_HEADS = 2
GROUP = NSA_HEADS // KV_HEADS
NSA_WIDTH = NSA_HEADS * HEAD_DIM
KV_WIDTH = KV_HEADS * HEAD_DIM
GMLP_WIDTH = 512
GMLP_GROUPS = 8
GMLP_CHUNK = 128
CMP_BLOCK = 32
CMP_HIDDEN = 256
SLC_BLOCK = 64
N_SELECT = 16
WINDOW = 512
QUERY_BLOCK = 128
PAGE_SIZE = 128
FORCE_BONUS = 1.0e4
ROPE_THETA = 10000.0
LN_EPS = 1e-5
NEG_INF = -1e30
MAX_SLC_BLOCKS = LANES
SAMPLE_ROWS = SUBLANES
SLC_KEY_TILE = 512
VMEM_LIMIT = 56 * 1024 * 1024
Q_SCALE = HEAD_DIM ** -0.5 * math.log2(math.e)

C_Q = 0
C_CMP = C_Q + NSA_WIDTH
C_SLC = C_CMP + 2 * KV_WIDTH
C_WIN = C_SLC + 2 * KV_WIDTH
C_GATE = C_WIN + 2 * KV_WIDTH
C_U = C_GATE + LANES
C_V = C_U + GMLP_WIDTH
C_END = C_V + GMLP_WIDTH


def _dot(a, b):
    return jnp.dot(a, b, preferred_element_type=F32)


def _dot_nt(a, b):
    return lax.dot_general(a, b, (((1,), (1,)), ((), ())), preferred_element_type=F32)


def _layer_norm(x, g, b):
    mu = jnp.mean(x, axis=-1, keepdims=True)
    xc = x - mu
    var = jnp.mean(xc * xc, axis=-1, keepdims=True)
    return xc * lax.rsqrt(var + LN_EPS) * g + b


def _softmax_masked(s, mask):
    s = jnp.where(mask, s, NEG_INF)
    m = jnp.max(s, axis=-1, keepdims=True)
    e = jnp.where(mask, jnp.exp2(s - m), 0.0)
    return e * (1.0 / jnp.maximum(jnp.sum(e, axis=-1, keepdims=True), 1e-30))


def _rope_lanes(c, cos, sin, first_half):
    partner = jnp.where(first_half, pltpu.roll(c, LANES - HEAD_DIM // 2, 1),
                        pltpu.roll(c, HEAD_DIM // 2, 1))
    return c * cos + partner * sin


def _inproj_common(xb, wt_ref, cos, sin, first_half, lng_ref, lnb_ref,
                   qraw_ref, qrot_ref, gate_ref, u_ref, v_ref):
    zq = _dot_nt(xb, wt_ref[C_Q:C_CMP, :])
    qraw_ref[...] = zq * Q_SCALE
    for c in range(NSA_WIDTH // LANES):
        qrot_ref[:, c * LANES:(c + 1) * LANES] = _rope_lanes(
            zq[:, c * LANES:(c + 1) * LANES], cos, sin, first_half) * Q_SCALE
    gate_ref[...] = jax.nn.sigmoid(_dot_nt(xb, wt_ref[C_GATE:C_U, :]))
    u_ref[...] = jax.nn.gelu(_dot_nt(xb, wt_ref[C_U:C_V, :]))
    zv = jax.nn.gelu(_dot_nt(xb, wt_ref[C_V:C_END, :]))
    v_ref[...] = _layer_norm(zv, lng_ref[...], lnb_ref[...])


def _inproj_prompt_kernel(x_ref, wt_ref, cos_ref, sin_ref, cost_ref, sint_ref, lng_ref, lnb_ref,
                          qraw_ref, qrot_ref, gate_ref, u_ref, v_ref,
                          cmp_ref, cmpt_ref, slct_ref, wint_ref, slcb_ref, winb_ref):
    xb = x_ref[...].astype(BF16)
    cos = cos_ref[...]
    lane = lax.broadcasted_iota(jnp.int32, cos.shape, 1)
    first_half = (lane % HEAD_DIM) < (HEAD_DIM // 2)
    _inproj_common(xb, wt_ref, cos, sin_ref[...], first_half, lng_ref, lnb_ref,
                   qraw_ref, qrot_ref, gate_ref, u_ref, v_ref)
    cmp_ref[...] = _dot_nt(xb, wt_ref[C_CMP:C_SLC, :])
    cmpt_ref[...] = _dot_nt(wt_ref[C_CMP:C_SLC, :], xb)
    cost = cost_ref[...]
    sint = sint_ref[...]
    half = HEAD_DIM // 2
    for row0, f_ref, b_ref in ((C_SLC, slct_ref, slcb_ref), (C_WIN, wint_ref, winb_ref)):
        zt = _dot_nt(wt_ref[row0:row0 + 2 * KV_WIDTH, :], xb)
        for g in range(KV_HEADS):
            kg = zt[g * HEAD_DIM:(g + 1) * HEAD_DIM, :]
            partner = jnp.concatenate([kg[half:, :], kg[:half, :]], axis=0)
            kr = kg * cost + partner * sint
            f_ref[g * HEAD_DIM:(g + 1) * HEAD_DIM, :] = kr
            b_ref[g * HEAD_DIM:(g + 1) * HEAD_DIM, :] = kr.astype(BF16)
        f_ref[KV_WIDTH:, :] = zt[KV_WIDTH:, :]
        b_ref[KV_WIDTH:, :] = zt[KV_WIDTH:, :].astype(BF16)


def _inproj_sample_kernel(x_ref, wt_ref, cos_ref, sin_ref, lng_ref, lnb_ref,
                          qraw_ref, qrot_ref, gate_ref, u_ref, v_ref, cmp_ref, slc_ref, win_ref):
    xb = x_ref[...].astype(BF16)
    cos = cos_ref[...]
    sin = sin_ref[...]
    lane = lax.broadcasted_iota(jnp.int32, cos.shape, 1)
    first_half = (lane % HEAD_DIM) < (HEAD_DIM // 2)
    _inproj_common(xb, wt_ref, cos, sin, first_half, lng_ref, lnb_ref,
                   qraw_ref, qrot_ref, gate_ref, u_ref, v_ref)
    cmp_ref[...] = _dot_nt(xb, wt_ref[C_CMP:C_SLC, :])
    for row0, f_ref in ((C_SLC, slc_ref), (C_WIN, win_ref)):
        z = _dot_nt(xb, wt_ref[row0:row0 + 2 * KV_WIDTH, :])
        f_ref[:, :KV_WIDTH] = _rope_lanes(z[:, :KV_WIDTH], cos, sin, first_half)
        f_ref[:, KV_WIDTH:] = z[:, KV_WIDTH:]


def _inproj_prompt(x, wt, cos, sin, cost, sint, ln_g, ln_b, tm, b, t):
    n, d = x.shape
    n_tab = t // tm
    row = lambda width: pl.BlockSpec((tm, width), lambda i: (i, 0))
    const = lambda a: pl.BlockSpec(a.shape, lambda i: (0,) * a.ndim)
    tab = pl.BlockSpec((tm, LANES), lambda i: (i % n_tab, 0))
    tabt = pl.BlockSpec((HEAD_DIM, tm), lambda i: (0, i % n_tab))
    fm = pl.BlockSpec((None, 2 * KV_WIDTH, tm), lambda i: (i // n_tab, 0, i % n_tab))
    row_out = ((NSA_WIDTH, F32), (NSA_WIDTH, F32), (LANES, F32), (GMLP_WIDTH, F32), (GMLP_WIDTH, F32),
               (2 * KV_WIDTH, F32))
    fm_out = (F32, F32, F32, BF16, BF16)
    return pl.pallas_call(
        _inproj_prompt_kernel,
        grid=(n // tm,),
        in_specs=[row(d), const(wt), tab, tab, tabt, tabt, const(ln_g), const(ln_b)],
        out_specs=[row(wd) for wd, _ in row_out] + [fm] * len(fm_out),
        out_shape=[jax.ShapeDtypeStruct((n, wd), dt) for wd, dt in row_out]
        + [jax.ShapeDtypeStruct((b, 2 * KV_WIDTH, t), dt) for dt in fm_out],
        compiler_params=pltpu.CompilerParams(dimension_semantics=("parallel",),
                                             vmem_limit_bytes=VMEM_LIMIT),
        name="inproj_prompt",
    )(x, wt, cos, sin, cost, sint, ln_g, ln_b)


def _inproj_sample(x, wt, cos, sin, ln_g, ln_b, tm):
    n, d = x.shape
    row = lambda width: pl.BlockSpec((tm, width), lambda i: (i, 0))
    const = lambda a: pl.BlockSpec(a.shape, lambda i: (0,) * a.ndim)
    widths = (NSA_WIDTH, NSA_WIDTH, LANES, GMLP_WIDTH, GMLP_WIDTH,
              2 * KV_WIDTH, 2 * KV_WIDTH, 2 * KV_WIDTH)
    return pl.pallas_call(
        _inproj_sample_kernel,
        grid=(n // tm,),
        in_specs=[row(d), const(wt), const(cos), const(sin), const(ln_g), const(ln_b)],
        out_specs=[row(wd) for wd in widths],
        out_shape=[jax.ShapeDtypeStruct((n, wd), F32) for wd in widths],
        compiler_params=pltpu.CompilerParams(dimension_semantics=("parallel",),
                                             vmem_limit_bytes=VMEM_LIMIT),
        name="inproj_sample",
    )(x, wt, cos, sin, ln_g, ln_b)


def _compress_rows(chunk, pe_ref, w1_ref, b1_ref, w2_ref):
    outs = []
    for k in range(2):
        a = jnp.concatenate([chunk(l, k) for l in range(CMP_BLOCK)], axis=1)
        a = (a + pe_ref[k]).astype(BF16)
        h = jax.nn.silu(_dot(a, w1_ref[k]) + b1_ref[k])
        outs.append(_dot(h.astype(BF16), w2_ref[k]))
    return jnp.concatenate(outs, axis=1)


def _compress_prompt_kernel(x_ref, pe_ref, w1_ref, b1_ref, w2_ref, o_ref):
    row_w = 2 * KV_WIDTH
    chunk = lambda l, k: x_ref[:, l * row_w + k * KV_WIDTH:l * row_w + (k + 1) * KV_WIDTH]
    o_ref[...] = _compress_rows(chunk, pe_ref, w1_ref, b1_ref, w2_ref)


def _compress_prompt(x, pe, w1, b1, w2, tm):
    n = x.shape[0]
    const = lambda a: pl.BlockSpec(a.shape, lambda i: (0,) * a.ndim)
    return pl.pallas_call(
        _compress_prompt_kernel,
        grid=(n // tm,),
        in_specs=[pl.BlockSpec((tm, x.shape[1]), lambda i: (i, 0)),
                  const(pe), const(w1), const(b1), const(w2)],
        out_specs=pl.BlockSpec((tm, 2 * KV_WIDTH), lambda i: (i, 0)),
        out_shape=jax.ShapeDtypeStruct((n, 2 * KV_WIDTH), F32),
        compiler_params=pltpu.CompilerParams(dimension_semantics=("parallel",),
                                             vmem_limit_bytes=VMEM_LIMIT),
        name="compress_prompt",
    )(x, pe, w1, b1, w2)


def _compress_sample_kernel(n_seq, n_pages, pt_ref, *refs):
    page_refs = refs[:n_seq * n_pages]
    pe_ref, w1_ref, b1_ref, w2_ref, o_ref, posk_ref, posv_ref = refs[n_seq * n_pages:]
    for j, page_ref in enumerate(page_refs):
        posk_ref[j * PAGE_SIZE:(j + 1) * PAGE_SIZE, :] = page_ref[:KV_WIDTH, :].T
        posv_ref[j * PAGE_SIZE:(j + 1) * PAGE_SIZE, :] = page_ref[KV_WIDTH:, :].T
    m = n_seq * n_pages * (PAGE_SIZE // CMP_BLOCK)
    chunk = lambda l, k: (posk_ref, posv_ref)[k][pl.ds(l, m, stride=CMP_BLOCK), :]
    o_ref[...] = _compress_rows(chunk, pe_ref, w1_ref, b1_ref, w2_ref)


def _compress_sample(pool, layer, page_table, pe, w1, b1, w2, n_seq):
    db, n_pages = page_table.shape
    rows = n_seq * n_pages * (PAGE_SIZE // CMP_BLOCK)
    const = lambda a: pl.BlockSpec(a.shape, lambda i, pt: (0,) * a.ndim)

    def page_spec(s, p):
        return pl.BlockSpec((None, None) + pool.shape[2:],
                            lambda i, pt: (layer, pt[i * n_seq + s, p], 0, 0))

    grid_spec = pltpu.PrefetchScalarGridSpec(
        num_scalar_prefetch=1,
        grid=(db // n_seq,),
        in_specs=[page_spec(s, p) for s in range(n_seq) for p in range(n_pages)]
        + [const(pe), const(w1), const(b1), const(w2)],
        out_specs=pl.BlockSpec((rows, 2 * KV_WIDTH), lambda i, pt: (i, 0)),
        scratch_shapes=[pltpu.VMEM((n_seq * n_pages * PAGE_SIZE, KV_WIDTH), F32)] * 2,
    )
    return pl.pallas_call(
        functools.partial(_compress_sample_kernel, n_seq, n_pages),
        grid_spec=grid_spec,
        out_shape=jax.ShapeDtypeStruct((db * n_pages * (PAGE_SIZE // CMP_BLOCK), 2 * KV_WIDTH), F32),
        compiler_params=pltpu.CompilerParams(dimension_semantics=("parallel",),
                                             vmem_limit_bytes=VMEM_LIMIT),
        name="compress_sample",
    )(page_table, *([pool] * (n_seq * n_pages)), pe, w1, b1, w2)


def _stack_heads(q, tq):
    lane_lo = lax.broadcasted_iota(jnp.int32, (tq, LANES), 1) < HEAD_DIM
    pieces = []
    for hh in range(NSA_HEADS):
        g = hh // GROUP
        c = hh // 2
        chunk = q[:, c * LANES:(c + 1) * LANES]
        if hh % 2 != g:
            chunk = pltpu.roll(chunk, HEAD_DIM, 1)
        pieces.append(jnp.where(lane_lo if g == 0 else jnp.logical_not(lane_lo), chunk, 0.0))
    return jnp.concatenate(pieces, axis=0).astype(BF16)


def _cmp_branch(qraw, ckv, qpos, tq):
    nb = MAX_SLC_BLOCKS
    kc = jnp.concatenate([ckv[:, 0:KV_WIDTH], ckv[:, 2 * KV_WIDTH:3 * KV_WIDTH]], axis=0).astype(BF16)
    vc = jnp.concatenate([ckv[:, KV_WIDTH:2 * KV_WIDTH], ckv[:, 3 * KV_WIDTH:]], axis=0).astype(BF16)
    s = _dot_nt(qraw, kc).reshape(NSA_HEADS, tq, 2 * nb)
    lane = lax.broadcasted_iota(jnp.int32, (tq, 2 * nb), 1)
    blk = jnp.where(lane < nb, 2 * lane, 2 * (lane - nb) + 1)
    mask = ((blk + 1) * CMP_BLOCK - 1) <= qpos
    p = _softmax_masked(s, mask[None])
    o_cmp = _dot(p.reshape(NSA_HEADS * tq, 2 * nb).astype(BF16), vc)
    scores = []
    sblk = lax.broadcasted_iota(jnp.int32, (tq, nb), 1)
    cur = qpos // SLC_BLOCK
    valid = sblk <= cur
    forced = ((sblk == 0) | (sblk == cur) | (sblk == cur - 1)).astype(F32)
    for g in range(KV_HEADS):
        imp = p[g * GROUP]
        for r in range(1, GROUP):
            imp = imp + p[g * GROUP + r]
        imp = imp[:, :nb] + imp[:, nb:]
        scores.append(jnp.where(valid, imp + FORCE_BONUS * forced, -1.0))
    return o_cmp, scores, valid


def _rank_blocks_wide(score, st_ref, cnt_ref, n_chunks):
    st_ref[...] = score.T
    cnt_ref[...] = jnp.zeros_like(cnt_ref)
    sub = lax.broadcasted_iota(jnp.int32, (SUBLANES, LANES), 0)
    n_vregs = MAX_SLC_BLOCKS // SUBLANES
    for c in range(n_vregs):
        @pl.when(c < n_chunks)
        def _():
            tiles = [st_ref[v * SUBLANES:(v + 1) * SUBLANES, :] for v in range(n_vregs)]
            cnts = [cnt_ref[v * SUBLANES:(v + 1) * SUBLANES, :] for v in range(n_vregs)]
            for ii in range(SUBLANES):
                row = tiles[c][ii:ii + 1, :]
                for v in range(n_vregs):
                    if v < c:
                        inc = jnp.where(row > tiles[v], 1.0, 0.0)
                    elif v > c:
                        inc = jnp.where(row >= tiles[v], 1.0, 0.0)
                    else:
                        inc = jnp.where(sub > ii, jnp.where(row >= tiles[v], 1.0, 0.0),
                                        jnp.where(row > tiles[v], 1.0, 0.0))
                    cnts[v] = cnts[v] + inc
            for v in range(n_vregs):
                cnt_ref[v * SUBLANES:(v + 1) * SUBLANES, :] = cnts[v]
    return cnt_ref[...].T


def _rank_blocks_narrow(score, n_blocks):
    lane = lax.broadcasted_iota(jnp.int32, score.shape, 1)
    cnt = jnp.zeros(score.shape, F32)
    for i in range(n_blocks):
        col = score[:, i:i + 1]
        cnt = cnt + jnp.where(lane > i, jnp.where(col >= score, 1.0, 0.0),
                              jnp.where(col > score, 1.0, 0.0))
    return cnt


def _slc_branch(q_aug, kv_ref, et_ref, n_full, qpos, tq, tk):
    rows = NSA_HEADS * tq
    ones_rows = jnp.where(lax.broadcasted_iota(jnp.int32, (KV_WIDTH, tk), 0) == 0, 1.0, 0.0).astype(BF16)

    def tile(kt, carry, diagonal):
        m, acc = carry
        off = pl.multiple_of(kt * tk, tk)
        ka = jnp.concatenate([et_ref[:, pl.ds(off, tk)], kv_ref[:KV_WIDTH, pl.ds(off, tk)]], axis=0)
        s = _dot(q_aug, ka).reshape(NSA_HEADS, tq, tk)
        if diagonal:
            kpos = off + lax.broadcasted_iota(jnp.int32, (tq, tk), 1)
            s = jnp.where((kpos <= qpos)[None], s, NEG_INF)
        m_new = jnp.maximum(m, jnp.max(s, axis=-1, keepdims=True))
        alpha = jnp.exp2(m - m_new)
        p = jnp.exp2(s - m_new)
        va = jnp.concatenate([kv_ref[KV_WIDTH:, pl.ds(off, tk)], ones_rows], axis=0)
        pv = _dot_nt(p.reshape(rows, tk).astype(BF16), va)
        acc = alpha.reshape(rows, 1) * acc + pv
        return m_new, acc

    init = (jnp.full((NSA_HEADS, tq, 1), NEG_INF, F32), jnp.zeros((rows, 2 * KV_WIDTH), F32))
    carry = lax.fori_loop(0, n_full, lambda kt, c: tile(kt, c, False), init)
    _, acc = tile(n_full, carry, True)
    return acc[:, :KV_WIDTH] * (1.0 / jnp.maximum(acc[:, KV_WIDTH:KV_WIDTH + 1], 1e-30))


def _win_branch(qrot, kt, vt, kpos, qpos, tq):
    nk = kt.shape[1]
    s = _dot(qrot, kt).reshape(NSA_HEADS, tq, nk)
    diff = qpos - kpos
    mask = (diff >= 0) & (diff < WINDOW)
    p = _softmax_masked(s, mask[None])
    return _dot_nt(p.reshape(NSA_HEADS * tq, nk).astype(BF16), vt)


def _combine(o_cmp, o_slc, o_win, gates, tq):
    lane_lo = lax.broadcasted_iota(jnp.int32, (tq, LANES), 1) < HEAD_DIM
    heads = []
    for hh in range(NSA_HEADS):
        tot = None
        for br, o in enumerate((o_cmp, o_slc, o_win)):
            col = br * NSA_HEADS + hh
            term = gates[:, col:col + 1] * o[hh * tq:(hh + 1) * tq, :]
            tot = term if tot is None else tot + term
        heads.append(tot)
    chunks = []
    for c in range(NSA_HEADS // 2):
        a, b = heads[2 * c], heads[2 * c + 1]
        if c // 2 == 0:
            chunks.append(jnp.where(lane_lo, a, pltpu.roll(b, HEAD_DIM, 1)))
        else:
            chunks.append(jnp.where(lane_lo, pltpu.roll(a, HEAD_DIM, 1), b))
    return jnp.concatenate(chunks, axis=1)


def _selection_bias(rank, valid):
    return jnp.where((rank < float(N_SELECT)) & valid, 0.0, NEG_INF).astype(BF16)


def _pad_ckv(ckv):
    n = ckv.shape[0]
    if n == MAX_SLC_BLOCKS:
        return ckv
    return jnp.concatenate([ckv, jnp.zeros((MAX_SLC_BLOCKS - n, ckv.shape[1]), F32)], axis=0)


def _nsa_prompt_kernel(qraw_ref, qrot_ref, gate_ref, ckv_ref, slc_ref, win_ref, et_ref,
                       o_ref, st_ref, cnt_ref):
    tq = QUERY_BLOCK
    i = pl.program_id(1)
    qpos0 = i * tq
    qpos = qpos0 + lax.broadcasted_iota(jnp.int32, (tq, 1), 0)
    qraw = _stack_heads(qraw_ref[...], tq)
    qrot = _stack_heads(qrot_ref[...], tq)

    o_cmp, scores, valid = _cmp_branch(qraw, _pad_ckv(ckv_ref[...]), qpos, tq)
    n_chunks = (qpos0 + tq - 1) // (SLC_BLOCK * SUBLANES) + 1
    bias = []
    for g in range(KV_HEADS):
        rank = _rank_blocks_wide(scores[g], st_ref, cnt_ref, n_chunks)
        bias.append(_selection_bias(rank, valid))
    bias_rows = jnp.concatenate([bias[0]] * GROUP + [bias[1]] * GROUP, axis=0)
    q_aug = jnp.concatenate([bias_rows, qrot], axis=1)
    o_slc = _slc_branch(q_aug, slc_ref, et_ref, qpos0 // SLC_KEY_TILE, qpos, tq, SLC_KEY_TILE)

    nk = WINDOW + tq
    start = pl.multiple_of(jnp.maximum(qpos0 - WINDOW, 0), tq)
    kpos = start + lax.broadcasted_iota(jnp.int32, (1, nk), 1)
    o_win = _win_branch(qrot, win_ref[:KV_WIDTH, pl.ds(start, nk)], win_ref[KV_WIDTH:, pl.ds(start, nk)],
                        kpos, qpos, tq)

    o_ref[...] = _combine(o_cmp, o_slc, o_win, gate_ref[...], tq)


def _nsa_prompt(qraw, qrot, gates, ckv, slcb, winb, et):
    b, t, _ = qraw.shape
    tq = QUERY_BLOCK
    blk = lambda width: pl.BlockSpec((None, tq, width), lambda bi, i: (bi, i, 0))
    per_batch = lambda a: pl.BlockSpec((None,) + a.shape[1:], lambda bi, i: (bi, 0, 0))
    return pl.pallas_call(
        _nsa_prompt_kernel,
        grid=(b, t // tq),
        in_specs=[blk(NSA_WIDTH), blk(NSA_WIDTH), blk(LANES), per_batch(ckv), per_batch(slcb),
                  per_batch(winb), pl.BlockSpec(et.shape, lambda bi, i: (0, 0))],
        out_specs=blk(NSA_WIDTH),
        out_shape=jax.ShapeDtypeStruct((b, t, NSA_WIDTH), F32),
        scratch_shapes=[pltpu.VMEM((MAX_SLC_BLOCKS, tq), F32), pltpu.VMEM((MAX_SLC_BLOCKS, tq), F32)],
        compiler_params=pltpu.CompilerParams(dimension_semantics=("parallel", "arbitrary"),
                                             vmem_limit_bytes=VMEM_LIMIT),
        name="nsa_prompt",
    )(qraw, qrot, gates, ckv, slcb, winb, et)


def _nsa_sample_kernel(n_seq, n_pages, past, s_new, pt_ref, *refs):
    n_in = n_seq * n_pages
    page_refs = refs[:n_in]
    (qraw_ref, qrot_ref, gate_ref, ckv_ref, slcn_ref, wins_ref, winn_ref, et_ref,
     o_ref, wout_ref) = refs[n_in:n_in + 10]
    bufs = refs[n_in + 10:]
    tq = SAMPLE_ROWS
    tk = SLC_KEY_TILE
    qpos = past + lax.broadcasted_iota(jnp.int32, (tq, 1), 0)
    n_buf = wins_ref.shape[2]
    n_blocks = (past + tq - 1) // SLC_BLOCK + 1
    n_keys = bufs[0].shape[1]
    pad_rows = jnp.zeros((LANES - tq, 2 * KV_WIDTH), F32)
    wpos = past - n_buf + lax.broadcasted_iota(jnp.int32, (1, n_buf + LANES), 1)
    lane = lax.broadcasted_iota(jnp.int32, (2 * KV_WIDTH, LANES), 1)
    for s in range(n_seq):
        buf = bufs[s]
        for p in range(n_pages):
            buf[:, p * PAGE_SIZE:(p + 1) * PAGE_SIZE] = page_refs[s * n_pages + p][...].astype(BF16)
        slc_new = jnp.concatenate([slcn_ref[s], pad_rows], axis=0).T
        buf[:, past:past + LANES] = slc_new.astype(BF16)
        buf[:, past + LANES:] = jnp.zeros((2 * KV_WIDTH, n_keys - past - LANES), BF16)

        qraw = _stack_heads(qraw_ref[s], tq)
        qrot = _stack_heads(qrot_ref[s], tq)
        o_cmp, scores, valid = _cmp_branch(qraw, _pad_ckv(ckv_ref[s]), qpos, tq)
        bias = [_selection_bias(_rank_blocks_narrow(scores[g], n_blocks), valid)
                for g in range(KV_HEADS)]
        bias_rows = jnp.concatenate([bias[0]] * GROUP + [bias[1]] * GROUP, axis=0)
        q_aug = jnp.concatenate([bias_rows, qrot], axis=1)
        o_slc = _slc_branch(q_aug, buf, et_ref, past // tk, qpos, tq, tk)

        state = wins_ref[s]
        win_new = jnp.concatenate([winn_ref[s], pad_rows], axis=0).T
        kt = jnp.concatenate([state[:KV_WIDTH], win_new[:KV_WIDTH]], axis=1).astype(BF16)
        vt = jnp.concatenate([state[KV_WIDTH:], win_new[KV_WIDTH:]], axis=1).astype(BF16)
        o_win = _win_branch(qrot, kt, vt, wpos, qpos, tq)
        o_ref[s] = _combine(o_cmp, o_slc, o_win, gate_ref[s], tq)

        shifted = pltpu.roll(state, n_buf - s_new, 1)
        tail = jnp.where(lane >= LANES - s_new, pltpu.roll(win_new, LANES - s_new, 1),
                         shifted[:, n_buf - LANES:])
        wout_ref[s, :, :n_buf - LANES] = shifted[:, :n_buf - LANES]
        wout_ref[s, :, n_buf - LANES:] = tail


def _nsa_sample(pool, win_state, layer, page_table, qraw, qrot, gates, ckv, slc_new, win_new, et,
                n_seq, s_new):
    db, n_pages = page_table.shape
    past = n_pages * PAGE_SIZE
    tq = SAMPLE_ROWS
    n_buf = win_state.shape[3]
    n_keys = (past // SLC_KEY_TILE + 1) * SLC_KEY_TILE
    seq = lambda a: pl.BlockSpec((n_seq,) + a.shape[1:], lambda i, pt: (i, 0, 0))

    def page_spec(s, p):
        return pl.BlockSpec((None, None) + pool.shape[2:],
                            lambda i, pt: (layer, pt[i * n_seq + s, p], 0, 0))

    state_spec = pl.BlockSpec((None, n_seq) + win_state.shape[2:], lambda i, pt: (layer, i, 0, 0))
    grid_spec = pltpu.PrefetchScalarGridSpec(
        num_scalar_prefetch=1,
        grid=(db // n_seq,),
        in_specs=[page_spec(s, p) for s in range(n_seq) for p in range(n_pages)]
        + [seq(qraw), seq(qrot), seq(gates), seq(ckv), seq(slc_new), state_spec, seq(win_new),
           pl.BlockSpec(et.shape, lambda i, pt: (0, 0))],
        out_specs=[pl.BlockSpec((n_seq, tq, NSA_WIDTH), lambda i, pt: (i, 0, 0)),
                   pl.BlockSpec((n_seq,) + win_state.shape[2:], lambda i, pt: (i, 0, 0))],
        scratch_shapes=[pltpu.VMEM((2 * KV_WIDTH, n_keys), BF16)] * n_seq,
    )
    return pl.pallas_call(
        functools.partial(_nsa_sample_kernel, n_seq, n_pages, past, s_new),
        grid_spec=grid_spec,
        out_shape=[jax.ShapeDtypeStruct((db, tq, NSA_WIDTH), F32),
                   jax.ShapeDtypeStruct((db,) + win_state.shape[2:], F32)],
        compiler_params=pltpu.CompilerParams(dimension_semantics=("parallel",),
                                             vmem_limit_bytes=VMEM_LIMIT),
        name="nsa_sample",
    )(page_table, *([pool] * (n_seq * n_pages)), qraw, qrot, gates, ckv, slc_new, win_state, win_new, et)


def _post_kernel(chunk, alpha, n_ffn_chunks, x_ref, o_ref, u_ref, v_ref, gw_ref, gb_ref,
                 wout_ref, ln1g_ref, ln1b_ref, wg_ref, wu_ref, wd_ref, ln2g_ref, ln2b_ref, y_ref):
    tm = x_ref.shape[0]
    c = GMLP_CHUNK
    row = lax.broadcasted_iota(jnp.int32, (c, c), 0)
    col = lax.broadcasted_iota(jnp.int32, (c, c), 1)
    causal = (col <= row) & ((row // chunk) == (col // chunk))
    w_cat = jnp.concatenate([jnp.where(causal, gw_ref[h], 0.0) for h in range(GMLP_GROUPS)],
                            axis=1).astype(BF16)
    lane_group = lax.broadcasted_iota(jnp.int32, (c, GMLP_WIDTH), 1) // (GMLP_WIDTH // GMLP_GROUPS)
    mixed = []
    for sub in range(tm // c):
        v = v_ref[sub * c:(sub + 1) * c, :]
        v_diag = jnp.concatenate([jnp.where(lane_group == h, v, 0.0) for h in range(GMLP_GROUPS)],
                                 axis=0).astype(BF16)
        s = _dot(w_cat, v_diag) + gb_ref[...]
        mixed.append(u_ref[sub * c:(sub + 1) * c, :] * s)
    o_gmlp = jnp.concatenate(mixed, axis=0)
    mix = jnp.concatenate([o_ref[...], o_gmlp], axis=1).astype(BF16)
    h = _dot(mix, wout_ref[...])
    x1 = _layer_norm(alpha * x_ref[...] + h, ln1g_ref[...], ln1b_ref[...])
    x1b = x1.astype(BF16)
    f = None
    fc = wg_ref.shape[1] // n_ffn_chunks
    for j in range(n_ffn_chunks):
        gate = _dot(x1b, wg_ref[:, j * fc:(j + 1) * fc])
        up = _dot(x1b, wu_ref[:, j * fc:(j + 1) * fc])
        a = (jax.nn.silu(gate) * up).astype(BF16)
        part = _dot(a, wd_ref[j * fc:(j + 1) * fc, :])
        f = part if f is None else f + part
    y_ref[...] = _layer_norm(alpha * x1 + f, ln2g_ref[...], ln2b_ref[...])


def _post(x, o_nsa, u, v, gw, gb, wout, ln1g, ln1b, wg, wu, wd, ln2g, ln2b, chunk, alpha, tm):
    n, d = x.shape
    ffn = wg.shape[1]
    n_ffn_chunks = next(k for k in (4, 2, 1) if ffn % (k * LANES) == 0)
    row = lambda width: pl.BlockSpec((tm, width), lambda i: (i, 0))
    const = lambda a: pl.BlockSpec(a.shape, lambda i: (0,) * a.ndim, pipeline_mode=pl.Buffered(1))
    consts = (gw, gb, wout, ln1g, ln1b, wg, wu, wd, ln2g, ln2b)
    return pl.pallas_call(
        functools.partial(_post_kernel, chunk, alpha, n_ffn_chunks),
        grid=(n // tm,),
        in_specs=[row(d), row(NSA_WIDTH), row(GMLP_WIDTH), row(GMLP_WIDTH)] + [const(a) for a in consts],
        out_specs=row(d),
        out_shape=jax.ShapeDtypeStruct((n, d), F32),
        compiler_params=pltpu.CompilerParams(dimension_semantics=("parallel",),
                                             vmem_limit_bytes=VMEM_LIMIT),
        name="post",
    )(x, o_nsa, u, v, *consts)


def _pack_w_in_t(w_in):
    wt = w_in.T
    o_gate = NSA_WIDTH + 3 * 2 * KV_WIDTH
    o_gmlp = o_gate + 3 * NSA_HEADS
    gate = jnp.pad(wt[o_gate:o_gmlp], ((0, LANES - 3 * NSA_HEADS), (0, 0)))
    return jnp.concatenate([wt[:o_gate], gate, wt[o_gmlp:]], axis=0).astype(BF16)


def _pack_compress(pe, w1, b1, w2):
    eye = jnp.eye(KV_HEADS, dtype=F32)
    w1p = jnp.einsum('kldh,gG->klgdGh', w1, eye).reshape(2, CMP_BLOCK * KV_WIDTH, KV_HEADS * CMP_HIDDEN)
    pep = jnp.broadcast_to(pe[:, :, None, :], (2, CMP_BLOCK, KV_HEADS, HEAD_DIM)).reshape(2, 1, -1)
    b1p = jnp.tile(b1, (1, KV_HEADS))[:, None, :]
    w2p = jnp.einsum('khd,gG->kghGd', w2, eye).reshape(2, KV_HEADS * CMP_HIDDEN, KV_WIDTH)
    return pep, w1p.astype(BF16), b1p, w2p.astype(BF16)


def _pack_gmlp(ws, bs, chunk):
    reps = GMLP_CHUNK // chunk
    gw = jnp.tile(ws[:, :chunk, :chunk], (1, reps, reps))
    gb = jnp.tile(jnp.repeat(bs[:, :chunk].T, GMLP_WIDTH // GMLP_GROUPS, axis=1), (reps, 1))
    return gw, gb


def _rope_tables(pos):
    half = HEAD_DIM // 2
    inv_freq = ROPE_THETA ** (-jnp.arange(half, dtype=F32) / half)
    ang = pos.astype(F32)[:, None] * inv_freq[None, :]
    cos, sin = jnp.cos(ang), jnp.sin(ang)
    return jnp.concatenate([cos, cos], axis=1), jnp.concatenate([-sin, sin], axis=1)


def _block_indicator(n_keys):
    key_block = np.arange(n_keys)[None, :] // SLC_BLOCK
    return jnp.asarray(key_block == np.arange(MAX_SLC_BLOCKS)[:, None], dtype=BF16)


def _feature_major(a):
    lead = a.shape[:-4]
    n = len(lead)
    perm = tuple(range(n)) + (n + 1, n + 2, n + 3, n)
    return jnp.transpose(a, perm).reshape(lead + (2 * KV_WIDTH, a.shape[-4]))


def _position_major(a):
    lead = a.shape[:-2]
    n = len(lead)
    a = a.reshape(lead + (2, KV_HEADS, HEAD_DIM, a.shape[-1]))
    return jnp.transpose(a, tuple(range(n)) + (n + 3, n, n + 1, n + 2))


def kernel(x_prompt, x_sample, cache_cmp_kv, cache_slc_kv, state_win_kv, page_table,
           w_in, cmp_pe, cmp_w1, cmp_b1, cmp_w2, gmlp_ln_g, gmlp_ln_b, gmlp_ws, gmlp_bs,
           w_out, ln1_g, ln1_b, w_gate, w_up, w_down, ln2_g, ln2_b):
    depth = w_in.shape[0]
    b, t, d = x_prompt.shape
    db, s_new, _ = x_sample.shape
    n_pages = page_table.shape[1]
    past = n_pages * PAGE_SIZE
    n_buf = state_win_kv.shape[2]
    kv_tail = (2, KV_HEADS, HEAD_DIM)
    rows_s = SAMPLE_ROWS
    assert t % SLC_KEY_TILE == 0 and t >= WINDOW + QUERY_BLOCK and t // SLC_BLOCK <= MAX_SLC_BLOCKS
    assert s_new <= rows_s and (past + rows_s - 1) // SLC_BLOCK < MAX_SLC_BLOCKS
    assert past % CMP_BLOCK == 0 and past % CMP_BLOCK + s_new < CMP_BLOCK
    assert n_buf % LANES == 0 and n_buf >= LANES
    alpha = (2 * depth) ** 0.25

    tm_p = 512
    tm_s = min(512, db * rows_s)
    seq_per_step = 2 if db % 2 == 0 else 1
    cos_p, sin_p = _rope_tables(jnp.arange(t, dtype=jnp.int32))
    cos_s, sin_s = _rope_tables(past + jnp.arange(rows_s, dtype=jnp.int32))
    lane_reps = LANES // HEAD_DIM
    cos_pl, sin_pl = jnp.tile(cos_p, (1, lane_reps)), jnp.tile(sin_p, (1, lane_reps))
    cos_pt, sin_pt = cos_p.T, sin_p.T
    cos_s = jnp.tile(cos_s, (tm_s // rows_s, lane_reps))
    sin_s = jnp.tile(sin_s, (tm_s // rows_s, lane_reps))
    et_p = _block_indicator(t)
    et_s = _block_indicator((past // SLC_KEY_TILE + 1) * SLC_KEY_TILE)
    pool_cmp = _feature_major(cache_cmp_kv)
    pool_slc = _feature_major(cache_slc_kv)
    win_state = _feature_major(state_win_kv)

    yp = x_prompt.reshape(b * t, d)
    ys = jnp.pad(x_sample, ((0, 0), (0, rows_s - s_new), (0, 0))).reshape(db * rows_s, d)
    outs = {k: [] for k in ("p_cmp", "p_slc", "p_win", "s_cmp", "s_slc", "s_win", "s_v")}
    n_keep = min(WINDOW, t)
    cmp_tile = min(128, b * t // CMP_BLOCK)

    for l in range(depth):
        wt = _pack_w_in_t(w_in[l])
        pe_p, w1_p, b1_p, w2_p = _pack_compress(cmp_pe[l], cmp_w1[l], cmp_b1[l], cmp_w2[l])
        ln_g = gmlp_ln_g[l][None, :]
        ln_b = gmlp_ln_b[l][None, :]
        tail = (w_out[l].astype(BF16), ln1_g[l][None, :], ln1_b[l][None, :], w_gate[l].astype(BF16),
                w_up[l].astype(BF16), w_down[l].astype(BF16), ln2_g[l][None, :], ln2_b[l][None, :])

        qraw, qrot, gates, u, v, cmp_kv, cmpt, slct, wint, slcb, winb = _inproj_prompt(
            yp, wt, cos_pl, sin_pl, cos_pt, sin_pt, ln_g, ln_b, tm_p, b, t)
        ckv = _compress_prompt(cmp_kv.reshape(b * t // CMP_BLOCK, CMP_BLOCK * 2 * KV_WIDTH),
                               pe_p, w1_p, b1_p, w2_p, cmp_tile)
        r3 = lambda a: a.reshape(b, t, a.shape[-1])
        o_nsa = _nsa_prompt(r3(qraw), r3(qrot), r3(gates), ckv.reshape(b, t // SLC_BLOCK, 4 * KV_WIDTH),
                            slcb, winb, et_p)
        gw, gb = _pack_gmlp(gmlp_ws[l], gmlp_bs[l], GMLP_CHUNK)
        yp = _post(yp, o_nsa.reshape(b * t, NSA_WIDTH), u, v, gw, gb, *tail, GMLP_CHUNK, alpha, tm_p)
        outs["p_cmp"].append(_position_major(cmpt))
        outs["p_slc"].append(_position_major(slct))
        outs["p_win"].append(_position_major(wint[:, :, t - n_keep:]))

        qraw, qrot, gates, u, v, cmp_kv, slc_kv, win_kv = _inproj_sample(
            ys, wt, cos_s, sin_s, ln_g, ln_b, tm_s)
        ckv = _compress_sample(pool_cmp, l, page_table, pe_p, w1_p, b1_p, w2_p, seq_per_step)
        s3 = lambda a: a.reshape(db, rows_s, a.shape[-1])
        o_nsa, win_next = _nsa_sample(pool_slc, win_state, l, page_table, s3(qraw), s3(qrot), s3(gates),
                                      ckv.reshape(db, past // SLC_BLOCK, 4 * KV_WIDTH),
                                      s3(slc_kv), s3(win_kv), et_s, seq_per_step, s_new)
        gw, gb = _pack_gmlp(gmlp_ws[l], gmlp_bs[l], rows_s)
        ys = _post(ys, o_nsa.reshape(db * rows_s, NSA_WIDTH), u, v, gw, gb, *tail, rows_s, alpha, tm_s)
        new = lambda a: a.reshape((db, rows_s) + a.shape[1:])[:, :s_new]
        outs["s_cmp"].append(new(cmp_kv).reshape((db, s_new) + kv_tail))
        outs["s_slc"].append(new(slc_kv).reshape((db, s_new) + kv_tail))
        outs["s_win"].append(_position_major(win_next))
        outs["s_v"].append(new(v))

    y_sample = ys.reshape(db, rows_s, d)[:, :s_new]
    return (yp.reshape(b, t, d), y_sample, jnp.stack(outs["p_cmp"]), jnp.stack(outs["p_slc"]),
            jnp.stack(outs["p_win"]), jnp.stack(outs["s_cmp"]), jnp.stack(outs["s_slc"]),
            jnp.stack(outs["s_win"]), jnp.stack(outs["s_v"]))
```

```python
import functools
import math

import numpy as np
import jax
import jax.numpy as jnp
from jax import lax
from jax.experimental import pallas as pl
from jax.experimental.pallas import tpu as pltpu

F32 = jnp.float32
BF16 = jnp.bfloat16

LANES = 128
SUBLANES = 8
HEAD_DIM = 64
NSA_HEADS = 8
KV_HEADS = 2
GROUP = NSA_HEADS // KV_HEADS
NSA_WIDTH = NSA_HEADS * HEAD_DIM
KV_WIDTH = KV_HEADS * HEAD_DIM
GMLP_WIDTH = 512
GMLP_GROUPS = 8
GMLP_CHUNK = 128
CMP_BLOCK = 32
CMP_HIDDEN = 256
SLC_BLOCK = 64
N_SELECT = 16
WINDOW = 512
QUERY_BLOCK = 128
PAGE_SIZE = 128
FORCE_BONUS = 1.0e4
ROPE_THETA = 10000.0
LN_EPS = 1e-5
NEG_INF = -1e30
MAX_SLC_BLOCKS = LANES
SAMPLE_ROWS = SUBLANES
SLC_KEY_TILE = 512
VMEM_LIMIT = 56 * 1024 * 1024
Q_SCALE = HEAD_DIM ** -0.5 * math.log2(math.e)

C_Q = 0
C_CMP = C_Q + NSA_WIDTH
C_SLC = C_CMP + 2 * KV_WIDTH
C_WIN = C_SLC + 2 * KV_WIDTH
C_GATE = C_WIN + 2 * KV_WIDTH
C_U = C_GATE + LANES
C_V = C_U + GMLP_WIDTH
C_END = C_V + GMLP_WIDTH


def _dot(a, b):
    return jnp.dot(a, b, preferred_element_type=F32)


def _dot_nt(a, b):
    return lax.dot_general(a, b, (((1,), (1,)), ((), ())), preferred_element_type=F32)


def _layer_norm(x, g, b):
    mu = jnp.mean(x, axis=-1, keepdims=True)
    xc = x - mu
    var = jnp.mean(xc * xc, axis=-1, keepdims=True)
    return xc * lax.rsqrt(var + LN_EPS) * g + b


def _softmax_masked(s, mask):
    s = jnp.where(mask, s, NEG_INF)
    m = jnp.max(s, axis=-1, keepdims=True)
    e = jnp.where(mask, jnp.exp2(s - m), 0.0)
    return e * (1.0 / jnp.maximum(jnp.sum(e, axis=-1, keepdims=True), 1e-30))


def _rope_lanes(c, cos, sin, first_half):
    partner = jnp.where(first_half, pltpu.roll(c, LANES - HEAD_DIM // 2, 1),
                        pltpu.roll(c, HEAD_DIM // 2, 1))
    return c * cos + partner * sin


def _inproj_common(xb, wt_ref, cos, sin, first_half, lng_ref, lnb_ref,
                   qraw_ref, qrot_ref, gate_ref, u_ref, v_ref):
    zq = _dot_nt(xb, wt_ref[C_Q:C_CMP, :])
    qraw_ref[...] = zq * Q_SCALE
    for c in range(NSA_WIDTH // LANES):
        qrot_ref[:, c * LANES:(c + 1) * LANES] = _rope_lanes(
            zq[:, c * LANES:(c + 1) * LANES], cos, sin, first_half) * Q_SCALE
    gate_ref[...] = jax.nn.sigmoid(_dot_nt(xb, wt_ref[C_GATE:C_U, :]))
    u_ref[...] = jax.nn.gelu(_dot_nt(xb, wt_ref[C_U:C_V, :]))
    zv = jax.nn.gelu(_dot_nt(xb, wt_ref[C_V:C_END, :]))
    v_ref[...] = _layer_norm(zv, lng_ref[...], lnb_ref[...])


def _inproj_prompt_kernel(x_ref, wt_ref, cos_ref, sin_ref, cost_ref, sint_ref, lng_ref, lnb_ref,
                          qraw_ref, qrot_ref, gate_ref, u_ref, v_ref,
                          cmp_ref, cmpt_ref, slct_ref, wint_ref, slcb_ref, winb_ref):
    xb = x_ref[...].astype(BF16)
    cos = cos_ref[...]
    lane = lax.broadcasted_iota(jnp.int32, cos.shape, 1)
    first_half = (lane % HEAD_DIM) < (HEAD_DIM // 2)
    _inproj_common(xb, wt_ref, cos, sin_ref[...], first_half, lng_ref, lnb_ref,
                   qraw_ref, qrot_ref, gate_ref, u_ref, v_ref)
    cmp_ref[...] = _dot_nt(xb, wt_ref[C_CMP:C_SLC, :])
    cmpt_ref[...] = _dot_nt(wt_ref[C_CMP:C_SLC, :], xb)
    cost = cost_ref[...]
    sint = sint_ref[...]
    half = HEAD_DIM // 2
    for row0, f_ref, b_ref in ((C_SLC, slct_ref, slcb_ref), (C_WIN, wint_ref, winb_ref)):
        zt = _dot_nt(wt_ref[row0:row0 + 2 * KV_WIDTH, :], xb)
        for g in range(KV_HEADS):
            kg = zt[g * HEAD_DIM:(g + 1) * HEAD_DIM, :]
            partner = jnp.concatenate([kg[half:, :], kg[:half, :]], axis=0)
            kr = kg * cost + partner * sint
            f_ref[g * HEAD_DIM:(g + 1) * HEAD_DIM, :] = kr
            b_ref[g * HEAD_DIM:(g + 1) * HEAD_DIM, :] = kr.astype(BF16)
        f_ref[KV_WIDTH:, :] = zt[KV_WIDTH:, :]
        b_ref[KV_WIDTH:, :] = zt[KV_WIDTH:, :].astype(BF16)


def _inproj_sample_kernel(x_ref, wt_ref, cos_ref, sin_ref, lng_ref, lnb_ref,
                          qraw_ref, qrot_ref, gate_ref, u_ref, v_ref, cmp_ref, slc_ref, win_ref):
    xb = x_ref[...].astype(BF16)
    cos = cos_ref[...]
    sin = sin_ref[...]
    lane = lax.broadcasted_iota(jnp.int32, cos.shape, 1)
    first_half = (lane % HEAD_DIM) < (HEAD_DIM // 2)
    _inproj_common(xb, wt_ref, cos, sin, first_half, lng_ref, lnb_ref,
                   qraw_ref, qrot_ref, gate_ref, u_ref, v_ref)
    cmp_ref[...] = _dot_nt(xb, wt_ref[C_CMP:C_SLC, :])
    for row0, f_ref in ((C_SLC, slc_ref), (C_WIN, win_ref)):
        z = _dot_nt(xb, wt_ref[row0:row0 + 2 * KV_WIDTH, :])
        f_ref[:, :KV_WIDTH] = _rope_lanes(z[:, :KV_WIDTH], cos, sin, first_half)
        f_ref[:, KV_WIDTH:] = z[:, KV_WIDTH:]


def _inproj_prompt(x, wt, cos, sin, cost, sint, ln_g, ln_b, tm, b, t):
    n, d = x.shape
    n_tab = t // tm
    row = lambda width: pl.BlockSpec((tm, width), lambda i: (i, 0))
    const = lambda a: pl.BlockSpec(a.shape, lambda i: (0,) * a.ndim)
    tab = pl.BlockSpec((tm, LANES), lambda i: (i % n_tab, 0))
    tabt = pl.BlockSpec((HEAD_DIM, tm), lambda i: (0, i % n_tab))
    fm = pl.BlockSpec((None, 2 * KV_WIDTH, tm), lambda i: (i // n_tab, 0, i % n_tab))
    row_out = ((NSA_WIDTH, F32), (NSA_WIDTH, F32), (LANES, F32), (GMLP_WIDTH, F32), (GMLP_WIDTH, F32),
               (2 * KV_WIDTH, F32))
    fm_out = (F32, F32, F32, BF16, BF16)
    return pl.pallas_call(
        _inproj_prompt_kernel,
        grid=(n // tm,),
        in_specs=[row(d), const(wt), tab, tab, tabt, tabt, const(ln_g), const(ln_b)],
        out_specs=[row(wd) for wd, _ in row_out] + [fm] * len(fm_out),
        out_shape=[jax.ShapeDtypeStruct((n, wd), dt) for wd, dt in row_out]
        + [jax.ShapeDtypeStruct((b, 2 * KV_WIDTH, t), dt) for dt in fm_out],
        compiler_params=pltpu.CompilerParams(dimension_semantics=("parallel",),
                                             vmem_limit_bytes=VMEM_LIMIT),
        name="inproj_prompt",
    )(x, wt, cos, sin, cost, sint, ln_g, ln_b)


def _inproj_sample(x, wt, cos, sin, ln_g, ln_b, tm):
    n, d = x.shape
    row = lambda width: pl.BlockSpec((tm, width), lambda i: (i, 0))
    const = lambda a: pl.BlockSpec(a.shape, lambda i: (0,) * a.ndim)
    widths = (NSA_WIDTH, NSA_WIDTH, LANES, GMLP_WIDTH, GMLP_WIDTH,
              2 * KV_WIDTH, 2 * KV_WIDTH, 2 * KV_WIDTH)
    return pl.pallas_call(
        _inproj_sample_kernel,
        grid=(n // tm,),
        in_specs=[row(d), const(wt), const(cos), const(sin), const(ln_g), const(ln_b)],
        out_specs=[row(wd) for wd in widths],
        out_shape=[jax.ShapeDtypeStruct((n, wd), F32) for wd in widths],
        compiler_params=pltpu.CompilerParams(dimension_semantics=("parallel",),
                                             vmem_limit_bytes=VMEM_LIMIT),
        name="inproj_sample",
    )(x, wt, cos, sin, ln_g, ln_b)


def _compress_rows(chunk, pe_ref, w1_ref, b1_ref, w2_ref):
    outs = []
    for k in range(2):
        a = jnp.concatenate([chunk(l, k) for l in range(CMP_BLOCK)], axis=1)
        a = (a + pe_ref[k]).astype(BF16)
        h = jax.nn.silu(_dot(a, w1_ref[k]) + b1_ref[k])
        outs.append(_dot(h.astype(BF16), w2_ref[k]))
    return jnp.concatenate(outs, axis=1)


def _compress_prompt_kernel(x_ref, pe_ref, w1_ref, b1_ref, w2_ref, o_ref):
    row_w = 2 * KV_WIDTH
    chunk = lambda l, k: x_ref[:, l * row_w + k * KV_WIDTH:l * row_w + (k + 1) * KV_WIDTH]
    o_ref[...] = _compress_rows(chunk, pe_ref, w1_ref, b1_ref, w2_ref)


def _compress_prompt(x, pe, w1, b1, w2, tm):
    n = x.shape[0]
    const = lambda a: pl.BlockSpec(a.shape, lambda i: (0,) * a.ndim)
    return pl.pallas_call(
        _compress_prompt_kernel,
        grid=(n // tm,),
        in_specs=[pl.BlockSpec((tm, x.shape[1]), lambda i: (i, 0)),
                  const(pe), const(w1), const(b1), const(w2)],
        out_specs=pl.BlockSpec((tm, 2 * KV_WIDTH), lambda i: (i, 0)),
        out_shape=jax.ShapeDtypeStruct((n, 2 * KV_WIDTH), F32),
        compiler_params=pltpu.CompilerParams(dimension_semantics=("parallel",),
                                             vmem_limit_bytes=VMEM_LIMIT),
        name="compress_prompt",
    )(x, pe, w1, b1, w2)


def _compress_sample_kernel(n_seq, n_pages, pt_ref, *refs):
    page_refs = refs[:n_seq * n_pages]
    pe_ref, w1_ref, b1_ref, w2_ref, o_ref, posk_ref, posv_ref = refs[n_seq * n_pages:]
    for j, page_ref in enumerate(page_refs):
        posk_ref[j * PAGE_SIZE:(j + 1) * PAGE_SIZE, :] = page_ref[:KV_WIDTH, :].T
        posv_ref[j * PAGE_SIZE:(j + 1) * PAGE_SIZE, :] = page_ref[KV_WIDTH:, :].T
    m = n_seq * n_pages * (PAGE_SIZE // CMP_BLOCK)
    chunk = lambda l, k: (posk_ref, posv_ref)[k][pl.ds(l, m, stride=CMP_BLOCK), :]
    o_ref[...] = _compress_rows(chunk, pe_ref, w1_ref, b1_ref, w2_ref)


def _compress_sample(pool, layer, page_table, pe, w1, b1, w2, n_seq):
    db, n_pages = page_table.shape
    rows = n_seq * n_pages * (PAGE_SIZE // CMP_BLOCK)
    const = lambda a: pl.BlockSpec(a.shape, lambda i, pt: (0,) * a.ndim)

    def page_spec(s, p):
        return pl.BlockSpec((None, None) + pool.shape[2:],
                            lambda i, pt: (layer, pt[i * n_seq + s, p], 0, 0))

    grid_spec = pltpu.PrefetchScalarGridSpec(
        num_scalar_prefetch=1,
        grid=(db // n_seq,),
        in_specs=[page_spec(s, p) for s in range(n_seq) for p in range(n_pages)]
        + [const(pe), const(w1), const(b1), const(w2)],
        out_specs=pl.BlockSpec((rows, 2 * KV_WIDTH), lambda i, pt: (i, 0)),
        scratch_shapes=[pltpu.VMEM((n_seq * n_pages * PAGE_SIZE, KV_WIDTH), F32)] * 2,
    )
    return pl.pallas_call(
        functools.partial(_compress_sample_kernel, n_seq, n_pages),
        grid_spec=grid_spec,
        out_shape=jax.ShapeDtypeStruct((db * n_pages * (PAGE_SIZE // CMP_BLOCK), 2 * KV_WIDTH), F32),
        compiler_params=pltpu.CompilerParams(dimension_semantics=("parallel",),
                                             vmem_limit_bytes=VMEM_LIMIT),
        name="compress_sample",
    )(page_table, *([pool] * (n_seq * n_pages)), pe, w1, b1, w2)


def _stack_heads(q, tq):
    lane_lo = lax.broadcasted_iota(jnp.int32, (tq, LANES), 1) < HEAD_DIM
    pieces = []
    for hh in range(NSA_HEADS):
        g = hh // GROUP
        c = hh // 2
        chunk = q[:, c * LANES:(c + 1) * LANES]
        if hh % 2 != g:
            chunk = pltpu.roll(chunk, HEAD_DIM, 1)
        pieces.append(jnp.where(lane_lo if g == 0 else jnp.logical_not(lane_lo), chunk, 0.0))
    return jnp.concatenate(pieces, axis=0).astype(BF16)


def _cmp_branch(qraw, ckv, qpos, tq):
    nb = MAX_SLC_BLOCKS
    kc = jnp.concatenate([ckv[:, 0:KV_WIDTH], ckv[:, 2 * KV_WIDTH:3 * KV_WIDTH]], axis=0).astype(BF16)
    vc = jnp.concatenate([ckv[:, KV_WIDTH:2 * KV_WIDTH], ckv[:, 3 * KV_WIDTH:]], axis=0).astype(BF16)
    s = _dot_nt(qraw, kc).reshape(NSA_HEADS, tq, 2 * nb)
    lane = lax.broadcasted_iota(jnp.int32, (tq, 2 * nb), 1)
    blk = jnp.where(lane < nb, 2 * lane, 2 * (lane - nb) + 1)
    mask = ((blk + 1) * CMP_BLOCK - 1) <= qpos
    p = _softmax_masked(s, mask[None])
    o_cmp = _dot(p.reshape(NSA_HEADS * tq, 2 * nb).astype(BF16), vc)
    scores = []
    sblk = lax.broadcasted_iota(jnp.int32, (tq, nb), 1)
    cur = qpos // SLC_BLOCK
    valid = sblk <= cur
    forced = ((sblk == 0) | (sblk == cur) | (sblk == cur - 1)).astype(F32)
    for g in range(KV_HEADS):
        imp = p[g * GROUP]
        for r in range(1, GROUP):
            imp = imp + p[g * GROUP + r]
        imp = imp[:, :nb] + imp[:, nb:]
        scores.append(jnp.where(valid, imp + FORCE_BONUS * forced, -1.0))
    return o_cmp, scores, valid


def _rank_blocks_wide(score, st_ref, cnt_ref, n_chunks):
    st_ref[...] = score.T
    cnt_ref[...] = jnp.zeros_like(cnt_ref)
    sub = lax.broadcasted_iota(jnp.int32, (SUBLANES, LANES), 0)
    n_vregs = MAX_SLC_BLOCKS // SUBLANES
    for c in range(n_vregs):
        @pl.when(c < n_chunks)
        def _():
            tiles = [st_ref[v * SUBLANES:(v + 1) * SUBLANES, :] for v in range(n_vregs)]
            cnts = [cnt_ref[v * SUBLANES:(v + 1) * SUBLANES, :] for v in range(n_vregs)]
            for ii in range(SUBLANES):
                row = tiles[c][ii:ii + 1, :]
                for v in range(n_vregs):
                    if v < c:
                        inc = jnp.where(row > tiles[v], 1.0, 0.0)
                    elif v > c:
                        inc = jnp.where(row >= tiles[v], 1.0, 0.0)
                    else:
                        inc = jnp.where(sub > ii, jnp.where(row >= tiles[v], 1.0, 0.0),
                                        jnp.where(row > tiles[v], 1.0, 0.0))
                    cnts[v] = cnts[v] + inc
            for v in range(n_vregs):
                cnt_ref[v * SUBLANES:(v + 1) * SUBLANES, :] = cnts[v]
    return cnt_ref[...].T


def _rank_blocks_narrow(score, n_blocks):
    lane = lax.broadcasted_iota(jnp.int32, score.shape, 1)
    cnt = jnp.zeros(score.shape, F32)
    for i in range(n_blocks):
        col = score[:, i:i + 1]
        cnt = cnt + jnp.where(lane > i, jnp.where(col >= score, 1.0, 0.0),
                              jnp.where(col > score, 1.0, 0.0))
    return cnt


def _slc_branch(q_aug, kv_ref, et_ref, n_full, qpos, tq, tk):
    ones_rows = jnp.where(lax.broadcasted_iota(jnp.int32, (HEAD_DIM, tk), 0) == 0, 1.0, 0.0).astype(BF16)

    def tile(kt, carry, diagonal):
        ms, accs = carry
        off = pl.multiple_of(kt * tk, tk)
        ka = jnp.concatenate([et_ref[:, pl.ds(off, tk)], kv_ref[:KV_WIDTH, pl.ds(off, tk)]], axis=0)
        s = _dot(q_aug, ka)
        if diagonal:
            visible = (off + lax.broadcasted_iota(jnp.int32, (tq, tk), 1)) <= qpos
        ps, m_new, alphas = [], [], []
        for hh in range(NSA_HEADS):
            s_h = s[hh * tq:(hh + 1) * tq, :]
            if diagonal:
                s_h = jnp.where(visible, s_h, NEG_INF)
            m_h = jnp.maximum(ms[hh], jnp.max(s_h, axis=-1, keepdims=True))
            ps.append(jnp.exp2(s_h - m_h).astype(BF16))
            alphas.append(jnp.exp2(ms[hh] - m_h))
            m_new.append(m_h)
        acc_new = []
        for g in range(KV_HEADS):
            v0 = KV_WIDTH + g * HEAD_DIM
            va = jnp.concatenate([kv_ref[v0:v0 + HEAD_DIM, pl.ds(off, tk)], ones_rows], axis=0)
            pv = _dot_nt(jnp.concatenate(ps[g * GROUP:(g + 1) * GROUP], axis=0), va)
            alpha = jnp.concatenate(alphas[g * GROUP:(g + 1) * GROUP], axis=0)
            acc_new.append(alpha * accs[g] + pv)
        return tuple(m_new), tuple(acc_new)

    init = (tuple(jnp.full((tq, 1), NEG_INF, F32) for _ in range(NSA_HEADS)),
            tuple(jnp.zeros((GROUP * tq, KV_WIDTH), F32) for _ in range(KV_HEADS)))
    odd = n_full % 2
    carry = lax.cond(odd == 1, lambda c: tile(0, c, False), lambda c: c, init)

    def pair(j, c):
        kt = odd + 2 * j
        return tile(kt + 1, tile(kt, c, False), False)

    carry = lax.fori_loop(0, n_full // 2, pair, carry)
    _, accs = tile(n_full, carry, True)
    outs = [a * (1.0 / jnp.maximum(a[:, HEAD_DIM:HEAD_DIM + 1], 1e-30)) for a in accs]
    return jnp.concatenate([outs[0], pltpu.roll(outs[1], HEAD_DIM, 1)], axis=0)


def _softmax_pv(s, mask, vt, tq):
    nk = s.shape[1]
    rows = NSA_HEADS * tq
    if rows <= LANES:
        s3 = jnp.where(mask[None], s.reshape(NSA_HEADS, tq, nk), NEG_INF)
        p = jnp.exp2(s3 - jnp.max(s3, axis=-1, keepdims=True)).reshape(rows, nk).astype(BF16)
        if rows < LANES:
            p = jnp.concatenate([p, jnp.zeros((LANES - rows, nk), BF16)], axis=0)
        ones_rows = jnp.where(lax.broadcasted_iota(jnp.int32, (2 * SUBLANES, nk), 0) == 0,
                              1.0, 0.0).astype(BF16)
        ot = _dot_nt(jnp.concatenate([vt, ones_rows], axis=0), p)
        return (ot[:KV_WIDTH] * (1.0 / jnp.maximum(ot[KV_WIDTH:KV_WIDTH + 1], 1e-30))).T[:rows]
    ones_rows = jnp.where(lax.broadcasted_iota(jnp.int32, (HEAD_DIM, nk), 0) == 0, 1.0, 0.0).astype(BF16)
    outs = []
    for g in range(KV_HEADS):
        ps = []
        for hh in range(g * GROUP, (g + 1) * GROUP):
            s_h = jnp.where(mask, s[hh * tq:(hh + 1) * tq, :], NEG_INF)
            ps.append(jnp.exp2(s_h - jnp.max(s_h, axis=-1, keepdims=True)).astype(BF16))
        va = jnp.concatenate([vt[g * HEAD_DIM:(g + 1) * HEAD_DIM, :], ones_rows], axis=0)
        pv = _dot_nt(jnp.concatenate(ps, axis=0), va)
        outs.append(pv * (1.0 / jnp.maximum(pv[:, HEAD_DIM:HEAD_DIM + 1], 1e-30)))
    return jnp.concatenate([outs[0], pltpu.roll(outs[1], HEAD_DIM, 1)], axis=0)


def _slc_single(bias_rows, qrot, kt, vt, et, qpos, tq):
    nk = kt.shape[1]
    s = _dot(bias_rows, et) + _dot(qrot, kt)
    visible = lax.broadcasted_iota(jnp.int32, (tq, nk), 1) <= qpos
    return _softmax_pv(s, visible, vt, tq)


def _win_branch(qrot, kt, vt, kpos, qpos, tq):
    diff = qpos - kpos
    return _softmax_pv(_dot(qrot, kt), (diff >= 0) & (diff < WINDOW), vt, tq)


def _combine(o_cmp, o_slc, o_win, gates, tq):
    lane_lo = lax.broadcasted_iota(jnp.int32, (tq, LANES), 1) < HEAD_DIM
    heads = []
    for hh in range(NSA_HEADS):
        tot = None
        for br, o in enumerate((o_cmp, o_slc, o_win)):
            col = br * NSA_HEADS + hh
            term = gates[:, col:col + 1] * o[hh * tq:(hh + 1) * tq, :]
            tot = term if tot is None else tot + term
        heads.append(tot)
    chunks = []
    for c in range(NSA_HEADS // 2):
        a, b = heads[2 * c], heads[2 * c + 1]
        if c // 2 == 0:
            chunks.append(jnp.where(lane_lo, a, pltpu.roll(b, HEAD_DIM, 1)))
        else:
            chunks.append(jnp.where(lane_lo, pltpu.roll(a, HEAD_DIM, 1), b))
    return jnp.concatenate(chunks, axis=1)


def _selection_bias(rank, valid):
    return jnp.where((rank < float(N_SELECT)) & valid, 0.0, NEG_INF).astype(BF16)


def _pad_ckv(ckv):
    n = ckv.shape[0]
    if n == MAX_SLC_BLOCKS:
        return ckv
    return jnp.concatenate([ckv, jnp.zeros((MAX_SLC_BLOCKS - n, ckv.shape[1]), F32)], axis=0)


def _nsa_prompt_kernel(qraw_ref, qrot_ref, gate_ref, ckv_ref, slc_ref, win_ref, et_ref,
                       o_ref, st_ref, cnt_ref):
    tq = QUERY_BLOCK
    i = pl.program_id(1)
    qpos0 = i * tq
    qpos = qpos0 + lax.broadcasted_iota(jnp.int32, (tq, 1), 0)
    qraw = _stack_heads(qraw_ref[...], tq)
    qrot = _stack_heads(qrot_ref[...], tq)

    o_cmp, scores, valid = _cmp_branch(qraw, _pad_ckv(ckv_ref[...]), qpos, tq)
    n_chunks = (qpos0 + tq - 1) // (SLC_BLOCK * SUBLANES) + 1
    bias = []
    for g in range(KV_HEADS):
        rank = _rank_blocks_wide(scores[g], st_ref, cnt_ref, n_chunks)
        bias.append(_selection_bias(rank, valid))
    bias_rows = jnp.concatenate([bias[0]] * GROUP + [bias[1]] * GROUP, axis=0)
    q_aug = jnp.concatenate([bias_rows, qrot], axis=1)
    o_slc = _slc_branch(q_aug, slc_ref, et_ref, qpos0 // SLC_KEY_TILE, qpos, tq, SLC_KEY_TILE)

    nk = WINDOW + tq
    start = pl.multiple_of(jnp.maximum(qpos0 - WINDOW, 0), tq)
    kpos = start + lax.broadcasted_iota(jnp.int32, (1, nk), 1)
    o_win = _win_branch(qrot, win_ref[:KV_WIDTH, pl.ds(start, nk)], win_ref[KV_WIDTH:, pl.ds(start, nk)],
                        kpos, qpos, tq)

    o_ref[...] = _combine(o_cmp, o_slc, o_win, gate_ref[...], tq)


def _nsa_prompt(qraw, qrot, gates, ckv, slcb, winb, et):
    b, t, _ = qraw.shape
    tq = QUERY_BLOCK
    blk = lambda width: pl.BlockSpec((None, tq, width), lambda bi, i: (bi, i, 0))
    per_batch = lambda a: pl.BlockSpec((None,) + a.shape[1:], lambda bi, i: (bi, 0, 0))
    return pl.pallas_call(
        _nsa_prompt_kernel,
        grid=(b, t // tq),
        in_specs=[blk(NSA_WIDTH), blk(NSA_WIDTH), blk(LANES), per_batch(ckv), per_batch(slcb),
                  per_batch(winb), pl.BlockSpec(et.shape, lambda bi, i: (0, 0))],
        out_specs=blk(NSA_WIDTH),
        out_shape=jax.ShapeDtypeStruct((b, t, NSA_WIDTH), F32),
        scratch_shapes=[pltpu.VMEM((MAX_SLC_BLOCKS, tq), F32), pltpu.VMEM((MAX_SLC_BLOCKS, tq), F32)],
        compiler_params=pltpu.CompilerParams(dimension_semantics=("parallel", "arbitrary"),
                                             vmem_limit_bytes=VMEM_LIMIT),
        name="nsa_prompt",
    )(qraw, qrot, gates, ckv, slcb, winb, et)


def _nsa_sample_kernel(n_seq, n_pages, past, s_new, pt_ref, *refs):
    n_in = n_seq * n_pages
    page_refs = refs[:n_in]
    (qraw_ref, qrot_ref, gate_ref, ckv_ref, slcn_ref, wins_ref, winn_ref, et_ref,
     o_ref, wout_ref) = refs[n_in:]
    tq = SAMPLE_ROWS
    qpos = past + lax.broadcasted_iota(jnp.int32, (tq, 1), 0)
    n_buf = wins_ref.shape[2]
    n_blocks = (past + tq - 1) // SLC_BLOCK + 1
    pad_rows = jnp.zeros((LANES - tq, 2 * KV_WIDTH), F32)
    wpos = past - n_buf + lax.broadcasted_iota(jnp.int32, (1, n_buf + LANES), 1)
    lane = lax.broadcasted_iota(jnp.int32, (2 * KV_WIDTH, LANES), 1)
    for s in range(n_seq):
        slc_new = jnp.concatenate([slcn_ref[s], pad_rows], axis=0).T
        tiles = [page_refs[s * n_pages + p][...].astype(BF16) for p in range(n_pages)]
        tiles.append(slc_new.astype(BF16))
        kt = jnp.concatenate([tl[:KV_WIDTH] for tl in tiles], axis=1)
        vt = jnp.concatenate([tl[KV_WIDTH:] for tl in tiles], axis=1)

        qraw = _stack_heads(qraw_ref[s], tq)
        qrot = _stack_heads(qrot_ref[s], tq)
        o_cmp, scores, valid = _cmp_branch(qraw, _pad_ckv(ckv_ref[s]), qpos, tq)
        bias = [_selection_bias(_rank_blocks_narrow(scores[g], n_blocks), valid)
                for g in range(KV_HEADS)]
        bias_rows = jnp.concatenate([bias[0]] * GROUP + [bias[1]] * GROUP, axis=0)
        o_slc = _slc_single(bias_rows, qrot, kt, vt, et_ref[...], qpos, tq)

        state = wins_ref[s]
        win_new = jnp.concatenate([winn_ref[s], pad_rows], axis=0).T
        kt = jnp.concatenate([state[:KV_WIDTH], win_new[:KV_WIDTH]], axis=1).astype(BF16)
        vt = jnp.concatenate([state[KV_WIDTH:], win_new[KV_WIDTH:]], axis=1).astype(BF16)
        o_win = _win_branch(qrot, kt, vt, wpos, qpos, tq)
        o_ref[s] = _combine(o_cmp, o_slc, o_win, gate_ref[s], tq)

        shifted = pltpu.roll(state, n_buf - s_new, 1)
        tail = jnp.where(lane >= LANES - s_new, pltpu.roll(win_new, LANES - s_new, 1),
                         shifted[:, n_buf - LANES:])
        wout_ref[s, :, :n_buf - LANES] = shifted[:, :n_buf - LANES]
        wout_ref[s, :, n_buf - LANES:] = tail


def _nsa_sample(pool, win_state, layer, page_table, qraw, qrot, gates, ckv, slc_new, win_new, et,
                n_seq, s_new):
    db, n_pages = page_table.shape
    past = n_pages * PAGE_SIZE
    tq = SAMPLE_ROWS
    n_buf = win_state.shape[3]
    seq = lambda a: pl.BlockSpec((n_seq,) + a.shape[1:], lambda i, pt: (i, 0, 0))

    def page_spec(s, p):
        return pl.BlockSpec((None, None) + pool.shape[2:],
                            lambda i, pt: (layer, pt[i * n_seq + s, p], 0, 0))

    state_spec = pl.BlockSpec((None, n_seq) + win_state.shape[2:], lambda i, pt: (layer, i, 0, 0))
    grid_spec = pltpu.PrefetchScalarGridSpec(
        num_scalar_prefetch=1,
        grid=(db // n_seq,),
        in_specs=[page_spec(s, p) for s in range(n_seq) for p in range(n_pages)]
        + [seq(qraw), seq(qrot), seq(gates), seq(ckv), seq(slc_new), state_spec, seq(win_new),
           pl.BlockSpec(et.shape, lambda i, pt: (0, 0))],
        out_specs=[pl.BlockSpec((n_seq, tq, NSA_WIDTH), lambda i, pt: (i, 0, 0)),
                   pl.BlockSpec((n_seq,) + win_state.shape[2:], lambda i, pt: (i, 0, 0))],
    )
    return pl.pallas_call(
        functools.partial(_nsa_sample_kernel, n_seq, n_pages, past, s_new),
        grid_spec=grid_spec,
        out_shape=[jax.ShapeDtypeStruct((db, tq, NSA_WIDTH), F32),
                   jax.ShapeDtypeStruct((db,) + win_state.shape[2:], F32)],
        compiler_params=pltpu.CompilerParams(dimension_semantics=("parallel",),
                                             vmem_limit_bytes=VMEM_LIMIT),
        name="nsa_sample",
    )(page_table, *([pool] * (n_seq * n_pages)), qraw, qrot, gates, ckv, slc_new, win_state, win_new, et)


def _post_kernel(chunk, alpha, n_ffn_chunks, x_ref, o_ref, u_ref, v_ref, gw_ref, gb_ref,
                 wout_ref, ln1g_ref, ln1b_ref, wg_ref, wu_ref, wd_ref, ln2g_ref, ln2b_ref, y_ref):
    tm = x_ref.shape[0]
    c = GMLP_CHUNK
    row = lax.broadcasted_iota(jnp.int32, (c, c), 0)
    col = lax.broadcasted_iota(jnp.int32, (c, c), 1)
    causal = (col <= row) & ((row // chunk) == (col // chunk))
    w_cat = jnp.concatenate([jnp.where(causal, gw_ref[h], 0.0) for h in range(GMLP_GROUPS)],
                            axis=1).astype(BF16)
    lane_group = lax.broadcasted_iota(jnp.int32, (c, GMLP_WIDTH), 1) // (GMLP_WIDTH // GMLP_GROUPS)
    mixed = []
    for sub in range(tm // c):
        v = v_ref[sub * c:(sub + 1) * c, :]
        v_diag = jnp.concatenate([jnp.where(lane_group == h, v, 0.0) for h in range(GMLP_GROUPS)],
                                 axis=0).astype(BF16)
        s = _dot(w_cat, v_diag) + gb_ref[...]
        mixed.append(u_ref[sub * c:(sub + 1) * c, :] * s)
    o_gmlp = jnp.concatenate(mixed, axis=0)
    mix = jnp.concatenate([o_ref[...], o_gmlp], axis=1).astype(BF16)
    h = _dot(mix, wout_ref[...])
    x1 = _layer_norm(alpha * x_ref[...] + h, ln1g_ref[...], ln1b_ref[...])
    x1b = x1.astype(BF16)
    f = None
    fc = wg_ref.shape[1] // n_ffn_chunks
    for j in range(n_ffn_chunks):
        gate = _dot(x1b, wg_ref[:, j * fc:(j + 1) * fc])
        up = _dot(x1b, wu_ref[:, j * fc:(j + 1) * fc])
        a = (jax.nn.silu(gate) * up).astype(BF16)
        part = _dot(a, wd_ref[j * fc:(j + 1) * fc, :])
        f = part if f is None else f + part
    y_ref[...] = _layer_norm(alpha * x1 + f, ln2g_ref[...], ln2b_ref[...])


def _post(x, o_nsa, u, v, gw, gb, wout, ln1g, ln1b, wg, wu, wd, ln2g, ln2b, chunk, alpha, tm):
    n, d = x.shape
    ffn = wg.shape[1]
    n_ffn_chunks = next(k for k in (4, 2, 1) if ffn % (k * LANES) == 0)
    row = lambda width: pl.BlockSpec((tm, width), lambda i: (i, 0))
    const = lambda a: pl.BlockSpec(a.shape, lambda i: (0,) * a.ndim, pipeline_mode=pl.Buffered(1))
    consts = (gw, gb, wout, ln1g, ln1b, wg, wu, wd, ln2g, ln2b)
    return pl.pallas_call(
        functools.partial(_post_kernel, chunk, alpha, n_ffn_chunks),
        grid=(n // tm,),
        in_specs=[row(d), row(NSA_WIDTH), row(GMLP_WIDTH), row(GMLP_WIDTH)] + [const(a) for a in consts],
        out_specs=row(d),
        out_shape=jax.ShapeDtypeStruct((n, d), F32),
        compiler_params=pltpu.CompilerParams(dimension_semantics=("parallel",),
                                             vmem_limit_bytes=VMEM_LIMIT),
        name="post",
    )(x, o_nsa, u, v, *consts)


def _pack_w_in_t(w_in):
    wt = w_in.T
    o_gate = NSA_WIDTH + 3 * 2 * KV_WIDTH
    o_gmlp = o_gate + 3 * NSA_HEADS
    gate = jnp.pad(wt[o_gate:o_gmlp], ((0, LANES - 3 * NSA_HEADS), (0, 0)))
    return jnp.concatenate([wt[:o_gate], gate, wt[o_gmlp:]], axis=0).astype(BF16)


def _pack_compress(pe, w1, b1, w2):
    eye = jnp.eye(KV_HEADS, dtype=F32)
    w1p = jnp.einsum('kldh,gG->klgdGh', w1, eye).reshape(2, CMP_BLOCK * KV_WIDTH, KV_HEADS * CMP_HIDDEN)
    pep = jnp.broadcast_to(pe[:, :, None, :], (2, CMP_BLOCK, KV_HEADS, HEAD_DIM)).reshape(2, 1, -1)
    b1p = jnp.tile(b1, (1, KV_HEADS))[:, None, :]
    w2p = jnp.einsum('khd,gG->kghGd', w2, eye).reshape(2, KV_HEADS * CMP_HIDDEN, KV_WIDTH)
    return pep, w1p.astype(BF16), b1p, w2p.astype(BF16)


def _pack_gmlp(ws, bs, chunk):
    reps = GMLP_CHUNK // chunk
    gw = jnp.tile(ws[:, :chunk, :chunk], (1, reps, reps))
    gb = jnp.tile(jnp.repeat(bs[:, :chunk].T, GMLP_WIDTH // GMLP_GROUPS, axis=1), (reps, 1))
    return gw, gb


def _rope_tables(pos):
    half = HEAD_DIM // 2
    inv_freq = ROPE_THETA ** (-jnp.arange(half, dtype=F32) / half)
    ang = pos.astype(F32)[:, None] * inv_freq[None, :]
    cos, sin = jnp.cos(ang), jnp.sin(ang)
    return jnp.concatenate([cos, cos], axis=1), jnp.concatenate([-sin, sin], axis=1)


def _block_indicator(n_keys):
    key_block = np.arange(n_keys)[None, :] // SLC_BLOCK
    return jnp.asarray(key_block == np.arange(MAX_SLC_BLOCKS)[:, None], dtype=BF16)


def _feature_major(a):
    lead = a.shape[:-4]
    n = len(lead)
    perm = tuple(range(n)) + (n + 1, n + 2, n + 3, n)
    return jnp.transpose(a, perm).reshape(lead + (2 * KV_WIDTH, a.shape[-4]))


def _position_major(a):
    lead = a.shape[:-2]
    n = len(lead)
    a = a.reshape(lead + (2, KV_HEADS, HEAD_DIM, a.shape[-1]))
    return jnp.transpose(a, tuple(range(n)) + (n + 3, n, n + 1, n + 2))


def kernel(x_prompt, x_sample, cache_cmp_kv, cache_slc_kv, state_win_kv, page_table,
           w_in, cmp_pe, cmp_w1, cmp_b1, cmp_w2, gmlp_ln_g, gmlp_ln_b, gmlp_ws, gmlp_bs,
           w_out, ln1_g, ln1_b, w_gate, w_up, w_down, ln2_g, ln2_b):
    depth = w_in.shape[0]
    b, t, d = x_prompt.shape
    db, s_new, _ = x_sample.shape
    n_pages = page_table.shape[1]
    past = n_pages * PAGE_SIZE
    n_buf = state_win_kv.shape[2]
    kv_tail = (2, KV_HEADS, HEAD_DIM)
    rows_s = SAMPLE_ROWS
    assert t % SLC_KEY_TILE == 0 and t >= WINDOW + QUERY_BLOCK and t // SLC_BLOCK <= MAX_SLC_BLOCKS
    assert s_new <= rows_s and (past + rows_s - 1) // SLC_BLOCK < MAX_SLC_BLOCKS
    assert past % CMP_BLOCK == 0 and past % CMP_BLOCK + s_new < CMP_BLOCK
    assert n_buf % LANES == 0 and n_buf >= LANES
    alpha = (2 * depth) ** 0.25

    tm_p = 512
    tm_s = min(512, db * rows_s)
    seq_per_step = 2 if db % 2 == 0 else 1
    attn_seq_per_step = 4 if db % 4 == 0 else seq_per_step
    cos_p, sin_p = _rope_tables(jnp.arange(t, dtype=jnp.int32))
    cos_s, sin_s = _rope_tables(past + jnp.arange(rows_s, dtype=jnp.int32))
    lane_reps = LANES // HEAD_DIM
    cos_pl, sin_pl = jnp.tile(cos_p, (1, lane_reps)), jnp.tile(sin_p, (1, lane_reps))
    cos_pt, sin_pt = cos_p.T, sin_p.T
    cos_s = jnp.tile(cos_s, (tm_s // rows_s, lane_reps))
    sin_s = jnp.tile(sin_s, (tm_s // rows_s, lane_reps))
    et_p = _block_indicator(t)
    et_s = _block_indicator(past + LANES)
    pool_cmp = _feature_major(cache_cmp_kv)
    pool_slc = _feature_major(cache_slc_kv)
    win_state = _feature_major(state_win_kv)

    yp = x_prompt.reshape(b * t, d)
    ys = jnp.pad(x_sample, ((0, 0), (0, rows_s - s_new), (0, 0))).reshape(db * rows_s, d)
    outs = {k: [] for k in ("p_cmp", "p_slc", "p_win", "s_cmp", "s_slc", "s_win", "s_v")}
    n_keep = min(WINDOW, t)
    cmp_tile = min(128, b * t // CMP_BLOCK)

    for l in range(depth):
        wt = _pack_w_in_t(w_in[l])
        pe_p, w1_p, b1_p, w2_p = _pack_compress(cmp_pe[l], cmp_w1[l], cmp_b1[l], cmp_w2[l])
        ln_g = gmlp_ln_g[l][None, :]
        ln_b = gmlp_ln_b[l][None, :]
        tail = (w_out[l].astype(BF16), ln1_g[l][None, :], ln1_b[l][None, :], w_gate[l].astype(BF16),
                w_up[l].astype(BF16), w_down[l].astype(BF16), ln2_g[l][None, :], ln2_b[l][None, :])

        qraw, qrot, gates, u, v, cmp_kv, cmpt, slct, wint, slcb, winb = _inproj_prompt(
            yp, wt, cos_pl, sin_pl, cos_pt, sin_pt, ln_g, ln_b, tm_p, b, t)
        ckv = _compress_prompt(cmp_kv.reshape(b * t // CMP_BLOCK, CMP_BLOCK * 2 * KV_WIDTH),
                               pe_p, w1_p, b1_p, w2_p, cmp_tile)
        r3 = lambda a: a.reshape(b, t, a.shape[-1])
        o_nsa = _nsa_prompt(r3(qraw), r3(qrot), r3(gates), ckv.reshape(b, t // SLC_BLOCK, 4 * KV_WIDTH),
                            slcb, winb, et_p)
        gw, gb = _pack_gmlp(gmlp_ws[l], gmlp_bs[l], GMLP_CHUNK)
        yp = _post(yp, o_nsa.reshape(b * t, NSA_WIDTH), u, v, gw, gb, *tail, GMLP_CHUNK, alpha, tm_p)
        outs["p_cmp"].append(_position_major(cmpt))
        outs["p_slc"].append(_position_major(slct))
        outs["p_win"].append(_position_major(wint[:, :, t - n_keep:]))

        qraw, qrot, gates, u, v, cmp_kv, slc_kv, win_kv = _inproj_sample(
            ys, wt, cos_s, sin_s, ln_g, ln_b, tm_s)
        ckv = _compress_sample(pool_cmp, l, page_table, pe_p, w1_p, b1_p, w2_p, seq_per_step)
        s3 = lambda a: a.reshape(db, rows_s, a.shape[-1])
        o_nsa, win_next = _nsa_sample(pool_slc, win_state, l, page_table, s3(qraw), s3(qrot), s3(gates),
                                      ckv.reshape(db, past // SLC_BLOCK, 4 * KV_WIDTH),
                                      s3(slc_kv), s3(win_kv), et_s, attn_seq_per_step, s_new)
        gw, gb = _pack_gmlp(gmlp_ws[l], gmlp_bs[l], rows_s)
        ys = _post(ys, o_nsa.reshape(db * rows_s, NSA_WIDTH), u, v, gw, gb, *tail, rows_s, alpha, tm_s)
        new = lambda a: a.reshape((db, rows_s) + a.shape[1:])[:, :s_new]
        outs["s_cmp"].append(new(cmp_kv).reshape((db, s_new) + kv_tail))
        outs["s_slc"].append(new(slc_kv).reshape((db, s_new) + kv_tail))
        outs["s_win"].append(_position_major(win_next))
        outs["s_v"].append(new(v))

    y_sample = ys.reshape(db, rows_s, d)[:, :s_new]
    return (yp.reshape(b, t, d), y_sample, jnp.stack(outs["p_cmp"]), jnp.stack(outs["p_slc"]),
            jnp.stack(outs["p_win"]), jnp.stack(outs["s_cmp"]), jnp.stack(outs["s_slc"]),
            jnp.stack(outs["s_win"]), jnp.stack(outs["s_v"]))
```

```python
import functools
import math

import numpy as np
import jax
import jax.numpy as jnp
from jax import lax
from jax.experimental import pallas as pl
from jax.experimental.pallas import tpu as pltpu

F32 = jnp.float32
BF16 = jnp.bfloat16

LANES = 128
SUBLANES = 8
HEAD_DIM = 64
NSA_HEADS = 8
KV_HEADS = 2
GROUP = NSA_HEADS // KV_HEADS
NSA_WIDTH = NSA_HEADS * HEAD_DIM
KV_WIDTH = KV_HEADS * HEAD_DIM
GMLP_WIDTH = 512
GMLP_GROUPS = 8
GMLP_CHUNK = 128
CMP_BLOCK = 32
CMP_HIDDEN = 256
CMP_PITCH = CMP_BLOCK + SUBLANES
SLC_BLOCK = 64
N_SELECT = 16
WINDOW = 512
QUERY_BLOCK = 128
PAGE_SIZE = 128
FORCE_BONUS = 1.0e4
ROPE_THETA = 10000.0
LN_EPS = 1e-5
NEG_INF = -1e30
MAX_SLC_BLOCKS = LANES
SAMPLE_ROWS = SUBLANES
SLC_KEY_TILE = 512
VMEM_LIMIT = 56 * 1024 * 1024
Q_SCALE = HEAD_DIM ** -0.5 * math.log2(math.e)

C_Q = 0
C_CMP = C_Q + NSA_WIDTH
C_SLC = C_CMP + 2 * KV_WIDTH
C_WIN = C_SLC + 2 * KV_WIDTH
C_GATE = C_WIN + 2 * KV_WIDTH
C_U = C_GATE + LANES
C_V = C_U + GMLP_WIDTH
C_END = C_V + GMLP_WIDTH


def _dot(a, b):
    return jnp.dot(a, b, preferred_element_type=F32)


def _dot_nt(a, b):
    return lax.dot_general(a, b, (((1,), (1,)), ((), ())), preferred_element_type=F32)


def _layer_norm(x, g, b):
    mu = jnp.mean(x, axis=-1, keepdims=True)
    xc = x - mu
    var = jnp.mean(xc * xc, axis=-1, keepdims=True)
    return xc * lax.rsqrt(var + LN_EPS) * g + b


def _softmax_masked(s, mask):
    s = jnp.where(mask, s, NEG_INF)
    m = jnp.max(s, axis=-1, keepdims=True)
    e = jnp.where(mask, jnp.exp2(s - m), 0.0)
    return e * (1.0 / jnp.maximum(jnp.sum(e, axis=-1, keepdims=True), 1e-30))


def _rope_lanes(c, cos, sin, first_half):
    partner = jnp.where(first_half, pltpu.roll(c, LANES - HEAD_DIM // 2, 1),
                        pltpu.roll(c, HEAD_DIM // 2, 1))
    return c * cos + partner * sin


def _inproj_common(xb, wt_ref, cos, sin, first_half, lng_ref, lnb_ref,
                   qraw_ref, qrot_ref, gate_ref, u_ref, v_ref):
    zq = _dot_nt(xb, wt_ref[C_Q:C_CMP, :])
    qraw_ref[...] = zq * Q_SCALE
    for c in range(NSA_WIDTH // LANES):
        qrot_ref[:, c * LANES:(c + 1) * LANES] = _rope_lanes(
            zq[:, c * LANES:(c + 1) * LANES], cos, sin, first_half) * Q_SCALE
    gate_ref[...] = jax.nn.sigmoid(_dot_nt(xb, wt_ref[C_GATE:C_U, :]))
    u_ref[...] = jax.nn.gelu(_dot_nt(xb, wt_ref[C_U:C_V, :]))
    zv = jax.nn.gelu(_dot_nt(xb, wt_ref[C_V:C_END, :]))
    v_ref[...] = _layer_norm(zv, lng_ref[...], lnb_ref[...])


def _inproj_prompt_kernel(n_carried, x_ref, wt_ref, cos_ref, sin_ref, cost_ref, sint_ref, lng_ref,
                          lnb_ref, *refs):
    (qraw_ref, qrot_ref, gate_ref, u_ref, v_ref,
     cmp_ref, cmpt_ref, slct_ref, wint_ref, slcb_ref, winb_ref) = refs[n_carried:]
    xb = x_ref[...].astype(BF16)
    cos = cos_ref[...]
    lane = lax.broadcasted_iota(jnp.int32, cos.shape, 1)
    first_half = (lane % HEAD_DIM) < (HEAD_DIM // 2)
    _inproj_common(xb, wt_ref, cos, sin_ref[...], first_half, lng_ref, lnb_ref,
                   qraw_ref, qrot_ref, gate_ref, u_ref, v_ref)
    cmp_ref[...] = _dot_nt(xb, wt_ref[C_CMP:C_SLC, :])
    cmpt_ref[...] = _dot_nt(wt_ref[C_CMP:C_SLC, :], xb)
    cost = cost_ref[...]
    sint = sint_ref[...]
    half = HEAD_DIM // 2
    for row0, f_ref, b_ref in ((C_SLC, slct_ref, slcb_ref), (C_WIN, wint_ref, winb_ref)):
        zt = _dot_nt(wt_ref[row0:row0 + 2 * KV_WIDTH, :], xb)
        for g in range(KV_HEADS):
            kg = zt[g * HEAD_DIM:(g + 1) * HEAD_DIM, :]
            partner = jnp.concatenate([kg[half:, :], kg[:half, :]], axis=0)
            kr = kg * cost + partner * sint
            f_ref[g * HEAD_DIM:(g + 1) * HEAD_DIM, :] = kr
            b_ref[g * HEAD_DIM:(g + 1) * HEAD_DIM, :] = kr.astype(BF16)
        f_ref[KV_WIDTH:, :] = zt[KV_WIDTH:, :]
        b_ref[KV_WIDTH:, :] = zt[KV_WIDTH:, :].astype(BF16)


def _inproj_sample_kernel(x_ref, wt_ref, cos_ref, sin_ref, lng_ref, lnb_ref,
                          qraw_ref, qrot_ref, gate_ref, u_ref, v_ref, cmp_ref, slc_ref, win_ref):
    xb = x_ref[...].astype(BF16)
    cos = cos_ref[...]
    sin = sin_ref[...]
    lane = lax.broadcasted_iota(jnp.int32, cos.shape, 1)
    first_half = (lane % HEAD_DIM) < (HEAD_DIM // 2)
    _inproj_common(xb, wt_ref, cos, sin, first_half, lng_ref, lnb_ref,
                   qraw_ref, qrot_ref, gate_ref, u_ref, v_ref)
    cmp_ref[...] = _dot_nt(xb, wt_ref[C_CMP:C_SLC, :])
    for row0, f_ref in ((C_SLC, slc_ref), (C_WIN, win_ref)):
        z = _dot_nt(xb, wt_ref[row0:row0 + 2 * KV_WIDTH, :])
        f_ref[:, :KV_WIDTH] = _rope_lanes(z[:, :KV_WIDTH], cos, sin, first_half)
        f_ref[:, KV_WIDTH:] = z[:, KV_WIDTH:]


def _inproj_prompt(x, wt, cos, sin, cost, sint, ln_g, ln_b, tm, b, t, depth, layer, carried):
    n, d = x.shape
    n_tab = t // tm
    row = lambda width: pl.BlockSpec((tm, width), lambda i: (i, 0))
    const = lambda a: pl.BlockSpec(a.shape, lambda i: (0,) * a.ndim)
    tab = pl.BlockSpec((tm, LANES), lambda i: (i % n_tab, 0))
    tabt = pl.BlockSpec((HEAD_DIM, tm), lambda i: (0, i % n_tab))
    fm = pl.BlockSpec((None, 2 * KV_WIDTH, tm), lambda i: (i // n_tab, 0, i % n_tab))
    fm_layer = pl.BlockSpec((None, None, 2 * KV_WIDTH, tm), lambda i: (layer, i // n_tab, 0, i % n_tab))
    keep_layer = pl.BlockSpec((None, None, 2 * KV_WIDTH, tm), lambda i: (layer, i // n_tab, 0, 0))
    row_out = ((NSA_WIDTH, F32), (NSA_WIDTH, F32), (LANES, F32), (GMLP_WIDTH, F32), (GMLP_WIDTH, F32),
               (2 * KV_WIDTH, F32))
    n_in = 8
    n_row = len(row_out)
    stacked = lambda width: jax.ShapeDtypeStruct((depth, b, 2 * KV_WIDTH, width), F32)
    return pl.pallas_call(
        functools.partial(_inproj_prompt_kernel, len(carried)),
        grid=(n // tm,),
        in_specs=[row(d), const(wt), tab, tab, tabt, tabt, const(ln_g), const(ln_b)]
        + [pl.BlockSpec(memory_space=pl.ANY)] * len(carried),
        out_specs=[row(wd) for wd, _ in row_out] + [fm_layer, fm_layer, keep_layer, fm, fm],
        out_shape=[jax.ShapeDtypeStruct((n, wd), dt) for wd, dt in row_out]
        + [stacked(t), stacked(t), stacked(tm)]
        + [jax.ShapeDtypeStruct((b, 2 * KV_WIDTH, t), BF16)] * 2,
        input_output_aliases={n_in + k: n_row + k for k in range(len(carried))},
        compiler_params=pltpu.CompilerParams(dimension_semantics=("arbitrary",),
                                             vmem_limit_bytes=VMEM_LIMIT),
        name="inproj_prompt",
    )(x, wt, cos, sin, cost, sint, ln_g, ln_b, *carried)


def _inproj_sample(x, wt, cos, sin, ln_g, ln_b, tm):
    n, d = x.shape
    row = lambda width: pl.BlockSpec((tm, width), lambda i: (i, 0))
    const = lambda a: pl.BlockSpec(a.shape, lambda i: (0,) * a.ndim)
    widths = (NSA_WIDTH, NSA_WIDTH, LANES, GMLP_WIDTH, GMLP_WIDTH,
              2 * KV_WIDTH, 2 * KV_WIDTH, 2 * KV_WIDTH)
    return pl.pallas_call(
        _inproj_sample_kernel,
        grid=(n // tm,),
        in_specs=[row(d), const(wt), const(cos), const(sin), const(ln_g), const(ln_b)],
        out_specs=[row(wd) for wd in widths],
        out_shape=[jax.ShapeDtypeStruct((n, wd), F32) for wd in widths],
        compiler_params=pltpu.CompilerParams(dimension_semantics=("parallel",),
                                             vmem_limit_bytes=VMEM_LIMIT),
        name="inproj_sample",
    )(x, wt, cos, sin, ln_g, ln_b)


def _compress_rows(chunk, pe_ref, w1_ref, b1_ref, w2_ref):
    outs = []
    for k in range(2):
        a = jnp.concatenate([chunk(l, k) for l in range(CMP_BLOCK)], axis=1)
        a = (a + pe_ref[k]).astype(BF16)
        h = jax.nn.silu(_dot(a, w1_ref[k]) + b1_ref[k])
        outs.append(_dot(h.astype(BF16), w2_ref[k]))
    return jnp.concatenate(outs, axis=1)


def _compress_prompt_kernel(x_ref, pe_ref, w1_ref, b1_ref, w2_ref, o_ref):
    row_w = 2 * KV_WIDTH
    chunk = lambda l, k: x_ref[:, l * row_w + k * KV_WIDTH:l * row_w + (k + 1) * KV_WIDTH]
    o_ref[...] = _compress_rows(chunk, pe_ref, w1_ref, b1_ref, w2_ref)


def _compress_prompt(x, pe, w1, b1, w2, tm):
    n = x.shape[0]
    const = lambda a: pl.BlockSpec(a.shape, lambda i: (0,) * a.ndim)
    return pl.pallas_call(
        _compress_prompt_kernel,
        grid=(n // tm,),
        in_specs=[pl.BlockSpec((tm, x.shape[1]), lambda i: (i, 0)),
                  const(pe), const(w1), const(b1), const(w2)],
        out_specs=pl.BlockSpec((tm, 2 * KV_WIDTH), lambda i: (i, 0)),
        out_shape=jax.ShapeDtypeStruct((n, 2 * KV_WIDTH), F32),
        compiler_params=pltpu.CompilerParams(dimension_semantics=("parallel",),
                                             vmem_limit_bytes=VMEM_LIMIT),
        name="compress_prompt",
    )(x, pe, w1, b1, w2)


def _compress_sample_kernel(n_seq, n_pages, pt_ref, *refs):
    page_refs = refs[:n_seq * n_pages]
    pe_ref, w1_ref, b1_ref, w2_ref, o_ref, posk_ref, posv_ref = refs[n_seq * n_pages:]
    blocks_per_page = PAGE_SIZE // CMP_BLOCK
    for j, page_ref in enumerate(page_refs):
        for dst, rows in ((posk_ref, page_ref[:KV_WIDTH, :].T), (posv_ref, page_ref[KV_WIDTH:, :].T)):
            for n in range(blocks_per_page):
                r0 = (j * blocks_per_page + n) * CMP_PITCH
                dst[r0:r0 + CMP_BLOCK, :] = rows[n * CMP_BLOCK:(n + 1) * CMP_BLOCK, :]
    m = n_seq * n_pages * blocks_per_page
    chunk = lambda l, k: (posk_ref, posv_ref)[k][pl.ds(l, m, stride=CMP_PITCH), :]
    o_ref[...] = _compress_rows(chunk, pe_ref, w1_ref, b1_ref, w2_ref)


def _compress_sample(pool, layer, page_table, pe, w1, b1, w2, n_seq):
    db, n_pages = page_table.shape
    rows = n_seq * n_pages * (PAGE_SIZE // CMP_BLOCK)
    const = lambda a: pl.BlockSpec(a.shape, lambda i, pt: (0,) * a.ndim)

    def page_spec(s, p):
        return pl.BlockSpec((None, None) + pool.shape[2:],
                            lambda i, pt: (layer, pt[i * n_seq + s, p], 0, 0))

    grid_spec = pltpu.PrefetchScalarGridSpec(
        num_scalar_prefetch=1,
        grid=(db // n_seq,),
        in_specs=[page_spec(s, p) for s in range(n_seq) for p in range(n_pages)]
        + [const(pe), const(w1), const(b1), const(w2)],
        out_specs=pl.BlockSpec((rows, 2 * KV_WIDTH), lambda i, pt: (i, 0)),
        scratch_shapes=[pltpu.VMEM((rows * CMP_PITCH, KV_WIDTH), F32)] * 2,
    )
    return pl.pallas_call(
        functools.partial(_compress_sample_kernel, n_seq, n_pages),
        grid_spec=grid_spec,
        out_shape=jax.ShapeDtypeStruct((db * n_pages * (PAGE_SIZE // CMP_BLOCK), 2 * KV_WIDTH), F32),
        compiler_params=pltpu.CompilerParams(dimension_semantics=("parallel",),
                                             vmem_limit_bytes=VMEM_LIMIT),
        name="compress_sample",
    )(page_table, *([pool] * (n_seq * n_pages)), pe, w1, b1, w2)


def _stack_heads(q, tq):
    lane_lo = lax.broadcasted_iota(jnp.int32, (tq, LANES), 1) < HEAD_DIM
    pieces = []
    for hh in range(NSA_HEADS):
        g = hh // GROUP
        c = hh // 2
        chunk = q[:, c * LANES:(c + 1) * LANES]
        if hh % 2 != g:
            chunk = pltpu.roll(chunk, HEAD_DIM, 1)
        pieces.append(jnp.where(lane_lo if g == 0 else jnp.logical_not(lane_lo), chunk, 0.0))
    return jnp.concatenate(pieces, axis=0).astype(BF16)


def _cmp_branch(qraw, ckv, qpos, tq):
    nb = MAX_SLC_BLOCKS
    kc = jnp.concatenate([ckv[:, 0:KV_WIDTH], ckv[:, 2 * KV_WIDTH:3 * KV_WIDTH]], axis=0).astype(BF16)
    vc = jnp.concatenate([ckv[:, KV_WIDTH:2 * KV_WIDTH], ckv[:, 3 * KV_WIDTH:]], axis=0).astype(BF16)
    s = _dot_nt(qraw, kc).reshape(NSA_HEADS, tq, 2 * nb)
    lane = lax.broadcasted_iota(jnp.int32, (tq, 2 * nb), 1)
    blk = jnp.where(lane < nb, 2 * lane, 2 * (lane - nb) + 1)
    mask = ((blk + 1) * CMP_BLOCK - 1) <= qpos
    p = _softmax_masked(s, mask[None])
    o_cmp = _dot(p.reshape(NSA_HEADS * tq, 2 * nb).astype(BF16), vc)
    scores = []
    sblk = lax.broadcasted_iota(jnp.int32, (tq, nb), 1)
    cur = qpos // SLC_BLOCK
    valid = sblk <= cur
    forced = ((sblk == 0) | (sblk == cur) | (sblk == cur - 1)).astype(F32)
    for g in range(KV_HEADS):
        imp = p[g * GROUP]
        for r in range(1, GROUP):
            imp = imp + p[g * GROUP + r]
        imp = imp[:, :nb] + imp[:, nb:]
        scores.append(jnp.where(valid, imp + FORCE_BONUS * forced, -1.0))
    return o_cmp, scores, valid


def _rank_blocks_wide(score, st_ref, cnt_ref, n_chunks):
    st_ref[...] = score.T
    cnt_ref[...] = jnp.zeros_like(cnt_ref)
    sub = lax.broadcasted_iota(jnp.int32, (SUBLANES, LANES), 0)
    n_vregs = MAX_SLC_BLOCKS // SUBLANES
    for c in range(n_vregs):
        @pl.when(c < n_chunks)
        def _():
            tiles = [st_ref[v * SUBLANES:(v + 1) * SUBLANES, :] for v in range(n_vregs)]
            cnts = [cnt_ref[v * SUBLANES:(v + 1) * SUBLANES, :] for v in range(n_vregs)]
            for ii in range(SUBLANES):
                row = tiles[c][ii:ii + 1, :]
                for v in range(n_vregs):
                    if v < c:
                        inc = jnp.where(row > tiles[v], 1.0, 0.0)
                    elif v > c:
                        inc = jnp.where(row >= tiles[v], 1.0, 0.0)
                    else:
                        inc = jnp.where(sub > ii, jnp.where(row >= tiles[v], 1.0, 0.0),
                                        jnp.where(row > tiles[v], 1.0, 0.0))
                    cnts[v] = cnts[v] + inc
            for v in range(n_vregs):
                cnt_ref[v * SUBLANES:(v + 1) * SUBLANES, :] = cnts[v]
    return cnt_ref[...].T


def _rank_blocks_narrow(score, n_blocks):
    lane = lax.broadcasted_iota(jnp.int32, score.shape, 1)
    cnt = jnp.zeros(score.shape, F32)
    for i in range(n_blocks):
        col = score[:, i:i + 1]
        cnt = cnt + jnp.where(lane > i, jnp.where(col >= score, 1.0, 0.0),
                              jnp.where(col > score, 1.0, 0.0))
    return cnt


def _slc_branch(q_aug, kv_ref, et_ref, n_full, qpos, tq, tk):
    ones_rows = jnp.where(lax.broadcasted_iota(jnp.int32, (HEAD_DIM, tk), 0) == 0, 1.0, 0.0).astype(BF16)

    def tile(kt, carry, diagonal):
        ms, accs = carry
        off = pl.multiple_of(kt * tk, tk)
        ka = jnp.concatenate([et_ref[:, pl.ds(off, tk)], kv_ref[:KV_WIDTH, pl.ds(off, tk)]], axis=0)
        s = _dot(q_aug, ka)
        if diagonal:
            visible = (off + lax.broadcasted_iota(jnp.int32, (tq, tk), 1)) <= qpos
        ps, m_new, alphas = [], [], []
        for hh in range(NSA_HEADS):
            s_h = s[hh * tq:(hh + 1) * tq, :]
            if diagonal:
                s_h = jnp.where(visible, s_h, NEG_INF)
            m_h = jnp.maximum(ms[hh], jnp.max(s_h, axis=-1, keepdims=True))
            ps.append(jnp.exp2(s_h - m_h).astype(BF16))
            alphas.append(jnp.exp2(ms[hh] - m_h))
            m_new.append(m_h)
        acc_new = []
        for g in range(KV_HEADS):
            v0 = KV_WIDTH + g * HEAD_DIM
            va = jnp.concatenate([kv_ref[v0:v0 + HEAD_DIM, pl.ds(off, tk)], ones_rows], axis=0)
            pv = _dot_nt(jnp.concatenate(ps[g * GROUP:(g + 1) * GROUP], axis=0), va)
            alpha = jnp.concatenate(alphas[g * GROUP:(g + 1) * GROUP], axis=0)
            acc_new.append(alpha * accs[g] + pv)
        return tuple(m_new), tuple(acc_new)

    init = (tuple(jnp.full((tq, 1), NEG_INF, F32) for _ in range(NSA_HEADS)),
            tuple(jnp.zeros((GROUP * tq, KV_WIDTH), F32) for _ in range(KV_HEADS)))
    rem = n_full % 4
    carry = lax.cond(rem % 2 == 1, lambda c: tile(0, c, False), lambda c: c, init)
    carry = lax.cond(rem >= 2,
                     lambda c: tile(rem % 2 + 1, tile(rem % 2, c, False), False), lambda c: c, carry)

    def quad(j, c):
        kt = rem + 4 * j
        for u in range(4):
            c = tile(kt + u, c, False)
        return c

    carry = lax.fori_loop(0, n_full // 4, quad, carry)
    _, accs = tile(n_full, carry, True)
    outs = [a * (1.0 / jnp.maximum(a[:, HEAD_DIM:HEAD_DIM + 1], 1e-30)) for a in accs]
    return jnp.concatenate([outs[0], pltpu.roll(outs[1], HEAD_DIM, 1)], axis=0)


def _softmax_pv(s, mask, vt, tq):
    nk = s.shape[1]
    rows = NSA_HEADS * tq
    if rows <= LANES:
        s3 = jnp.where(mask[None], s.reshape(NSA_HEADS, tq, nk), NEG_INF)
        p = jnp.exp2(s3 - jnp.max(s3, axis=-1, keepdims=True)).reshape(rows, nk).astype(BF16)
        if rows < LANES:
            p = jnp.concatenate([p, jnp.zeros((LANES - rows, nk), BF16)], axis=0)
        ones_rows = jnp.where(lax.broadcasted_iota(jnp.int32, (2 * SUBLANES, nk), 0) == 0,
                              1.0, 0.0).astype(BF16)
        ot = _dot_nt(jnp.concatenate([vt, ones_rows], axis=0), p)
        return (ot[:KV_WIDTH] * (1.0 / jnp.maximum(ot[KV_WIDTH:KV_WIDTH + 1], 1e-30))).T[:rows]
    ones_rows = jnp.where(lax.broadcasted_iota(jnp.int32, (HEAD_DIM, nk), 0) == 0, 1.0, 0.0).astype(BF16)
    outs = []
    for g in range(KV_HEADS):
        ps = []
        for hh in range(g * GROUP, (g + 1) * GROUP):
            s_h = jnp.where(mask, s[hh * tq:(hh + 1) * tq, :], NEG_INF)
            ps.append(jnp.exp2(s_h - jnp.max(s_h, axis=-1, keepdims=True)).astype(BF16))
        va = jnp.concatenate([vt[g * HEAD_DIM:(g + 1) * HEAD_DIM, :], ones_rows], axis=0)
        pv = _dot_nt(jnp.concatenate(ps, axis=0), va)
        outs.append(pv * (1.0 / jnp.maximum(pv[:, HEAD_DIM:HEAD_DIM + 1], 1e-30)))
    return jnp.concatenate([outs[0], pltpu.roll(outs[1], HEAD_DIM, 1)], axis=0)


def _slc_single(bias_rows, qrot, kt, vt, et, qpos, tq):
    nk = kt.shape[1]
    s = _dot(bias_rows, et) + _dot(qrot, kt)
    visible = lax.broadcasted_iota(jnp.int32, (tq, nk), 1) <= qpos
    return _softmax_pv(s, visible, vt, tq)


def _win_branch(qrot, kt, vt, kpos, qpos, tq):
    diff = qpos - kpos
    return _softmax_pv(_dot(qrot, kt), (diff >= 0) & (diff < WINDOW), vt, tq)


def _combine(o_cmp, o_slc, o_win, gates, tq):
    lane_lo = lax.broadcasted_iota(jnp.int32, (tq, LANES), 1) < HEAD_DIM
    heads = []
    for hh in range(NSA_HEADS):
        tot = None
        for br, o in enumerate((o_cmp, o_slc, o_win)):
            col = br * NSA_HEADS + hh
            term = gates[:, col:col + 1] * o[hh * tq:(hh + 1) * tq, :]
            tot = term if tot is None else tot + term
        heads.append(tot)
    chunks = []
    for c in range(NSA_HEADS // 2):
        a, b = heads[2 * c], heads[2 * c + 1]
        if c // 2 == 0:
            chunks.append(jnp.where(lane_lo, a, pltpu.roll(b, HEAD_DIM, 1)))
        else:
            chunks.append(jnp.where(lane_lo, pltpu.roll(a, HEAD_DIM, 1), b))
    return jnp.concatenate(chunks, axis=1)


def _selection_bias(rank, valid):
    return jnp.where((rank < float(N_SELECT)) & valid, 0.0, NEG_INF).astype(BF16)


def _pad_ckv(ckv):
    n = ckv.shape[0]
    if n == MAX_SLC_BLOCKS:
        return ckv
    return jnp.concatenate([ckv, jnp.zeros((MAX_SLC_BLOCKS - n, ckv.shape[1]), F32)], axis=0)


def _nsa_prompt_kernel(qraw_ref, qrot_ref, gate_ref, ckv_ref, slc_ref, win_ref, et_ref,
                       o_ref, st_ref, cnt_ref):
    tq = QUERY_BLOCK
    i = pl.program_id(1)
    qpos0 = i * tq
    qpos = qpos0 + lax.broadcasted_iota(jnp.int32, (tq, 1), 0)
    qraw = _stack_heads(qraw_ref[...], tq)
    qrot = _stack_heads(qrot_ref[...], tq)

    o_cmp, scores, valid = _cmp_branch(qraw, _pad_ckv(ckv_ref[...]), qpos, tq)
    n_chunks = (qpos0 + tq - 1) // (SLC_BLOCK * SUBLANES) + 1
    bias = []
    for g in range(KV_HEADS):
        rank = _rank_blocks_wide(scores[g], st_ref, cnt_ref, n_chunks)
        bias.append(_selection_bias(rank, valid))
    bias_rows = jnp.concatenate([bias[0]] * GROUP + [bias[1]] * GROUP, axis=0)
    q_aug = jnp.concatenate([bias_rows, qrot], axis=1)
    o_slc = _slc_branch(q_aug, slc_ref, et_ref, qpos0 // SLC_KEY_TILE, qpos, tq, SLC_KEY_TILE)

    nk = WINDOW + tq
    start = pl.multiple_of(jnp.maximum(qpos0 - WINDOW, 0), tq)
    kpos = start + lax.broadcasted_iota(jnp.int32, (1, nk), 1)
    o_win = _win_branch(qrot, win_ref[:KV_WIDTH, pl.ds(start, nk)], win_ref[KV_WIDTH:, pl.ds(start, nk)],
                        kpos, qpos, tq)

    o_ref[...] = _combine(o_cmp, o_slc, o_win, gate_ref[...], tq)


def _nsa_prompt(qraw, qrot, gates, ckv, slcb, winb, et):
    b, t, _ = qraw.shape
    tq = QUERY_BLOCK
    blk = lambda width: pl.BlockSpec((None, tq, width), lambda bi, i: (bi, i, 0))
    per_batch = lambda a: pl.BlockSpec((None,) + a.shape[1:], lambda bi, i: (bi, 0, 0))
    return pl.pallas_call(
        _nsa_prompt_kernel,
        grid=(b, t // tq),
        in_specs=[blk(NSA_WIDTH), blk(NSA_WIDTH), blk(LANES), per_batch(ckv), per_batch(slcb),
                  per_batch(winb), pl.BlockSpec(et.shape, lambda bi, i: (0, 0))],
        out_specs=blk(NSA_WIDTH),
        out_shape=jax.ShapeDtypeStruct((b, t, NSA_WIDTH), F32),
        scratch_shapes=[pltpu.VMEM((MAX_SLC_BLOCKS, tq), F32), pltpu.VMEM((MAX_SLC_BLOCKS, tq), F32)],
        compiler_params=pltpu.CompilerParams(dimension_semantics=("parallel", "arbitrary"),
                                             vmem_limit_bytes=VMEM_LIMIT),
        name="nsa_prompt",
    )(qraw, qrot, gates, ckv, slcb, winb, et)


def _nsa_sample_kernel(n_seq, n_pages, past, s_new, pt_ref, *refs):
    n_in = n_seq * n_pages
    page_refs = refs[:n_in]
    (qraw_ref, qrot_ref, gate_ref, ckv_ref, slcn_ref, wins_ref, winn_ref, et_ref) = refs[n_in:n_in + 8]
    o_ref, wout_ref = refs[-2:]
    tq = SAMPLE_ROWS
    qpos = past + lax.broadcasted_iota(jnp.int32, (tq, 1), 0)
    n_buf = wins_ref.shape[2]
    n_blocks = (past + tq - 1) // SLC_BLOCK + 1
    pad_rows = jnp.zeros((LANES - tq, 2 * KV_WIDTH), F32)
    wpos = past - n_buf + lax.broadcasted_iota(jnp.int32, (1, n_buf + LANES), 1)
    lane = lax.broadcasted_iota(jnp.int32, (2 * KV_WIDTH, LANES), 1)
    for s in range(n_seq):
        slc_new = jnp.concatenate([slcn_ref[s], pad_rows], axis=0).T
        tiles = [page_refs[s * n_pages + p][...].astype(BF16) for p in range(n_pages)]
        tiles.append(slc_new.astype(BF16))
        kt = jnp.concatenate([tl[:KV_WIDTH] for tl in tiles], axis=1)
        vt = jnp.concatenate([tl[KV_WIDTH:] for tl in tiles], axis=1)

        qraw = _stack_heads(qraw_ref[s], tq)
        qrot = _stack_heads(qrot_ref[s], tq)
        o_cmp, scores, valid = _cmp_branch(qraw, _pad_ckv(ckv_ref[s]), qpos, tq)
        bias = [_selection_bias(_rank_blocks_narrow(scores[g], n_blocks), valid)
                for g in range(KV_HEADS)]
        bias_rows = jnp.concatenate([bias[0]] * GROUP + [bias[1]] * GROUP, axis=0)
        o_slc = _slc_single(bias_rows, qrot, kt, vt, et_ref[...], qpos, tq)

        state = wins_ref[s]
        win_new = jnp.concatenate([winn_ref[s], pad_rows], axis=0).T
        kt = jnp.concatenate([state[:KV_WIDTH], win_new[:KV_WIDTH]], axis=1).astype(BF16)
        vt = jnp.concatenate([state[KV_WIDTH:], win_new[KV_WIDTH:]], axis=1).astype(BF16)
        o_win = _win_branch(qrot, kt, vt, wpos, qpos, tq)
        o_ref[s] = _combine(o_cmp, o_slc, o_win, gate_ref[s], tq)

        shifted = pltpu.roll(state, n_buf - s_new, 1)
        tail = jnp.where(lane >= LANES - s_new, pltpu.roll(win_new, LANES - s_new, 1),
                         shifted[:, n_buf - LANES:])
        wout_ref[s, :, :n_buf - LANES] = shifted[:, :n_buf - LANES]
        wout_ref[s, :, n_buf - LANES:] = tail


def _nsa_sample(pool, win_state, layer, page_table, qraw, qrot, gates, ckv, slc_new, win_new, et,
                n_seq, s_new, carried):
    db, n_pages = page_table.shape
    past = n_pages * PAGE_SIZE
    tq = SAMPLE_ROWS
    n_buf = win_state.shape[3]
    seq = lambda a: pl.BlockSpec((n_seq,) + a.shape[1:], lambda i, pt: (i, 0, 0))

    def page_spec(s, p):
        return pl.BlockSpec((None, None) + pool.shape[2:],
                            lambda i, pt: (layer, pt[i * n_seq + s, p], 0, 0))

    state_spec = pl.BlockSpec((None, n_seq) + win_state.shape[2:], lambda i, pt: (layer, i, 0, 0))
    grid_spec = pltpu.PrefetchScalarGridSpec(
        num_scalar_prefetch=1,
        grid=(db // n_seq,),
        in_specs=[page_spec(s, p) for s in range(n_seq) for p in range(n_pages)]
        + [seq(qraw), seq(qrot), seq(gates), seq(ckv), seq(slc_new), state_spec, seq(win_new),
           pl.BlockSpec(et.shape, lambda i, pt: (0, 0))]
        + [pl.BlockSpec(memory_space=pl.ANY)] * len(carried),
        out_specs=[pl.BlockSpec((n_seq, tq, NSA_WIDTH), lambda i, pt: (i, 0, 0)),
                   pl.BlockSpec((None, n_seq) + win_state.shape[2:], lambda i, pt: (layer, i, 0, 0))],
    )
    n_operands = 1 + n_seq * n_pages + 8
    return pl.pallas_call(
        functools.partial(_nsa_sample_kernel, n_seq, n_pages, past, s_new),
        grid_spec=grid_spec,
        out_shape=[jax.ShapeDtypeStruct((db, tq, NSA_WIDTH), F32),
                   jax.ShapeDtypeStruct(win_state.shape, F32)],
        input_output_aliases={n_operands + k: 1 + k for k in range(len(carried))},
        compiler_params=pltpu.CompilerParams(dimension_semantics=("parallel",),
                                             vmem_limit_bytes=VMEM_LIMIT),
        name="nsa_sample",
    )(page_table, *([pool] * (n_seq * n_pages)), qraw, qrot, gates, ckv, slc_new, win_state, win_new, et,
      *carried)


def _post_kernel(chunk, alpha, n_ffn_chunks, x_ref, o_ref, u_ref, v_ref, gw_ref, gb_ref,
                 wout_ref, ln1g_ref, ln1b_ref, wg_ref, wu_ref, wd_ref, ln2g_ref, ln2b_ref, y_ref):
    tm = x_ref.shape[0]
    c = GMLP_CHUNK
    row = lax.broadcasted_iota(jnp.int32, (c, c), 0)
    col = lax.broadcasted_iota(jnp.int32, (c, c), 1)
    causal = (col <= row) & ((row // chunk) == (col // chunk))
    w_cat = jnp.concatenate([jnp.where(causal, gw_ref[h], 0.0) for h in range(GMLP_GROUPS)],
                            axis=1).astype(BF16)
    lane_group = lax.broadcasted_iota(jnp.int32, (c, GMLP_WIDTH), 1) // (GMLP_WIDTH // GMLP_GROUPS)
    mixed = []
    for sub in range(tm // c):
        v = v_ref[sub * c:(sub + 1) * c, :]
        v_diag = jnp.concatenate([jnp.where(lane_group == h, v, 0.0) for h in range(GMLP_GROUPS)],
                                 axis=0).astype(BF16)
        s = _dot(w_cat, v_diag) + gb_ref[...]
        mixed.append(u_ref[sub * c:(sub + 1) * c, :] * s)
    o_gmlp = jnp.concatenate(mixed, axis=0)
    mix = jnp.concatenate([o_ref[...], o_gmlp], axis=1).astype(BF16)
    h = _dot(mix, wout_ref[...])
    x1 = _layer_norm(alpha * x_ref[...] + h, ln1g_ref[...], ln1b_ref[...])
    x1b = x1.astype(BF16)
    f = None
    fc = wg_ref.shape[1] // n_ffn_chunks
    for j in range(n_ffn_chunks):
        gate = _dot(x1b, wg_ref[:, j * fc:(j + 1) * fc])
        up = _dot(x1b, wu_ref[:, j * fc:(j + 1) * fc])
        a = (jax.nn.silu(gate) * up).astype(BF16)
        part = _dot(a, wd_ref[j * fc:(j + 1) * fc, :])
        f = part if f is None else f + part
    y_ref[...] = _layer_norm(alpha * x1 + f, ln2g_ref[...], ln2b_ref[...])


def _post(x, o_nsa, u, v, gw, gb, wout, ln1g, ln1b, wg, wu, wd, ln2g, ln2b, chunk, alpha, tm):
    n, d = x.shape
    ffn = wg.shape[1]
    n_ffn_chunks = next(k for k in (4, 2, 1) if ffn % (k * LANES) == 0)
    row = lambda width: pl.BlockSpec((tm, width), lambda i: (i, 0))
    const = lambda a: pl.BlockSpec(a.shape, lambda i: (0,) * a.ndim, pipeline_mode=pl.Buffered(1))
    consts = (gw, gb, wout, ln1g, ln1b, wg, wu, wd, ln2g, ln2b)
    return pl.pallas_call(
        functools.partial(_post_kernel, chunk, alpha, n_ffn_chunks),
        grid=(n // tm,),
        in_specs=[row(d), row(NSA_WIDTH), row(GMLP_WIDTH), row(GMLP_WIDTH)] + [const(a) for a in consts],
        out_specs=row(d),
        out_shape=jax.ShapeDtypeStruct((n, d), F32),
        compiler_params=pltpu.CompilerParams(dimension_semantics=("parallel",),
                                             vmem_limit_bytes=VMEM_LIMIT),
        name="post",
    )(x, o_nsa, u, v, *consts)


def _pack_w_in_t(w_in):
    wt = w_in.T
    o_gate = NSA_WIDTH + 3 * 2 * KV_WIDTH
    o_gmlp = o_gate + 3 * NSA_HEADS
    gate = jnp.pad(wt[o_gate:o_gmlp], ((0, LANES - 3 * NSA_HEADS), (0, 0)))
    return jnp.concatenate([wt[:o_gate], gate, wt[o_gmlp:]], axis=0).astype(BF16)


def _pack_compress(pe, w1, b1, w2):
    eye = jnp.eye(KV_HEADS, dtype=F32)
    zero = jnp.zeros_like(w1)
    w1p = jnp.stack([jnp.concatenate([w1, zero], axis=-1), jnp.concatenate([zero, w1], axis=-1)], axis=2)
    w1p = w1p.reshape(2, CMP_BLOCK * KV_WIDTH, KV_HEADS * CMP_HIDDEN)
    pep = jnp.broadcast_to(pe[:, :, None, :], (2, CMP_BLOCK, KV_HEADS, HEAD_DIM)).reshape(2, 1, -1)
    b1p = jnp.tile(b1, (1, KV_HEADS))[:, None, :]
    w2p = jnp.einsum('khd,gG->kghGd', w2, eye).reshape(2, KV_HEADS * CMP_HIDDEN, KV_WIDTH)
    return pep, w1p.astype(BF16), b1p, w2p.astype(BF16)


def _pack_gmlp(ws, bs, chunk):
    reps = GMLP_CHUNK // chunk
    gw = jnp.tile(ws[:, :chunk, :chunk], (1, reps, reps))
    gb = jnp.tile(jnp.repeat(bs[:, :chunk].T, GMLP_WIDTH // GMLP_GROUPS, axis=1), (reps, 1))
    return gw, gb


def _rope_tables(pos):
    half = HEAD_DIM // 2
    inv_freq = ROPE_THETA ** (-jnp.arange(half, dtype=F32) / half)
    ang = pos.astype(F32)[:, None] * inv_freq[None, :]
    cos, sin = jnp.cos(ang), jnp.sin(ang)
    return jnp.concatenate([cos, cos], axis=1), jnp.concatenate([-sin, sin], axis=1)


def _block_indicator(n_keys):
    key_block = np.arange(n_keys)[None, :] // SLC_BLOCK
    return jnp.asarray(key_block == np.arange(MAX_SLC_BLOCKS)[:, None], dtype=BF16)


def _feature_major(a):
    lead = a.shape[:-4]
    n = len(lead)
    perm = tuple(range(n)) + (n + 1, n + 2, n + 3, n)
    return jnp.transpose(a, perm).reshape(lead + (2 * KV_WIDTH, a.shape[-4]))


def _position_major(a):
    lead = a.shape[:-2]
    n = len(lead)
    a = a.reshape(lead + (2, KV_HEADS, HEAD_DIM, a.shape[-1]))
    return jnp.transpose(a, tuple(range(n)) + (n + 3, n, n + 1, n + 2))


def kernel(x_prompt, x_sample, cache_cmp_kv, cache_slc_kv, state_win_kv, page_table,
           w_in, cmp_pe, cmp_w1, cmp_b1, cmp_w2, gmlp_ln_g, gmlp_ln_b, gmlp_ws, gmlp_bs,
           w_out, ln1_g, ln1_b, w_gate, w_up, w_down, ln2_g, ln2_b):
    depth = w_in.shape[0]
    b, t, d = x_prompt.shape
    db, s_new, _ = x_sample.shape
    n_pages = page_table.shape[1]
    past = n_pages * PAGE_SIZE
    n_buf = state_win_kv.shape[2]
    kv_tail = (2, KV_HEADS, HEAD_DIM)
    rows_s = SAMPLE_ROWS
    assert t % SLC_KEY_TILE == 0 and t >= WINDOW + QUERY_BLOCK and t // SLC_BLOCK <= MAX_SLC_BLOCKS
    assert s_new <= rows_s and (past + rows_s - 1) // SLC_BLOCK < MAX_SLC_BLOCKS
    assert past % CMP_BLOCK == 0 and past % CMP_BLOCK + s_new < CMP_BLOCK
    assert n_buf % LANES == 0 and n_buf >= LANES
    alpha = (2 * depth) ** 0.25

    tm_p = 512
    tm_s = min(512, db * rows_s)
    seq_per_step = 2 if db % 2 == 0 else 1
    attn_seq_per_step = 4 if db % 4 == 0 else seq_per_step
    cos_p, sin_p = _rope_tables(jnp.arange(t, dtype=jnp.int32))
    cos_s, sin_s = _rope_tables(past + jnp.arange(rows_s, dtype=jnp.int32))
    lane_reps = LANES // HEAD_DIM
    cos_pl, sin_pl = jnp.tile(cos_p, (1, lane_reps)), jnp.tile(sin_p, (1, lane_reps))
    cos_pt, sin_pt = cos_p.T, sin_p.T
    cos_s = jnp.tile(cos_s, (tm_s // rows_s, lane_reps))
    sin_s = jnp.tile(sin_s, (tm_s // rows_s, lane_reps))
    et_p = _block_indicator(t)
    et_s = _block_indicator(past + LANES)
    pool_cmp = _feature_major(cache_cmp_kv)
    pool_slc = _feature_major(cache_slc_kv)
    win_state = _feature_major(state_win_kv)

    yp = x_prompt.reshape(b * t, d)
    ys = jnp.pad(x_sample, ((0, 0), (0, rows_s - s_new), (0, 0))).reshape(db * rows_s, d)
    outs = {k: [] for k in ("s_cmp", "s_slc", "s_v")}
    assert min(WINDOW, t) == tm_p
    new_kv_prompt = ()
    win_next = ()
    cmp_tile = min(128, b * t // CMP_BLOCK)

    for l in range(depth):
        wt = _pack_w_in_t(w_in[l])
        pe_p, w1_p, b1_p, w2_p = _pack_compress(cmp_pe[l], cmp_w1[l], cmp_b1[l], cmp_w2[l])
        ln_g = gmlp_ln_g[l][None, :]
        ln_b = gmlp_ln_b[l][None, :]
        tail = (w_out[l].astype(BF16), ln1_g[l][None, :], ln1_b[l][None, :], w_gate[l].astype(BF16),
                w_up[l].astype(BF16), w_down[l].astype(BF16), ln2_g[l][None, :], ln2_b[l][None, :])

        qraw, qrot, gates, u, v, cmp_kv, *new_kv_prompt, slcb, winb = _inproj_prompt(
            yp, wt, cos_pl, sin_pl, cos_pt, sin_pt, ln_g, ln_b, tm_p, b, t, depth, l, new_kv_prompt)
        ckv = _compress_prompt(cmp_kv.reshape(b * t // CMP_BLOCK, CMP_BLOCK * 2 * KV_WIDTH),
                               pe_p, w1_p, b1_p, w2_p, cmp_tile)
        r3 = lambda a: a.reshape(b, t, a.shape[-1])
        o_nsa = _nsa_prompt(r3(qraw), r3(qrot), r3(gates), ckv.reshape(b, t // SLC_BLOCK, 4 * KV_WIDTH),
                            slcb, winb, et_p)
        gw, gb = _pack_gmlp(gmlp_ws[l], gmlp_bs[l], GMLP_CHUNK)
        yp = _post(yp, o_nsa.reshape(b * t, NSA_WIDTH), u, v, gw, gb, *tail, GMLP_CHUNK, alpha, tm_p)

        qraw, qrot, gates, u, v, cmp_kv, slc_kv, win_kv = _inproj_sample(
            ys, wt, cos_s, sin_s, ln_g, ln_b, tm_s)
        ckv = _compress_sample(pool_cmp, l, page_table, pe_p, w1_p, b1_p, w2_p, attn_seq_per_step)
        s3 = lambda a: a.reshape(db, rows_s, a.shape[-1])
        o_nsa, *win_next = _nsa_sample(pool_slc, win_state, l, page_table, s3(qraw), s3(qrot), s3(gates),
                                       ckv.reshape(db, past // SLC_BLOCK, 4 * KV_WIDTH),
                                       s3(slc_kv), s3(win_kv), et_s, attn_seq_per_step, s_new, win_next)
        gw, gb = _pack_gmlp(gmlp_ws[l], gmlp_bs[l], rows_s)
        ys = _post(ys, o_nsa.reshape(db * rows_s, NSA_WIDTH), u, v, gw, gb, *tail, rows_s, alpha, tm_s)
        new = lambda a: a.reshape((db, rows_s) + a.shape[1:])[:, :s_new]
        outs["s_cmp"].append(new(cmp_kv).reshape((db, s_new) + kv_tail))
        outs["s_slc"].append(new(slc_kv).reshape((db, s_new) + kv_tail))
        outs["s_v"].append(new(v))

    y_sample = ys.reshape(db, rows_s, d)[:, :s_new]
    p_cmp, p_slc, p_win = (_position_major(a) for a in new_kv_prompt)
    return (yp.reshape(b, t, d), y_sample, p_cmp, p_slc, p_win,
            jnp.stack(outs["s_cmp"]), jnp.stack(outs["s_slc"]),
            _position_major(win_next[0]), jnp.stack(outs["s_v"]))
```

```python
import functools
import math

import numpy as np
import jax
import jax.numpy as jnp
from jax import lax
from jax.experimental import pallas as pl
from jax.experimental.pallas import tpu as pltpu

F32 = jnp.float32
BF16 = jnp.bfloat16

LANES = 128
SUBLANES = 8
HEAD_DIM = 64
NSA_HEADS = 8
KV_HEADS = 2
GROUP = NSA_HEADS // KV_HEADS
NSA_WIDTH = NSA_HEADS * HEAD_DIM
KV_WIDTH = KV_HEADS * HEAD_DIM
GMLP_WIDTH = 512
GMLP_GROUPS = 8
GMLP_CHUNK = 128
CMP_BLOCK = 32
CMP_HIDDEN = 256
CMP_PITCH = CMP_BLOCK + SUBLANES
SLC_BLOCK = 64
N_SELECT = 16
WINDOW = 512
QUERY_BLOCK = 128
PAGE_SIZE = 128
FORCE_BONUS = 1.0e4
ROPE_THETA = 10000.0
LN_EPS = 1e-5
NEG_INF = -1e30
MAX_SLC_BLOCKS = LANES
SAMPLE_ROWS = SUBLANES
SLC_KEY_TILE = 512
VMEM_LIMIT = 56 * 1024 * 1024
Q_SCALE = HEAD_DIM ** -0.5 * math.log2(math.e)

C_Q = 0
C_CMP = C_Q + NSA_WIDTH
C_SLC = C_CMP + 2 * KV_WIDTH
C_WIN = C_SLC + 2 * KV_WIDTH
C_GATE = C_WIN + 2 * KV_WIDTH
C_U = C_GATE + LANES
C_V = C_U + GMLP_WIDTH
C_END = C_V + GMLP_WIDTH


def _dot(a, b):
    return jnp.dot(a, b, preferred_element_type=F32)


def _dot_nt(a, b):
    return lax.dot_general(a, b, (((1,), (1,)), ((), ())), preferred_element_type=F32)


def _layer_norm(x, g, b):
    mu = jnp.mean(x, axis=-1, keepdims=True)
    xc = x - mu
    var = jnp.mean(xc * xc, axis=-1, keepdims=True)
    return xc * lax.rsqrt(var + LN_EPS) * g + b


def _softmax_masked(s, mask):
    s = jnp.where(mask, s, NEG_INF)
    m = jnp.max(s, axis=-1, keepdims=True)
    e = jnp.where(mask, jnp.exp2(s - m), 0.0)
    return e * (1.0 / jnp.maximum(jnp.sum(e, axis=-1, keepdims=True), 1e-30))


def _rope_lanes(c, cos, sin, first_half):
    partner = jnp.where(first_half, pltpu.roll(c, LANES - HEAD_DIM // 2, 1),
                        pltpu.roll(c, HEAD_DIM // 2, 1))
    return c * cos + partner * sin


def _inproj_common(xb, wt_ref, cos, sin, first_half, lng_ref, lnb_ref,
                   qraw_ref, qrot_ref, gate_ref, u_ref, v_ref):
    zq = _dot_nt(xb, wt_ref[C_Q:C_CMP, :])
    qraw_ref[...] = (zq * Q_SCALE).astype(qraw_ref.dtype)
    for c in range(NSA_WIDTH // LANES):
        qrot_ref[:, c * LANES:(c + 1) * LANES] = (_rope_lanes(
            zq[:, c * LANES:(c + 1) * LANES], cos, sin, first_half) * Q_SCALE).astype(qrot_ref.dtype)
    gate_ref[...] = jax.nn.sigmoid(_dot_nt(xb, wt_ref[C_GATE:C_U, :]))
    u_ref[...] = jax.nn.gelu(_dot_nt(xb, wt_ref[C_U:C_V, :])).astype(u_ref.dtype)
    zv = jax.nn.gelu(_dot_nt(xb, wt_ref[C_V:C_END, :]))
    v_ref[...] = _layer_norm(zv, lng_ref[...], lnb_ref[...]).astype(v_ref.dtype)


def _inproj_prompt_kernel(n_carried, x_ref, wt_ref, cos_ref, sin_ref, cost_ref, sint_ref, lng_ref,
                          lnb_ref, *refs):
    (qraw_ref, qrot_ref, gate_ref, u_ref, v_ref,
     cmp_ref, cmpt_ref, slct_ref, wint_ref, slcb_ref, winb_ref) = refs[n_carried:]
    xb = x_ref[...].astype(BF16)
    cos = cos_ref[...]
    lane = lax.broadcasted_iota(jnp.int32, cos.shape, 1)
    first_half = (lane % HEAD_DIM) < (HEAD_DIM // 2)
    _inproj_common(xb, wt_ref, cos, sin_ref[...], first_half, lng_ref, lnb_ref,
                   qraw_ref, qrot_ref, gate_ref, u_ref, v_ref)
    cmp_ref[...] = _dot_nt(xb, wt_ref[C_CMP:C_SLC, :])
    cmpt_ref[...] = _dot_nt(wt_ref[C_CMP:C_SLC, :], xb)
    cost = cost_ref[...]
    sint = sint_ref[...]
    half = HEAD_DIM // 2
    for row0, f_ref, b_ref in ((C_SLC, slct_ref, slcb_ref), (C_WIN, wint_ref, winb_ref)):
        zt = _dot_nt(wt_ref[row0:row0 + 2 * KV_WIDTH, :], xb)
        for g in range(KV_HEADS):
            kg = zt[g * HEAD_DIM:(g + 1) * HEAD_DIM, :]
            partner = jnp.concatenate([kg[half:, :], kg[:half, :]], axis=0)
            kr = kg * cost + partner * sint
            f_ref[g * HEAD_DIM:(g + 1) * HEAD_DIM, :] = kr
            b_ref[g * HEAD_DIM:(g + 1) * HEAD_DIM, :] = kr.astype(BF16)
        f_ref[KV_WIDTH:, :] = zt[KV_WIDTH:, :]
        b_ref[KV_WIDTH:, :] = zt[KV_WIDTH:, :].astype(BF16)


def _inproj_sample_kernel(x_ref, wt_ref, cos_ref, sin_ref, lng_ref, lnb_ref,
                          qraw_ref, qrot_ref, gate_ref, u_ref, v_ref, cmp_ref, slc_ref, win_ref):
    xb = x_ref[...].astype(BF16)
    cos = cos_ref[...]
    sin = sin_ref[...]
    lane = lax.broadcasted_iota(jnp.int32, cos.shape, 1)
    first_half = (lane % HEAD_DIM) < (HEAD_DIM // 2)
    _inproj_common(xb, wt_ref, cos, sin, first_half, lng_ref, lnb_ref,
                   qraw_ref, qrot_ref, gate_ref, u_ref, v_ref)
    cmp_ref[...] = _dot_nt(xb, wt_ref[C_CMP:C_SLC, :])
    for row0, f_ref in ((C_SLC, slc_ref), (C_WIN, win_ref)):
        z = _dot_nt(xb, wt_ref[row0:row0 + 2 * KV_WIDTH, :])
        f_ref[:, :KV_WIDTH] = _rope_lanes(z[:, :KV_WIDTH], cos, sin, first_half)
        f_ref[:, KV_WIDTH:] = z[:, KV_WIDTH:]


def _inproj_prompt(x, wt, cos, sin, cost, sint, ln_g, ln_b, tm, b, t, depth, layer, carried):
    n, d = x.shape
    n_tab = t // tm
    row = lambda width: pl.BlockSpec((tm, width), lambda i: (i, 0))
    const = lambda a: pl.BlockSpec(a.shape, lambda i: (0,) * a.ndim)
    tab = pl.BlockSpec((tm, LANES), lambda i: (i % n_tab, 0))
    tabt = pl.BlockSpec((HEAD_DIM, tm), lambda i: (0, i % n_tab))
    fm = pl.BlockSpec((None, 2 * KV_WIDTH, tm), lambda i: (i // n_tab, 0, i % n_tab))
    fm_layer = pl.BlockSpec((None, None, 2 * KV_WIDTH, tm), lambda i: (layer, i // n_tab, 0, i % n_tab))
    keep_layer = pl.BlockSpec((None, None, 2 * KV_WIDTH, tm), lambda i: (layer, i // n_tab, 0, 0))
    row_out = ((NSA_WIDTH, BF16), (NSA_WIDTH, BF16), (LANES, F32), (GMLP_WIDTH, BF16), (GMLP_WIDTH, BF16),
               (2 * KV_WIDTH, F32))
    n_in = 8
    n_row = len(row_out)
    stacked = lambda width: jax.ShapeDtypeStruct((depth, b, 2 * KV_WIDTH, width), F32)
    return pl.pallas_call(
        functools.partial(_inproj_prompt_kernel, len(carried)),
        grid=(n // tm,),
        in_specs=[row(d), const(wt), tab, tab, tabt, tabt, const(ln_g), const(ln_b)]
        + [pl.BlockSpec(memory_space=pl.ANY)] * len(carried),
        out_specs=[row(wd) for wd, _ in row_out] + [fm_layer, fm_layer, keep_layer, fm, fm],
        out_shape=[jax.ShapeDtypeStruct((n, wd), dt) for wd, dt in row_out]
        + [stacked(t), stacked(t), stacked(tm)]
        + [jax.ShapeDtypeStruct((b, 2 * KV_WIDTH, t), BF16)] * 2,
        input_output_aliases={n_in + k: n_row + k for k in range(len(carried))},
        compiler_params=pltpu.CompilerParams(dimension_semantics=("arbitrary",),
                                             vmem_limit_bytes=VMEM_LIMIT),
        name="inproj_prompt",
    )(x, wt, cos, sin, cost, sint, ln_g, ln_b, *carried)


def _inproj_sample(x, wt, cos, sin, ln_g, ln_b, tm):
    n, d = x.shape
    row = lambda width: pl.BlockSpec((tm, width), lambda i: (i, 0))
    const = lambda a: pl.BlockSpec(a.shape, lambda i: (0,) * a.ndim)
    widths = (NSA_WIDTH, NSA_WIDTH, LANES, GMLP_WIDTH, GMLP_WIDTH,
              2 * KV_WIDTH, 2 * KV_WIDTH, 2 * KV_WIDTH)
    return pl.pallas_call(
        _inproj_sample_kernel,
        grid=(n // tm,),
        in_specs=[row(d), const(wt), const(cos), const(sin), const(ln_g), const(ln_b)],
        out_specs=[row(wd) for wd in widths],
        out_shape=[jax.ShapeDtypeStruct((n, wd), F32) for wd in widths],
        compiler_params=pltpu.CompilerParams(dimension_semantics=("parallel",),
                                             vmem_limit_bytes=VMEM_LIMIT),
        name="inproj_sample",
    )(x, wt, cos, sin, ln_g, ln_b)


def _compress_rows(chunk, pe_ref, w1_ref, b1_ref, w2_ref, prepare=None):
    outs = []
    for k in range(2):
        if prepare is not None:
            prepare(k)
        a = jnp.concatenate([chunk(l, k) for l in range(CMP_BLOCK)], axis=1)
        a = (a + pe_ref[k]).astype(BF16)
        h = jax.nn.silu(_dot(a, w1_ref[k]) + b1_ref[k])
        outs.append(_dot(h.astype(BF16), w2_ref[k]))
    return jnp.concatenate(outs, axis=1)


def _compress_prompt_kernel(x_ref, pe_ref, w1_ref, b1_ref, w2_ref, o_ref):
    row_w = 2 * KV_WIDTH
    chunk = lambda l, k: x_ref[:, l * row_w + k * KV_WIDTH:l * row_w + (k + 1) * KV_WIDTH]
    o_ref[...] = _compress_rows(chunk, pe_ref, w1_ref, b1_ref, w2_ref)


def _compress_prompt(x, pe, w1, b1, w2, tm):
    n = x.shape[0]
    const = lambda a: pl.BlockSpec(a.shape, lambda i: (0,) * a.ndim)
    return pl.pallas_call(
        _compress_prompt_kernel,
        grid=(n // tm,),
        in_specs=[pl.BlockSpec((tm, x.shape[1]), lambda i: (i, 0)),
                  const(pe), const(w1), const(b1), const(w2)],
        out_specs=pl.BlockSpec((tm, 2 * KV_WIDTH), lambda i: (i, 0)),
        out_shape=jax.ShapeDtypeStruct((n, 2 * KV_WIDTH), F32),
        compiler_params=pltpu.CompilerParams(dimension_semantics=("parallel",),
                                             vmem_limit_bytes=VMEM_LIMIT),
        name="compress_prompt",
    )(x, pe, w1, b1, w2)


def _compress_sample_kernel(n_seq, n_pages, pt_ref, *refs):
    page_refs = refs[:n_seq * n_pages]
    pe_ref, w1_ref, b1_ref, w2_ref, o_ref, posk_ref, posv_ref = refs[n_seq * n_pages:]
    blocks_per_page = PAGE_SIZE // CMP_BLOCK
    stage = (posk_ref, posv_ref)

    def prepare(k):
        for j, page_ref in enumerate(page_refs):
            rows = page_ref[k * KV_WIDTH:(k + 1) * KV_WIDTH, :].T
            for n in range(blocks_per_page):
                r0 = (j * blocks_per_page + n) * CMP_PITCH
                stage[k][r0:r0 + CMP_BLOCK, :] = rows[n * CMP_BLOCK:(n + 1) * CMP_BLOCK, :]

    m = n_seq * n_pages * blocks_per_page
    chunk = lambda l, k: stage[k][pl.ds(l, m, stride=CMP_PITCH), :]
    o_ref[...] = _compress_rows(chunk, pe_ref, w1_ref, b1_ref, w2_ref, prepare)


def _compress_sample(pool, layer, page_table, pe, w1, b1, w2, n_seq):
    db, n_pages = page_table.shape
    rows = n_seq * n_pages * (PAGE_SIZE // CMP_BLOCK)
    const = lambda a: pl.BlockSpec(a.shape, lambda i, pt: (0,) * a.ndim)

    def page_spec(s, p):
        return pl.BlockSpec((None, None) + pool.shape[2:],
                            lambda i, pt: (layer, pt[i * n_seq + s, p], 0, 0))

    grid_spec = pltpu.PrefetchScalarGridSpec(
        num_scalar_prefetch=1,
        grid=(db // n_seq,),
        in_specs=[page_spec(s, p) for s in range(n_seq) for p in range(n_pages)]
        + [const(pe), const(w1), const(b1), const(w2)],
        out_specs=pl.BlockSpec((rows, 2 * KV_WIDTH), lambda i, pt: (i, 0)),
        scratch_shapes=[pltpu.VMEM((rows * CMP_PITCH, KV_WIDTH), F32)] * 2,
    )
    return pl.pallas_call(
        functools.partial(_compress_sample_kernel, n_seq, n_pages),
        grid_spec=grid_spec,
        out_shape=jax.ShapeDtypeStruct((db * n_pages * (PAGE_SIZE // CMP_BLOCK), 2 * KV_WIDTH), F32),
        compiler_params=pltpu.CompilerParams(dimension_semantics=("parallel",),
                                             vmem_limit_bytes=VMEM_LIMIT),
        name="compress_sample",
    )(page_table, *([pool] * (n_seq * n_pages)), pe, w1, b1, w2)


def _stack_heads(q, tq):
    lane_lo = lax.broadcasted_iota(jnp.int32, (tq, LANES), 1) < HEAD_DIM
    pieces = []
    for hh in range(NSA_HEADS):
        g = hh // GROUP
        c = hh // 2
        chunk = q[:, c * LANES:(c + 1) * LANES]
        if hh % 2 != g:
            chunk = pltpu.roll(chunk, HEAD_DIM, 1)
        pieces.append(jnp.where(lane_lo if g == 0 else jnp.logical_not(lane_lo), chunk, 0.0))
    return jnp.concatenate(pieces, axis=0).astype(BF16)


def _cmp_branch(qraw, ckv, qpos, tq):
    nb = MAX_SLC_BLOCKS
    kc = jnp.concatenate([ckv[:, 0:KV_WIDTH], ckv[:, 2 * KV_WIDTH:3 * KV_WIDTH]], axis=0).astype(BF16)
    vc = jnp.concatenate([ckv[:, KV_WIDTH:2 * KV_WIDTH], ckv[:, 3 * KV_WIDTH:]], axis=0).astype(BF16)
    s = _dot_nt(qraw, kc).reshape(NSA_HEADS, tq, 2 * nb)
    lane = lax.broadcasted_iota(jnp.int32, (tq, 2 * nb), 1)
    blk = jnp.where(lane < nb, 2 * lane, 2 * (lane - nb) + 1)
    mask = ((blk + 1) * CMP_BLOCK - 1) <= qpos
    p = _softmax_masked(s, mask[None])
    o_cmp = _dot(p.reshape(NSA_HEADS * tq, 2 * nb).astype(BF16), vc)
    scores = []
    sblk = lax.broadcasted_iota(jnp.int32, (tq, nb), 1)
    cur = qpos // SLC_BLOCK
    valid = sblk <= cur
    forced = ((sblk == 0) | (sblk == cur) | (sblk == cur - 1)).astype(F32)
    for g in range(KV_HEADS):
        imp = p[g * GROUP]
        for r in range(1, GROUP):
            imp = imp + p[g * GROUP + r]
        imp = imp[:, :nb] + imp[:, nb:]
        scores.append(jnp.where(valid, imp + FORCE_BONUS * forced, -1.0))
    return o_cmp, scores, valid


def _rank_blocks_wide(score, st_ref, cnt_ref, n_chunks):
    st_ref[...] = score.T
    cnt_ref[...] = jnp.zeros_like(cnt_ref)
    sub = lax.broadcasted_iota(jnp.int32, (SUBLANES, LANES), 0)
    n_vregs = MAX_SLC_BLOCKS // SUBLANES
    for c in range(n_vregs):
        @pl.when(c < n_chunks)
        def _():
            tiles = [st_ref[v * SUBLANES:(v + 1) * SUBLANES, :] for v in range(n_vregs)]
            cnts = [cnt_ref[v * SUBLANES:(v + 1) * SUBLANES, :] for v in range(n_vregs)]
            for ii in range(SUBLANES):
                row = tiles[c][ii:ii + 1, :]
                for v in range(n_vregs):
                    if v < c:
                        inc = jnp.where(row > tiles[v], 1.0, 0.0)
                    elif v > c:
                        inc = jnp.where(row >= tiles[v], 1.0, 0.0)
                    else:
                        inc = jnp.where(sub > ii, jnp.where(row >= tiles[v], 1.0, 0.0),
                                        jnp.where(row > tiles[v], 1.0, 0.0))
                    cnts[v] = cnts[v] + inc
            for v in range(n_vregs):
                cnt_ref[v * SUBLANES:(v + 1) * SUBLANES, :] = cnts[v]
    return cnt_ref[...].T


def _rank_blocks_narrow(score, n_blocks):
    lane = lax.broadcasted_iota(jnp.int32, score.shape, 1)
    cnt = jnp.zeros(score.shape, F32)
    for i in range(n_blocks):
        col = score[:, i:i + 1]
        cnt = cnt + jnp.where(lane > i, jnp.where(col >= score, 1.0, 0.0),
                              jnp.where(col > score, 1.0, 0.0))
    return cnt


def _slc_branch(q_aug, kv_ref, et_ref, n_full, qpos, tq, tk):
    ones_rows = jnp.where(lax.broadcasted_iota(jnp.int32, (HEAD_DIM, tk), 0) == 0, 1.0, 0.0).astype(BF16)

    def tile(kt, carry, diagonal):
        ms, accs = carry
        off = pl.multiple_of(kt * tk, tk)
        ka = jnp.concatenate([et_ref[:, pl.ds(off, tk)], kv_ref[:KV_WIDTH, pl.ds(off, tk)]], axis=0)
        s = _dot(q_aug, ka)
        if diagonal:
            visible = (off + lax.broadcasted_iota(jnp.int32, (tq, tk), 1)) <= qpos
        ps, m_new, alphas = [], [], []
        for hh in range(NSA_HEADS):
            s_h = s[hh * tq:(hh + 1) * tq, :]
            if diagonal:
                s_h = jnp.where(visible, s_h, NEG_INF)
            m_h = jnp.maximum(ms[hh], jnp.max(s_h, axis=-1, keepdims=True))
            ps.append(jnp.exp2(s_h - m_h).astype(BF16))
            alphas.append(jnp.exp2(ms[hh] - m_h))
            m_new.append(m_h)
        acc_new = []
        for g in range(KV_HEADS):
            v0 = KV_WIDTH + g * HEAD_DIM
            va = jnp.concatenate([kv_ref[v0:v0 + HEAD_DIM, pl.ds(off, tk)], ones_rows], axis=0)
            pv = _dot_nt(jnp.concatenate(ps[g * GROUP:(g + 1) * GROUP], axis=0), va)
            alpha = jnp.concatenate(alphas[g * GROUP:(g + 1) * GROUP], axis=0)
            acc_new.append(alpha * accs[g] + pv)
        return tuple(m_new), tuple(acc_new)

    init = (tuple(jnp.full((tq, 1), NEG_INF, F32) for _ in range(NSA_HEADS)),
            tuple(jnp.zeros((GROUP * tq, KV_WIDTH), F32) for _ in range(KV_HEADS)))
    rem = n_full % 4
    carry = lax.cond(rem % 2 == 1, lambda c: tile(0, c, False), lambda c: c, init)
    carry = lax.cond(rem >= 2,
                     lambda c: tile(rem % 2 + 1, tile(rem % 2, c, False), False), lambda c: c, carry)

    def quad(j, c):
        kt = rem + 4 * j
        for u in range(4):
            c = tile(kt + u, c, False)
        return c

    carry = lax.fori_loop(0, n_full // 4, quad, carry)
    _, accs = tile(n_full, carry, True)
    outs = [a * (1.0 / jnp.maximum(a[:, HEAD_DIM:HEAD_DIM + 1], 1e-30)) for a in accs]
    return jnp.concatenate([outs[0], pltpu.roll(outs[1], HEAD_DIM, 1)], axis=0)


def _softmax_pv(s, mask, vt, tq):
    nk = s.shape[1]
    rows = NSA_HEADS * tq
    if rows <= LANES:
        s3 = jnp.where(mask[None], s.reshape(NSA_HEADS, tq, nk), NEG_INF)
        p = jnp.exp2(s3 - jnp.max(s3, axis=-1, keepdims=True)).reshape(rows, nk).astype(BF16)
        if rows < LANES:
            p = jnp.concatenate([p, jnp.zeros((LANES - rows, nk), BF16)], axis=0)
        ones_rows = jnp.where(lax.broadcasted_iota(jnp.int32, (2 * SUBLANES, nk), 0) == 0,
                              1.0, 0.0).astype(BF16)
        ot = _dot_nt(jnp.concatenate([vt, ones_rows], axis=0), p)
        return (ot[:KV_WIDTH] * (1.0 / jnp.maximum(ot[KV_WIDTH:KV_WIDTH + 1], 1e-30))).T[:rows]
    ones_rows = jnp.where(lax.broadcasted_iota(jnp.int32, (HEAD_DIM, nk), 0) == 0, 1.0, 0.0).astype(BF16)
    outs = []
    for g in range(KV_HEADS):
        ps = []
        for hh in range(g * GROUP, (g + 1) * GROUP):
            s_h = jnp.where(mask, s[hh * tq:(hh + 1) * tq, :], NEG_INF)
            ps.append(jnp.exp2(s_h - jnp.max(s_h, axis=-1, keepdims=True)).astype(BF16))
        va = jnp.concatenate([vt[g * HEAD_DIM:(g + 1) * HEAD_DIM, :], ones_rows], axis=0)
        pv = _dot_nt(jnp.concatenate(ps, axis=0), va)
        outs.append(pv * (1.0 / jnp.maximum(pv[:, HEAD_DIM:HEAD_DIM + 1], 1e-30)))
    return jnp.concatenate([outs[0], pltpu.roll(outs[1], HEAD_DIM, 1)], axis=0)


def _slc_single(bias_rows, qrot, kt, vt, et, qpos, tq):
    nk = kt.shape[1]
    s = _dot(bias_rows, et) + _dot(qrot, kt)
    visible = lax.broadcasted_iota(jnp.int32, (tq, nk), 1) <= qpos
    return _softmax_pv(s, visible, vt, tq)


def _win_branch(qrot, kt, vt, kpos, qpos, tq):
    diff = qpos - kpos
    return _softmax_pv(_dot(qrot, kt), (diff >= 0) & (diff < WINDOW), vt, tq)


def _combine(o_cmp, o_slc, o_win, gates, tq):
    lane_lo = lax.broadcasted_iota(jnp.int32, (tq, LANES), 1) < HEAD_DIM
    heads = []
    for hh in range(NSA_HEADS):
        tot = None
        for br, o in enumerate((o_cmp, o_slc, o_win)):
            col = br * NSA_HEADS + hh
            term = gates[:, col:col + 1] * o[hh * tq:(hh + 1) * tq, :]
            tot = term if tot is None else tot + term
        heads.append(tot)
    chunks = []
    for c in range(NSA_HEADS // 2):
        a, b = heads[2 * c], heads[2 * c + 1]
        if c // 2 == 0:
            chunks.append(jnp.where(lane_lo, a, pltpu.roll(b, HEAD_DIM, 1)))
        else:
            chunks.append(jnp.where(lane_lo, pltpu.roll(a, HEAD_DIM, 1), b))
    return jnp.concatenate(chunks, axis=1)


def _selection_bias(rank, valid):
    return jnp.where((rank < float(N_SELECT)) & valid, 0.0, NEG_INF).astype(BF16)


def _pad_ckv(ckv):
    n = ckv.shape[0]
    if n == MAX_SLC_BLOCKS:
        return ckv
    return jnp.concatenate([ckv, jnp.zeros((MAX_SLC_BLOCKS - n, ckv.shape[1]), F32)], axis=0)


def _nsa_prompt_kernel(qraw_ref, qrot_ref, gate_ref, ckv_ref, slc_ref, win_ref, et_ref,
                       o_ref, st_ref, cnt_ref):
    tq = QUERY_BLOCK
    i = pl.program_id(1)
    qpos0 = i * tq
    qpos = qpos0 + lax.broadcasted_iota(jnp.int32, (tq, 1), 0)
    qraw = _stack_heads(qraw_ref[...].astype(F32), tq)
    qrot = _stack_heads(qrot_ref[...].astype(F32), tq)

    o_cmp, scores, valid = _cmp_branch(qraw, _pad_ckv(ckv_ref[...]), qpos, tq)
    n_chunks = (qpos0 + tq - 1) // (SLC_BLOCK * SUBLANES) + 1
    bias = []
    for g in range(KV_HEADS):
        rank = _rank_blocks_wide(scores[g], st_ref, cnt_ref, n_chunks)
        bias.append(_selection_bias(rank, valid))
    bias_rows = jnp.concatenate([bias[0]] * GROUP + [bias[1]] * GROUP, axis=0)
    q_aug = jnp.concatenate([bias_rows, qrot], axis=1)
    o_slc = _slc_branch(q_aug, slc_ref, et_ref, qpos0 // SLC_KEY_TILE, qpos, tq, SLC_KEY_TILE)

    nk = WINDOW + tq
    start = pl.multiple_of(jnp.maximum(qpos0 - WINDOW, 0), tq)
    kpos = start + lax.broadcasted_iota(jnp.int32, (1, nk), 1)
    o_win = _win_branch(qrot, win_ref[:KV_WIDTH, pl.ds(start, nk)], win_ref[KV_WIDTH:, pl.ds(start, nk)],
                        kpos, qpos, tq)

    o_ref[...] = _combine(o_cmp, o_slc, o_win, gate_ref[...], tq).astype(o_ref.dtype)


def _nsa_prompt(qraw, qrot, gates, ckv, slcb, winb, et):
    b, t, _ = qraw.shape
    tq = QUERY_BLOCK
    blk = lambda width: pl.BlockSpec((None, tq, width), lambda bi, i: (bi, i, 0))
    per_batch = lambda a: pl.BlockSpec((None,) + a.shape[1:], lambda bi, i: (bi, 0, 0))
    return pl.pallas_call(
        _nsa_prompt_kernel,
        grid=(b, t // tq),
        in_specs=[blk(NSA_WIDTH), blk(NSA_WIDTH), blk(LANES), per_batch(ckv), per_batch(slcb),
                  per_batch(winb), pl.BlockSpec(et.shape, lambda bi, i: (0, 0))],
        out_specs=blk(NSA_WIDTH),
        out_shape=jax.ShapeDtypeStruct((b, t, NSA_WIDTH), BF16),
        scratch_shapes=[pltpu.VMEM((MAX_SLC_BLOCKS, tq), F32), pltpu.VMEM((MAX_SLC_BLOCKS, tq), F32)],
        compiler_params=pltpu.CompilerParams(dimension_semantics=("parallel", "arbitrary"),
                                             vmem_limit_bytes=VMEM_LIMIT),
        name="nsa_prompt",
    )(qraw, qrot, gates, ckv, slcb, winb, et)


def _nsa_sample_kernel(n_seq, n_pages, past, s_new, pt_ref, *refs):
    n_in = n_seq * n_pages
    page_refs = refs[:n_in]
    (qraw_ref, qrot_ref, gate_ref, ckv_ref, slcn_ref, wins_ref, winn_ref, et_ref) = refs[n_in:n_in + 8]
    o_ref, wout_ref = refs[-2:]
    tq = SAMPLE_ROWS
    qpos = past + lax.broadcasted_iota(jnp.int32, (tq, 1), 0)
    n_buf = wins_ref.shape[2]
    n_blocks = (past + tq - 1) // SLC_BLOCK + 1
    pad_rows = jnp.zeros((LANES - tq, 2 * KV_WIDTH), F32)
    wpos = past - n_buf + lax.broadcasted_iota(jnp.int32, (1, n_buf + LANES), 1)
    lane = lax.broadcasted_iota(jnp.int32, (2 * KV_WIDTH, LANES), 1)
    for s in range(n_seq):
        slc_new = jnp.concatenate([slcn_ref[s], pad_rows], axis=0).T
        tiles = [page_refs[s * n_pages + p][...].astype(BF16) for p in range(n_pages)]
        tiles.append(slc_new.astype(BF16))
        kt = jnp.concatenate([tl[:KV_WIDTH] for tl in tiles], axis=1)
        vt = jnp.concatenate([tl[KV_WIDTH:] for tl in tiles], axis=1)

        qraw = _stack_heads(qraw_ref[s], tq)
        qrot = _stack_heads(qrot_ref[s], tq)
        o_cmp, scores, valid = _cmp_branch(qraw, _pad_ckv(ckv_ref[s]), qpos, tq)
        bias = [_selection_bias(_rank_blocks_narrow(scores[g], n_blocks), valid)
                for g in range(KV_HEADS)]
        bias_rows = jnp.concatenate([bias[0]] * GROUP + [bias[1]] * GROUP, axis=0)
        o_slc = _slc_single(bias_rows, qrot, kt, vt, et_ref[...], qpos, tq)

        state = wins_ref[s]
        win_new = jnp.concatenate([winn_ref[s], pad_rows], axis=0).T
        kt = jnp.concatenate([state[:KV_WIDTH], win_new[:KV_WIDTH]], axis=1).astype(BF16)
        vt = jnp.concatenate([state[KV_WIDTH:], win_new[KV_WIDTH:]], axis=1).astype(BF16)
        o_win = _win_branch(qrot, kt, vt, wpos, qpos, tq)
        o_ref[s] = _combine(o_cmp, o_slc, o_win, gate_ref[s], tq)

        shifted = pltpu.roll(state, n_buf - s_new, 1)
        tail = jnp.where(lane >= LANES - s_new, pltpu.roll(win_new, LANES - s_new, 1),
                         shifted[:, n_buf - LANES:])
        wout_ref[s, :, :n_buf - LANES] = shifted[:, :n_buf - LANES]
        wout_ref[s, :, n_buf - LANES:] = tail


def _nsa_sample(pool, win_state, layer, page_table, qraw, qrot, gates, ckv, slc_new, win_new, et,
                n_seq, s_new, carried):
    db, n_pages = page_table.shape
    past = n_pages * PAGE_SIZE
    tq = SAMPLE_ROWS
    n_buf = win_state.shape[3]
    seq = lambda a: pl.BlockSpec((n_seq,) + a.shape[1:], lambda i, pt: (i, 0, 0))

    def page_spec(s, p):
        return pl.BlockSpec((None, None) + pool.shape[2:],
                            lambda i, pt: (layer, pt[i * n_seq + s, p], 0, 0))

    state_spec = pl.BlockSpec((None, n_seq) + win_state.shape[2:], lambda i, pt: (layer, i, 0, 0))
    grid_spec = pltpu.PrefetchScalarGridSpec(
        num_scalar_prefetch=1,
        grid=(db // n_seq,),
        in_specs=[page_spec(s, p) for s in range(n_seq) for p in range(n_pages)]
        + [seq(qraw), seq(qrot), seq(gates), seq(ckv), seq(slc_new), state_spec, seq(win_new),
           pl.BlockSpec(et.shape, lambda i, pt: (0, 0))]
        + [pl.BlockSpec(memory_space=pl.ANY)] * len(carried),
        out_specs=[pl.BlockSpec((n_seq, tq, NSA_WIDTH), lambda i, pt: (i, 0, 0)),
                   pl.BlockSpec((None, n_seq) + win_state.shape[2:], lambda i, pt: (layer, i, 0, 0))],
    )
    n_operands = 1 + n_seq * n_pages + 8
    return pl.pallas_call(
        functools.partial(_nsa_sample_kernel, n_seq, n_pages, past, s_new),
        grid_spec=grid_spec,
        out_shape=[jax.ShapeDtypeStruct((db, tq, NSA_WIDTH), F32),
                   jax.ShapeDtypeStruct(win_state.shape, F32)],
        input_output_aliases={n_operands + k: 1 + k for k in range(len(carried))},
        compiler_params=pltpu.CompilerParams(dimension_semantics=("parallel",),
                                             vmem_limit_bytes=VMEM_LIMIT),
        name="nsa_sample",
    )(page_table, *([pool] * (n_seq * n_pages)), qraw, qrot, gates, ckv, slc_new, win_state, win_new, et,
      *carried)


def _post_kernel(chunk, alpha, n_ffn_chunks, x_ref, o_ref, u_ref, v_ref, gw_ref, gb_ref,
                 wout_ref, ln1g_ref, ln1b_ref, wg_ref, wu_ref, wd_ref, ln2g_ref, ln2b_ref, y_ref):
    tm = x_ref.shape[0]
    c = GMLP_CHUNK
    row = lax.broadcasted_iota(jnp.int32, (c, c), 0)
    col = lax.broadcasted_iota(jnp.int32, (c, c), 1)
    causal = (col <= row) & ((row // chunk) == (col // chunk))
    w_cat = jnp.concatenate([jnp.where(causal, gw_ref[h], 0.0) for h in range(GMLP_GROUPS)],
                            axis=1).astype(BF16)
    lane_group = lax.broadcasted_iota(jnp.int32, (c, GMLP_WIDTH), 1) // (GMLP_WIDTH // GMLP_GROUPS)
    mixed = []
    for sub in range(tm // c):
        v = v_ref[sub * c:(sub + 1) * c, :]
        v_diag = jnp.concatenate([jnp.where(lane_group == h, v, 0.0) for h in range(GMLP_GROUPS)],
                                 axis=0).astype(BF16)
        s = _dot(w_cat, v_diag) + gb_ref[...]
        mixed.append(u_ref[sub * c:(sub + 1) * c, :].astype(F32) * s)
    o_gmlp = jnp.concatenate(mixed, axis=0)
    mix = jnp.concatenate([o_ref[...], o_gmlp], axis=1).astype(BF16)
    h = _dot(mix, wout_ref[...])
    x1 = _layer_norm(alpha * x_ref[...] + h, ln1g_ref[...], ln1b_ref[...])
    x1b = x1.astype(BF16)
    f = None
    fc = wg_ref.shape[1] // n_ffn_chunks
    for j in range(n_ffn_chunks):
        gate = _dot(x1b, wg_ref[:, j * fc:(j + 1) * fc])
        up = _dot(x1b, wu_ref[:, j * fc:(j + 1) * fc])
        a = (jax.nn.silu(gate) * up).astype(BF16)
        part = _dot(a, wd_ref[j * fc:(j + 1) * fc, :])
        f = part if f is None else f + part
    y_ref[...] = _layer_norm(alpha * x1 + f, ln2g_ref[...], ln2b_ref[...])


def _post(x, o_nsa, u, v, gw, gb, wout, ln1g, ln1b, wg, wu, wd, ln2g, ln2b, chunk, alpha, tm):
    n, d = x.shape
    ffn = wg.shape[1]
    n_ffn_chunks = next(k for k in (4, 2, 1) if ffn % (k * LANES) == 0)
    row = lambda width: pl.BlockSpec((tm, width), lambda i: (i, 0))
    const = lambda a: pl.BlockSpec(a.shape, lambda i: (0,) * a.ndim, pipeline_mode=pl.Buffered(1))
    consts = (gw, gb, wout, ln1g, ln1b, wg, wu, wd, ln2g, ln2b)
    return pl.pallas_call(
        functools.partial(_post_kernel, chunk, alpha, n_ffn_chunks),
        grid=(n // tm,),
        in_specs=[row(d), row(NSA_WIDTH), row(GMLP_WIDTH), row(GMLP_WIDTH)] + [const(a) for a in consts],
        out_specs=row(d),
        out_shape=jax.ShapeDtypeStruct((n, d), F32),
        compiler_params=pltpu.CompilerParams(dimension_semantics=("parallel",),
                                             vmem_limit_bytes=VMEM_LIMIT),
        name="post",
    )(x, o_nsa, u, v, *consts)


def _pack_w_in_t(w_in):
    wt = w_in.T
    o_gate = NSA_WIDTH + 3 * 2 * KV_WIDTH
    o_gmlp = o_gate + 3 * NSA_HEADS
    gate = jnp.pad(wt[o_gate:o_gmlp], ((0, LANES - 3 * NSA_HEADS), (0, 0)))
    return jnp.concatenate([wt[:o_gate], gate, wt[o_gmlp:]], axis=0).astype(BF16)


def _pack_compress(pe, w1, b1, w2):
    eye = jnp.eye(KV_HEADS, dtype=F32)
    zero = jnp.zeros_like(w1)
    w1p = jnp.stack([jnp.concatenate([w1, zero], axis=-1), jnp.concatenate([zero, w1], axis=-1)], axis=2)
    w1p = w1p.reshape(2, CMP_BLOCK * KV_WIDTH, KV_HEADS * CMP_HIDDEN)
    pep = jnp.broadcast_to(pe[:, :, None, :], (2, CMP_BLOCK, KV_HEADS, HEAD_DIM)).reshape(2, 1, -1)
    b1p = jnp.tile(b1, (1, KV_HEADS))[:, None, :]
    w2p = jnp.einsum('khd,gG->kghGd', w2, eye).reshape(2, KV_HEADS * CMP_HIDDEN, KV_WIDTH)
    return pep, w1p.astype(BF16), b1p, w2p.astype(BF16)


def _pack_gmlp(ws, bs, chunk):
    reps = GMLP_CHUNK // chunk
    gw = jnp.tile(ws[:, :chunk, :chunk], (1, reps, reps))
    gb = jnp.tile(jnp.repeat(bs[:, :chunk].T, GMLP_WIDTH // GMLP_GROUPS, axis=1), (reps, 1))
    return gw, gb


def _rope_tables(pos):
    half = HEAD_DIM // 2
    inv_freq = ROPE_THETA ** (-jnp.arange(half, dtype=F32) / half)
    ang = pos.astype(F32)[:, None] * inv_freq[None, :]
    cos, sin = jnp.cos(ang), jnp.sin(ang)
    return jnp.concatenate([cos, cos], axis=1), jnp.concatenate([-sin, sin], axis=1)


def _block_indicator(n_keys):
    key_block = np.arange(n_keys)[None, :] // SLC_BLOCK
    return jnp.asarray(key_block == np.arange(MAX_SLC_BLOCKS)[:, None], dtype=BF16)


def _feature_major(a):
    lead = a.shape[:-4]
    n = len(lead)
    perm = tuple(range(n)) + (n + 1, n + 2, n + 3, n)
    return jnp.transpose(a, perm).reshape(lead + (2 * KV_WIDTH, a.shape[-4]))


def _position_major(a):
    lead = a.shape[:-2]
    n = len(lead)
    a = a.reshape(lead + (2, KV_HEADS, HEAD_DIM, a.shape[-1]))
    return jnp.transpose(a, tuple(range(n)) + (n + 3, n, n + 1, n + 2))


def kernel(x_prompt, x_sample, cache_cmp_kv, cache_slc_kv, state_win_kv, page_table,
           w_in, cmp_pe, cmp_w1, cmp_b1, cmp_w2, gmlp_ln_g, gmlp_ln_b, gmlp_ws, gmlp_bs,
           w_out, ln1_g, ln1_b, w_gate, w_up, w_down, ln2_g, ln2_b):
    depth = w_in.shape[0]
    b, t, d = x_prompt.shape
    db, s_new, _ = x_sample.shape
    n_pages = page_table.shape[1]
    past = n_pages * PAGE_SIZE
    n_buf = state_win_kv.shape[2]
    kv_tail = (2, KV_HEADS, HEAD_DIM)
    rows_s = SAMPLE_ROWS
    assert t % SLC_KEY_TILE == 0 and t >= WINDOW + QUERY_BLOCK and t // SLC_BLOCK <= MAX_SLC_BLOCKS
    assert s_new <= rows_s and (past + rows_s - 1) // SLC_BLOCK < MAX_SLC_BLOCKS
    assert past % CMP_BLOCK == 0 and past % CMP_BLOCK + s_new < CMP_BLOCK
    assert n_buf % LANES == 0 and n_buf >= LANES
    alpha = (2 * depth) ** 0.25

    tm_p = 512
    tm_s = min(512, db * rows_s)
    seq_per_step = 2 if db % 2 == 0 else 1
    attn_seq_per_step = 4 if db % 4 == 0 else seq_per_step
    cos_p, sin_p = _rope_tables(jnp.arange(t, dtype=jnp.int32))
    cos_s, sin_s = _rope_tables(past + jnp.arange(rows_s, dtype=jnp.int32))
    lane_reps = LANES // HEAD_DIM
    cos_pl, sin_pl = jnp.tile(cos_p, (1, lane_reps)), jnp.tile(sin_p, (1, lane_reps))
    cos_pt, sin_pt = cos_p.T, sin_p.T
    cos_s = jnp.tile(cos_s, (tm_s // rows_s, lane_reps))
    sin_s = jnp.tile(sin_s, (tm_s // rows_s, lane_reps))
    et_p = _block_indicator(t)
    et_s = _block_indicator(past + LANES)
    pool_cmp = _feature_major(cache_cmp_kv)
    pool_slc = _feature_major(cache_slc_kv)
    win_state = _feature_major(state_win_kv)

    yp = x_prompt.reshape(b * t, d)
    ys = jnp.pad(x_sample, ((0, 0), (0, rows_s - s_new), (0, 0))).reshape(db * rows_s, d)
    outs = {k: [] for k in ("s_cmp", "s_slc", "s_v")}
    assert min(WINDOW, t) == tm_p
    new_kv_prompt = ()
    win_next = ()
    cmp_tile = min(128, b * t // CMP_BLOCK)

    for l in range(depth):
        wt = _pack_w_in_t(w_in[l])
        pe_p, w1_p, b1_p, w2_p = _pack_compress(cmp_pe[l], cmp_w1[l], cmp_b1[l], cmp_w2[l])
        ln_g = gmlp_ln_g[l][None, :]
        ln_b = gmlp_ln_b[l][None, :]
        tail = (w_out[l].astype(BF16), ln1_g[l][None, :], ln1_b[l][None, :], w_gate[l].astype(BF16),
                w_up[l].astype(BF16), w_down[l].astype(BF16), ln2_g[l][None, :], ln2_b[l][None, :])

        qraw, qrot, gates, u, v, cmp_kv, *new_kv_prompt, slcb, winb = _inproj_prompt(
            yp, wt, cos_pl, sin_pl, cos_pt, sin_pt, ln_g, ln_b, tm_p, b, t, depth, l, new_kv_prompt)
        ckv = _compress_prompt(cmp_kv.reshape(b * t // CMP_BLOCK, CMP_BLOCK * 2 * KV_WIDTH),
                               pe_p, w1_p, b1_p, w2_p, cmp_tile)
        r3 = lambda a: a.reshape(b, t, a.shape[-1])
        o_nsa = _nsa_prompt(r3(qraw), r3(qrot), r3(gates), ckv.reshape(b, t // SLC_BLOCK, 4 * KV_WIDTH),
                            slcb, winb, et_p)
        gw, gb = _pack_gmlp(gmlp_ws[l], gmlp_bs[l], GMLP_CHUNK)
        yp = _post(yp, o_nsa.reshape(b * t, NSA_WIDTH), u, v, gw, gb, *tail, GMLP_CHUNK, alpha, tm_p)

        qraw, qrot, gates, u, v, cmp_kv, slc_kv, win_kv = _inproj_sample(
            ys, wt, cos_s, sin_s, ln_g, ln_b, tm_s)
        ckv = _compress_sample(pool_cmp, l, page_table, pe_p, w1_p, b1_p, w2_p, attn_seq_per_step)
        s3 = lambda a: a.reshape(db, rows_s, a.shape[-1])
        o_nsa, *win_next = _nsa_sample(pool_slc, win_state, l, page_table, s3(qraw), s3(qrot), s3(gates),
                                       ckv.reshape(db, past // SLC_BLOCK, 4 * KV_WIDTH),
                                       s3(slc_kv), s3(win_kv), et_s, attn_seq_per_step, s_new, win_next)
        gw, gb = _pack_gmlp(gmlp_ws[l], gmlp_bs[l], rows_s)
        ys = _post(ys, o_nsa.reshape(db * rows_s, NSA_WIDTH), u, v, gw, gb, *tail, rows_s, alpha, tm_s)
        new = lambda a: a.reshape((db, rows_s) + a.shape[1:])[:, :s_new]
        outs["s_cmp"].append(new(cmp_kv).reshape((db, s_new) + kv_tail))
        outs["s_slc"].append(new(slc_kv).reshape((db, s_new) + kv_tail))
        outs["s_v"].append(new(v))

    y_sample = ys.reshape(db, rows_s, d)[:, :s_new]
    p_cmp, p_slc, p_win = (_position_major(a) for a in new_kv_prompt)
    return (yp.reshape(b, t, d), y_sample, p_cmp, p_slc, p_win,
            jnp.stack(outs["s_cmp"]), jnp.stack(outs["s_slc"]),
            _position_major(win_next[0]), jnp.stack(outs["s_v"]))
```

```python
import functools
import math

import numpy as np
import jax
import jax.numpy as jnp
from jax import lax
from jax.experimental import pallas as pl
from jax.experimental.pallas import tpu as pltpu

F32 = jnp.float32
BF16 = jnp.bfloat16

LANES = 128
SUBLANES = 8
HEAD_DIM = 64
NSA_HEADS = 8
KV_HEADS = 2
GROUP = NSA_HEADS // KV_HEADS
NSA_WIDTH = NSA_HEADS * HEAD_DIM
KV_WIDTH = KV_HEADS * HEAD_DIM
GMLP_WIDTH = 512
GMLP_GROUPS = 8
GMLP_CHUNK = 128
CMP_BLOCK = 32
CMP_HIDDEN = 256
CMP_PITCH = CMP_BLOCK + SUBLANES
SLC_BLOCK = 64
N_SELECT = 16
WINDOW = 512
QUERY_BLOCK = 128
RANK_TOKENS = LANES
PAGE_SIZE = 128
FORCE_BONUS = 1.0e4
ROPE_THETA = 10000.0
LN_EPS = 1e-5
NEG_INF = -1e30
MAX_SLC_BLOCKS = LANES
SAMPLE_ROWS = SUBLANES
SLC_KEY_TILE = 512
VMEM_LIMIT = 56 * 1024 * 1024
Q_SCALE = HEAD_DIM ** -0.5 * math.log2(math.e)

C_Q = 0
C_CMP = C_Q + NSA_WIDTH
C_SLC = C_CMP + 2 * KV_WIDTH
C_WIN = C_SLC + 2 * KV_WIDTH
C_GATE = C_WIN + 2 * KV_WIDTH
C_U = C_GATE + LANES
C_V = C_U + GMLP_WIDTH
C_END = C_V + GMLP_WIDTH


def _dot(a, b):
    return jnp.dot(a, b, preferred_element_type=F32)


def _dot_nt(a, b):
    return lax.dot_general(a, b, (((1,), (1,)), ((), ())), preferred_element_type=F32)


def _layer_norm(x, g, b):
    mu = jnp.mean(x, axis=-1, keepdims=True)
    xc = x - mu
    var = jnp.mean(xc * xc, axis=-1, keepdims=True)
    return xc * lax.rsqrt(var + LN_EPS) * g + b


def _softmax_masked(s, mask):
    s = jnp.where(mask, s, NEG_INF)
    m = jnp.max(s, axis=-1, keepdims=True)
    e = jnp.where(mask, jnp.exp2(s - m), 0.0)
    return e * (1.0 / jnp.maximum(jnp.sum(e, axis=-1, keepdims=True), 1e-30))


def _rope_lanes(c, cos, sin, first_half):
    partner = jnp.where(first_half, pltpu.roll(c, LANES - HEAD_DIM // 2, 1),
                        pltpu.roll(c, HEAD_DIM // 2, 1))
    return c * cos + partner * sin


def _inproj_common(xb, wt_ref, cos, sin, first_half, lng_ref, lnb_ref,
                   qraw_ref, qrot_ref, gate_ref, u_ref, v_ref):
    zq = _dot_nt(xb, wt_ref[C_Q:C_CMP, :])
    qraw_ref[...] = (zq * Q_SCALE).astype(qraw_ref.dtype)
    for c in range(NSA_WIDTH // LANES):
        qrot_ref[:, c * LANES:(c + 1) * LANES] = (_rope_lanes(
            zq[:, c * LANES:(c + 1) * LANES], cos, sin, first_half) * Q_SCALE).astype(qrot_ref.dtype)
    gate_ref[...] = jax.nn.sigmoid(_dot_nt(xb, wt_ref[C_GATE:C_U, :]))
    u_ref[...] = jax.nn.gelu(_dot_nt(xb, wt_ref[C_U:C_V, :])).astype(u_ref.dtype)
    zv = jax.nn.gelu(_dot_nt(xb, wt_ref[C_V:C_END, :]))
    v_ref[...] = _layer_norm(zv, lng_ref[...], lnb_ref[...]).astype(v_ref.dtype)


def _inproj_prompt_kernel(n_carried, x_ref, wt_ref, cos_ref, sin_ref, cost_ref, sint_ref, lng_ref,
                          lnb_ref, *refs):
    (qraw_ref, qrot_ref, gate_ref, u_ref, v_ref,
     cmp_ref, cmpt_ref, slct_ref, wint_ref, slcb_ref, winb_ref) = refs[n_carried:]
    xb = x_ref[...].astype(BF16)
    cos = cos_ref[...]
    lane = lax.broadcasted_iota(jnp.int32, cos.shape, 1)
    first_half = (lane % HEAD_DIM) < (HEAD_DIM // 2)
    _inproj_common(xb, wt_ref, cos, sin_ref[...], first_half, lng_ref, lnb_ref,
                   qraw_ref, qrot_ref, gate_ref, u_ref, v_ref)
    cmp_ref[...] = _dot_nt(xb, wt_ref[C_CMP:C_SLC, :])
    cmpt_ref[...] = _dot_nt(wt_ref[C_CMP:C_SLC, :], xb)
    cost = cost_ref[...]
    sint = sint_ref[...]
    half = HEAD_DIM // 2
    for row0, f_ref, b_ref in ((C_SLC, slct_ref, slcb_ref), (C_WIN, wint_ref, winb_ref)):
        zt = _dot_nt(wt_ref[row0:row0 + 2 * KV_WIDTH, :], xb)
        for g in range(KV_HEADS):
            kg = zt[g * HEAD_DIM:(g + 1) * HEAD_DIM, :]
            partner = jnp.concatenate([kg[half:, :], kg[:half, :]], axis=0)
            kr = kg * cost + partner * sint
            f_ref[g * HEAD_DIM:(g + 1) * HEAD_DIM, :] = kr
            b_ref[g * HEAD_DIM:(g + 1) * HEAD_DIM, :] = kr.astype(BF16)
        f_ref[KV_WIDTH:, :] = zt[KV_WIDTH:, :]
        b_ref[KV_WIDTH:, :] = zt[KV_WIDTH:, :].astype(BF16)


def _inproj_sample_kernel(x_ref, wt_ref, cos_ref, sin_ref, lng_ref, lnb_ref,
                          qraw_ref, qrot_ref, gate_ref, u_ref, v_ref, cmp_ref, slc_ref, win_ref):
    xb = x_ref[...].astype(BF16)
    cos = cos_ref[...]
    sin = sin_ref[...]
    lane = lax.broadcasted_iota(jnp.int32, cos.shape, 1)
    first_half = (lane % HEAD_DIM) < (HEAD_DIM // 2)
    _inproj_common(xb, wt_ref, cos, sin, first_half, lng_ref, lnb_ref,
                   qraw_ref, qrot_ref, gate_ref, u_ref, v_ref)
    cmp_ref[...] = _dot_nt(xb, wt_ref[C_CMP:C_SLC, :])
    for row0, f_ref in ((C_SLC, slc_ref), (C_WIN, win_ref)):
        z = _dot_nt(xb, wt_ref[row0:row0 + 2 * KV_WIDTH, :])
        f_ref[:, :KV_WIDTH] = _rope_lanes(z[:, :KV_WIDTH], cos, sin, first_half)
        f_ref[:, KV_WIDTH:] = z[:, KV_WIDTH:]


def _inproj_prompt(x, wt, cos, sin, cost, sint, ln_g, ln_b, tm, b, t, depth, layer, carried):
    n, d = x.shape
    n_tab = t // tm
    row = lambda width: pl.BlockSpec((tm, width), lambda i: (i, 0))
    const = lambda a: pl.BlockSpec(a.shape, lambda i: (0,) * a.ndim)
    tab = pl.BlockSpec((tm, LANES), lambda i: (i % n_tab, 0))
    tabt = pl.BlockSpec((HEAD_DIM, tm), lambda i: (0, i % n_tab))
    fm = pl.BlockSpec((None, 2 * KV_WIDTH, tm), lambda i: (i // n_tab, 0, i % n_tab))
    fm_layer = pl.BlockSpec((None, None, 2 * KV_WIDTH, tm), lambda i: (layer, i // n_tab, 0, i % n_tab))
    keep_layer = pl.BlockSpec((None, None, 2 * KV_WIDTH, tm), lambda i: (layer, i // n_tab, 0, 0))
    row_out = ((NSA_WIDTH, BF16), (NSA_WIDTH, BF16), (LANES, F32), (GMLP_WIDTH, BF16), (GMLP_WIDTH, BF16),
               (2 * KV_WIDTH, F32))
    n_in = 8
    n_row = len(row_out)
    stacked = lambda width: jax.ShapeDtypeStruct((depth, b, 2 * KV_WIDTH, width), F32)
    return pl.pallas_call(
        functools.partial(_inproj_prompt_kernel, len(carried)),
        grid=(n // tm,),
        in_specs=[row(d), const(wt), tab, tab, tabt, tabt, const(ln_g), const(ln_b)]
        + [pl.BlockSpec(memory_space=pl.ANY)] * len(carried),
        out_specs=[row(wd) for wd, _ in row_out] + [fm_layer, fm_layer, keep_layer, fm, fm],
        out_shape=[jax.ShapeDtypeStruct((n, wd), dt) for wd, dt in row_out]
        + [stacked(t), stacked(t), stacked(tm)]
        + [jax.ShapeDtypeStruct((b, 2 * KV_WIDTH, t), BF16)] * 2,
        input_output_aliases={n_in + k: n_row + k for k in range(len(carried))},
        compiler_params=pltpu.CompilerParams(dimension_semantics=("arbitrary",),
                                             vmem_limit_bytes=VMEM_LIMIT),
        name="inproj_prompt",
    )(x, wt, cos, sin, cost, sint, ln_g, ln_b, *carried)


def _inproj_sample(x, wt, cos, sin, ln_g, ln_b, tm):
    n, d = x.shape
    row = lambda width: pl.BlockSpec((tm, width), lambda i: (i, 0))
    const = lambda a: pl.BlockSpec(a.shape, lambda i: (0,) * a.ndim)
    widths = (NSA_WIDTH, NSA_WIDTH, LANES, GMLP_WIDTH, GMLP_WIDTH,
              2 * KV_WIDTH, 2 * KV_WIDTH, 2 * KV_WIDTH)
    return pl.pallas_call(
        _inproj_sample_kernel,
        grid=(n // tm,),
        in_specs=[row(d), const(wt), const(cos), const(sin), const(ln_g), const(ln_b)],
        out_specs=[row(wd) for wd in widths],
        out_shape=[jax.ShapeDtypeStruct((n, wd), F32) for wd in widths],
        compiler_params=pltpu.CompilerParams(dimension_semantics=("parallel",),
                                             vmem_limit_bytes=VMEM_LIMIT),
        name="inproj_sample",
    )(x, wt, cos, sin, ln_g, ln_b)


def _compress_rows(chunk, pe_ref, w1_ref, b1_ref, w2_ref):
    outs = []
    for k in range(2):
        a = jnp.concatenate([chunk(l, k) for l in range(CMP_BLOCK)], axis=1)
        a = (a + pe_ref[k]).astype(BF16)
        h = jax.nn.silu(_dot(a, w1_ref[k]) + b1_ref[k])
        outs.append(_dot(h.astype(BF16), w2_ref[k]))
    return jnp.concatenate(outs, axis=1)


def _compress_prompt_kernel(x_ref, pe_ref, w1_ref, b1_ref, w2_ref, o_ref):
    row_w = 2 * KV_WIDTH
    chunk = lambda l, k: x_ref[:, l * row_w + k * KV_WIDTH:l * row_w + (k + 1) * KV_WIDTH]
    o_ref[...] = _compress_rows(chunk, pe_ref, w1_ref, b1_ref, w2_ref)


def _compress_prompt(x, pe, w1, b1, w2, tm):
    n = x.shape[0]
    const = lambda a: pl.BlockSpec(a.shape, lambda i: (0,) * a.ndim)
    return pl.pallas_call(
        _compress_prompt_kernel,
        grid=(n // tm,),
        in_specs=[pl.BlockSpec((tm, x.shape[1]), lambda i: (i, 0)),
                  const(pe), const(w1), const(b1), const(w2)],
        out_specs=pl.BlockSpec((tm, 2 * KV_WIDTH), lambda i: (i, 0)),
        out_shape=jax.ShapeDtypeStruct((n, 2 * KV_WIDTH), F32),
        compiler_params=pltpu.CompilerParams(dimension_semantics=("parallel",),
                                             vmem_limit_bytes=VMEM_LIMIT),
        name="compress_prompt",
    )(x, pe, w1, b1, w2)


def _compress_sample_kernel(n_seq, n_pages, pt_ref, *refs):
    page_refs = refs[:n_seq * n_pages]
    pe_ref, w1_ref, b1_ref, w2_ref, o_ref, posk_ref, posv_ref = refs[n_seq * n_pages:]
    blocks_per_page = PAGE_SIZE // CMP_BLOCK
    for j, page_ref in enumerate(page_refs):
        for dst, rows in ((posk_ref, page_ref[:KV_WIDTH, :].T), (posv_ref, page_ref[KV_WIDTH:, :].T)):
            for n in range(blocks_per_page):
                r0 = (j * blocks_per_page + n) * CMP_PITCH
                dst[r0:r0 + CMP_BLOCK, :] = rows[n * CMP_BLOCK:(n + 1) * CMP_BLOCK, :]
    m = n_seq * n_pages * blocks_per_page
    chunk = lambda l, k: (posk_ref, posv_ref)[k][pl.ds(l, m, stride=CMP_PITCH), :]
    o_ref[...] = _compress_rows(chunk, pe_ref, w1_ref, b1_ref, w2_ref)


def _compress_sample(pool, layer, page_table, pe, w1, b1, w2, n_seq):
    db, n_pages = page_table.shape
    rows = n_seq * n_pages * (PAGE_SIZE // CMP_BLOCK)
    const = lambda a: pl.BlockSpec(a.shape, lambda i, pt: (0,) * a.ndim)

    def page_spec(s, p):
        return pl.BlockSpec((None, None) + pool.shape[2:],
                            lambda i, pt: (layer, pt[i * n_seq + s, p], 0, 0))

    grid_spec = pltpu.PrefetchScalarGridSpec(
        num_scalar_prefetch=1,
        grid=(db // n_seq,),
        in_specs=[page_spec(s, p) for s in range(n_seq) for p in range(n_pages)]
        + [const(pe), const(w1), const(b1), const(w2)],
        out_specs=pl.BlockSpec((rows, 2 * KV_WIDTH), lambda i, pt: (i, 0)),
        scratch_shapes=[pltpu.VMEM((rows * CMP_PITCH, KV_WIDTH), F32)] * 2,
    )
    return pl.pallas_call(
        functools.partial(_compress_sample_kernel, n_seq, n_pages),
        grid_spec=grid_spec,
        out_shape=jax.ShapeDtypeStruct((db * n_pages * (PAGE_SIZE // CMP_BLOCK), 2 * KV_WIDTH), F32),
        compiler_params=pltpu.CompilerParams(dimension_semantics=("parallel",),
                                             vmem_limit_bytes=VMEM_LIMIT),
        name="compress_sample",
    )(page_table, *([pool] * (n_seq * n_pages)), pe, w1, b1, w2)


def _stack_heads(q, tq):
    lane_lo = lax.broadcasted_iota(jnp.int32, (tq, LANES), 1) < HEAD_DIM
    pieces = []
    for hh in range(NSA_HEADS):
        g = hh // GROUP
        c = hh // 2
        chunk = q[:, c * LANES:(c + 1) * LANES]
        if hh % 2 != g:
            chunk = pltpu.roll(chunk, HEAD_DIM, 1)
        pieces.append(jnp.where(lane_lo if g == 0 else jnp.logical_not(lane_lo), chunk, 0.0))
    return jnp.concatenate(pieces, axis=0).astype(BF16)


def _cmp_branch(qraw, ckv, qpos, tq):
    nb = MAX_SLC_BLOCKS
    kc = jnp.concatenate([ckv[:, 0:KV_WIDTH], ckv[:, 2 * KV_WIDTH:3 * KV_WIDTH]], axis=0).astype(BF16)
    vc = jnp.concatenate([ckv[:, KV_WIDTH:2 * KV_WIDTH], ckv[:, 3 * KV_WIDTH:]], axis=0).astype(BF16)
    s = _dot_nt(qraw, kc).reshape(NSA_HEADS, tq, 2 * nb)
    lane = lax.broadcasted_iota(jnp.int32, (tq, 2 * nb), 1)
    blk = jnp.where(lane < nb, 2 * lane, 2 * (lane - nb) + 1)
    mask = ((blk + 1) * CMP_BLOCK - 1) <= qpos
    p = _softmax_masked(s, mask[None])
    o_cmp = _dot(p.reshape(NSA_HEADS * tq, 2 * nb).astype(BF16), vc)
    scores = []
    sblk = lax.broadcasted_iota(jnp.int32, (tq, nb), 1)
    cur = qpos // SLC_BLOCK
    valid = sblk <= cur
    forced = ((sblk == 0) | (sblk == cur) | (sblk == cur - 1)).astype(F32)
    for g in range(KV_HEADS):
        imp = p[g * GROUP]
        for r in range(1, GROUP):
            imp = imp + p[g * GROUP + r]
        imp = imp[:, :nb] + imp[:, nb:]
        scores.append(jnp.where(valid, imp + FORCE_BONUS * forced, -1.0))
    return o_cmp, scores, valid


def _rank_blocks_wide(score, st_ref, cnt_ref, n_chunks):
    st_ref[...] = score.T
    cnt_ref[...] = jnp.zeros_like(cnt_ref)
    sub = lax.broadcasted_iota(jnp.int32, (SUBLANES, LANES), 0)
    n_vregs = MAX_SLC_BLOCKS // SUBLANES
    for c in range(n_vregs):
        @pl.when(c < n_chunks)
        def _():
            tiles = [st_ref[v * SUBLANES:(v + 1) * SUBLANES, :] for v in range(n_vregs)]
            cnts = [cnt_ref[v * SUBLANES:(v + 1) * SUBLANES, :] for v in range(n_vregs)]
            for ii in range(SUBLANES):
                row = tiles[c][ii:ii + 1, :]
                for v in range(n_vregs):
                    if v < c:
                        inc = jnp.where(row > tiles[v], 1.0, 0.0)
                    elif v > c:
                        inc = jnp.where(row >= tiles[v], 1.0, 0.0)
                    else:
                        inc = jnp.where(sub > ii, jnp.where(row >= tiles[v], 1.0, 0.0),
                                        jnp.where(row > tiles[v], 1.0, 0.0))
                    cnts[v] = cnts[v] + inc
            for v in range(n_vregs):
                cnt_ref[v * SUBLANES:(v + 1) * SUBLANES, :] = cnts[v]
    return cnt_ref[...].T


def _rank_blocks_narrow(score, n_blocks):
    lane = lax.broadcasted_iota(jnp.int32, score.shape, 1)
    cnt = jnp.zeros(score.shape, F32)
    for i in range(n_blocks):
        col = score[:, i:i + 1]
        cnt = cnt + jnp.where(lane > i, jnp.where(col >= score, 1.0, 0.0),
                              jnp.where(col > score, 1.0, 0.0))
    return cnt


def _slc_branch(q_aug, kv_ref, et_ref, n_full, qpos, tq, tk):
    ones_rows = jnp.where(lax.broadcasted_iota(jnp.int32, (HEAD_DIM, tk), 0) == 0, 1.0, 0.0).astype(BF16)

    def tile(kt, carry, diagonal):
        ms, accs = carry
        off = pl.multiple_of(kt * tk, tk)
        ka = jnp.concatenate([et_ref[:, pl.ds(off, tk)], kv_ref[:KV_WIDTH, pl.ds(off, tk)]], axis=0)
        s = _dot(q_aug, ka)
        if diagonal:
            visible = (off + lax.broadcasted_iota(jnp.int32, (tq, tk), 1)) <= qpos
        ps, m_new, alphas = [], [], []
        for hh in range(NSA_HEADS):
            s_h = s[hh * tq:(hh + 1) * tq, :]
            if diagonal:
                s_h = jnp.where(visible, s_h, NEG_INF)
            m_h = jnp.maximum(ms[hh], jnp.max(s_h, axis=-1, keepdims=True))
            ps.append(jnp.exp2(s_h - m_h).astype(BF16))
            alphas.append(jnp.exp2(ms[hh] - m_h))
            m_new.append(m_h)
        acc_new = []
        for g in range(KV_HEADS):
            v0 = KV_WIDTH + g * HEAD_DIM
            va = jnp.concatenate([kv_ref[v0:v0 + HEAD_DIM, pl.ds(off, tk)], ones_rows], axis=0)
            pv = _dot_nt(jnp.concatenate(ps[g * GROUP:(g + 1) * GROUP], axis=0), va)
            alpha = jnp.concatenate(alphas[g * GROUP:(g + 1) * GROUP], axis=0)
            acc_new.append(alpha * accs[g] + pv)
        return tuple(m_new), tuple(acc_new)

    init = (tuple(jnp.full((tq, 1), NEG_INF, F32) for _ in range(NSA_HEADS)),
            tuple(jnp.zeros((GROUP * tq, KV_WIDTH), F32) for _ in range(KV_HEADS)))
    rem = n_full % 4
    carry = lax.cond(rem % 2 == 1, lambda c: tile(0, tile(n_full, c, True), False),
                     lambda c: tile(n_full, c, True), init)
    carry = lax.cond(rem >= 2,
                     lambda c: tile(rem % 2 + 1, tile(rem % 2, c, False), False), lambda c: c, carry)

    def quad(j, c):
        kt = rem + 4 * j
        for u in range(4):
            c = tile(kt + u, c, False)
        return c

    _, accs = lax.fori_loop(0, n_full // 4, quad, carry)
    outs = [a * (1.0 / jnp.maximum(a[:, HEAD_DIM:HEAD_DIM + 1], 1e-30)) for a in accs]
    return jnp.concatenate([outs[0], pltpu.roll(outs[1], HEAD_DIM, 1)], axis=0)


def _softmax_pv(s, mask, vt, tq):
    nk = s.shape[1]
    rows = NSA_HEADS * tq
    if rows <= LANES:
        s3 = jnp.where(mask[None], s.reshape(NSA_HEADS, tq, nk), NEG_INF)
        p = jnp.exp2(s3 - jnp.max(s3, axis=-1, keepdims=True)).reshape(rows, nk).astype(BF16)
        if rows < LANES:
            p = jnp.concatenate([p, jnp.zeros((LANES - rows, nk), BF16)], axis=0)
        ones_rows = jnp.where(lax.broadcasted_iota(jnp.int32, (2 * SUBLANES, nk), 0) == 0,
                              1.0, 0.0).astype(BF16)
        ot = _dot_nt(jnp.concatenate([vt, ones_rows], axis=0), p)
        return (ot[:KV_WIDTH] * (1.0 / jnp.maximum(ot[KV_WIDTH:KV_WIDTH + 1], 1e-30))).T[:rows]
    ones_rows = jnp.where(lax.broadcasted_iota(jnp.int32, (HEAD_DIM, nk), 0) == 0, 1.0, 0.0).astype(BF16)
    outs = []
    for g in range(KV_HEADS):
        ps = []
        for hh in range(g * GROUP, (g + 1) * GROUP):
            s_h = jnp.where(mask, s[hh * tq:(hh + 1) * tq, :], NEG_INF)
            ps.append(jnp.exp2(s_h - jnp.max(s_h, axis=-1, keepdims=True)).astype(BF16))
        va = jnp.concatenate([vt[g * HEAD_DIM:(g + 1) * HEAD_DIM, :], ones_rows], axis=0)
        pv = _dot_nt(jnp.concatenate(ps, axis=0), va)
        outs.append(pv * (1.0 / jnp.maximum(pv[:, HEAD_DIM:HEAD_DIM + 1], 1e-30)))
    return jnp.concatenate([outs[0], pltpu.roll(outs[1], HEAD_DIM, 1)], axis=0)


def _slc_single(bias_rows, qrot, kt, vt, et, qpos, tq):
    nk = kt.shape[1]
    s = _dot(bias_rows, et) + _dot(qrot, kt)
    visible = lax.broadcasted_iota(jnp.int32, (tq, nk), 1) <= qpos
    return _softmax_pv(s, visible, vt, tq)


def _win_branch(qrot, kt, vt, kpos, qpos, tq):
    diff = qpos - kpos
    return _softmax_pv(_dot(qrot, kt), (diff >= 0) & (diff < WINDOW), vt, tq)


def _combine(o_cmp, o_slc, o_win, gates, tq):
    lane_lo = lax.broadcasted_iota(jnp.int32, (tq, LANES), 1) < HEAD_DIM
    heads = []
    for hh in range(NSA_HEADS):
        tot = None
        for br, o in enumerate((o_cmp, o_slc, o_win)):
            col = br * NSA_HEADS + hh
            term = gates[:, col:col + 1] * o[hh * tq:(hh + 1) * tq, :]
            tot = term if tot is None else tot + term
        heads.append(tot)
    chunks = []
    for c in range(NSA_HEADS // 2):
        a, b = heads[2 * c], heads[2 * c + 1]
        if c // 2 == 0:
            chunks.append(jnp.where(lane_lo, a, pltpu.roll(b, HEAD_DIM, 1)))
        else:
            chunks.append(jnp.where(lane_lo, pltpu.roll(a, HEAD_DIM, 1), b))
    return jnp.concatenate(chunks, axis=1)


def _selection_bias(rank, valid):
    return jnp.where((rank < float(N_SELECT)) & valid, 0.0, NEG_INF).astype(BF16)


def _pad_ckv(ckv):
    n = ckv.shape[0]
    if n == MAX_SLC_BLOCKS:
        return ckv
    return jnp.concatenate([ckv, jnp.zeros((MAX_SLC_BLOCKS - n, ckv.shape[1]), F32)], axis=0)


def _nsa_prompt_kernel(qraw_ref, qrot_ref, gate_ref, ckv_ref, slc_ref, win_ref, et_ref,
                       o_ref, st_ref, cnt_ref):
    tq = QUERY_BLOCK
    i = pl.program_id(1)
    qpos0 = i * tq
    qpos = qpos0 + lax.broadcasted_iota(jnp.int32, (tq, 1), 0)
    qraw = _stack_heads(qraw_ref[...].astype(F32), tq)
    qrot = _stack_heads(qrot_ref[...].astype(F32), tq)

    o_cmp, scores, valid = _cmp_branch(qraw, _pad_ckv(ckv_ref[...]), qpos, tq)
    n_chunks = (qpos0 + tq - 1) // (SLC_BLOCK * SUBLANES) + 1
    bias = []
    for g in range(KV_HEADS):
        parts = []
        for t0 in range(0, tq, RANK_TOKENS):
            rank = _rank_blocks_wide(scores[g][t0:t0 + RANK_TOKENS], st_ref, cnt_ref, n_chunks)
            parts.append(_selection_bias(rank, valid[t0:t0 + RANK_TOKENS]))
        bias.append(jnp.concatenate(parts, axis=0))
    bias_rows = jnp.concatenate([bias[0]] * GROUP + [bias[1]] * GROUP, axis=0)
    q_aug = jnp.concatenate([bias_rows, qrot], axis=1)
    o_slc = _slc_branch(q_aug, slc_ref, et_ref, qpos0 // SLC_KEY_TILE, qpos, tq, SLC_KEY_TILE)

    nk = WINDOW + tq
    start = pl.multiple_of(jnp.maximum(qpos0 - WINDOW, 0), tq)
    kpos = start + lax.broadcasted_iota(jnp.int32, (1, nk), 1)
    o_win = _win_branch(qrot, win_ref[:KV_WIDTH, pl.ds(start, nk)], win_ref[KV_WIDTH:, pl.ds(start, nk)],
                        kpos, qpos, tq)

    o_ref[...] = _combine(o_cmp, o_slc, o_win, gate_ref[...], tq).astype(o_ref.dtype)


def _nsa_prompt(qraw, qrot, gates, ckv, slcb, winb, et):
    b, t, _ = qraw.shape
    tq = QUERY_BLOCK
    blk = lambda width: pl.BlockSpec((None, tq, width), lambda bi, i: (bi, i, 0))
    per_batch = lambda a: pl.BlockSpec((None,) + a.shape[1:], lambda bi, i: (bi, 0, 0))
    return pl.pallas_call(
        _nsa_prompt_kernel,
        grid=(b, t // tq),
        in_specs=[blk(NSA_WIDTH), blk(NSA_WIDTH), blk(LANES), per_batch(ckv), per_batch(slcb),
                  per_batch(winb), pl.BlockSpec(et.shape, lambda bi, i: (0, 0))],
        out_specs=blk(NSA_WIDTH),
        out_shape=jax.ShapeDtypeStruct((b, t, NSA_WIDTH), BF16),
        scratch_shapes=[pltpu.VMEM((MAX_SLC_BLOCKS, RANK_TOKENS), F32)] * 2,
        compiler_params=pltpu.CompilerParams(dimension_semantics=("parallel", "arbitrary"),
                                             vmem_limit_bytes=VMEM_LIMIT),
        name="nsa_prompt",
    )(qraw, qrot, gates, ckv, slcb, winb, et)


def _nsa_sample_kernel(n_seq, n_pages, past, s_new, pt_ref, *refs):
    n_in = n_seq * n_pages
    page_refs = refs[:n_in]
    (qraw_ref, qrot_ref, gate_ref, ckv_ref, slcn_ref, wins_ref, winn_ref, et_ref) = refs[n_in:n_in + 8]
    o_ref, wout_ref = refs[-2:]
    tq = SAMPLE_ROWS
    qpos = past + lax.broadcasted_iota(jnp.int32, (tq, 1), 0)
    n_buf = wins_ref.shape[2]
    n_blocks = (past + tq - 1) // SLC_BLOCK + 1
    pad_rows = jnp.zeros((LANES - tq, 2 * KV_WIDTH), F32)
    wpos = past - n_buf + lax.broadcasted_iota(jnp.int32, (1, n_buf + LANES), 1)
    lane = lax.broadcasted_iota(jnp.int32, (2 * KV_WIDTH, LANES), 1)
    for s in range(n_seq):
        slc_new = jnp.concatenate([slcn_ref[s], pad_rows], axis=0).T
        tiles = [page_refs[s * n_pages + p][...].astype(BF16) for p in range(n_pages)]
        tiles.append(slc_new.astype(BF16))
        kt = jnp.concatenate([tl[:KV_WIDTH] for tl in tiles], axis=1)
        vt = jnp.concatenate([tl[KV_WIDTH:] for tl in tiles], axis=1)

        qraw = _stack_heads(qraw_ref[s], tq)
        qrot = _stack_heads(qrot_ref[s], tq)
        o_cmp, scores, valid = _cmp_branch(qraw, _pad_ckv(ckv_ref[s]), qpos, tq)
        bias = [_selection_bias(_rank_blocks_narrow(scores[g], n_blocks), valid)
                for g in range(KV_HEADS)]
        bias_rows = jnp.concatenate([bias[0]] * GROUP + [bias[1]] * GROUP, axis=0)
        o_slc = _slc_single(bias_rows, qrot, kt, vt, et_ref[...], qpos, tq)

        state = wins_ref[s]
        win_new = jnp.concatenate([winn_ref[s], pad_rows], axis=0).T
        kt = jnp.concatenate([state[:KV_WIDTH], win_new[:KV_WIDTH]], axis=1).astype(BF16)
        vt = jnp.concatenate([state[KV_WIDTH:], win_new[KV_WIDTH:]], axis=1).astype(BF16)
        o_win = _win_branch(qrot, kt, vt, wpos, qpos, tq)
        o_ref[s] = _combine(o_cmp, o_slc, o_win, gate_ref[s], tq)

        shifted = pltpu.roll(state, n_buf - s_new, 1)
        tail = jnp.where(lane >= LANES - s_new, pltpu.roll(win_new, LANES - s_new, 1),
                         shifted[:, n_buf - LANES:])
        wout_ref[s, :, :n_buf - LANES] = shifted[:, :n_buf - LANES]
        wout_ref[s, :, n_buf - LANES:] = tail


def _nsa_sample(pool, win_state, layer, page_table, qraw, qrot, gates, ckv, slc_new, win_new, et,
                n_seq, s_new, carried):
    db, n_pages = page_table.shape
    past = n_pages * PAGE_SIZE
    tq = SAMPLE_ROWS
    n_buf = win_state.shape[3]
    seq = lambda a: pl.BlockSpec((n_seq,) + a.shape[1:], lambda i, pt: (i, 0, 0))

    def page_spec(s, p):
        return pl.BlockSpec((None, None) + pool.shape[2:],
                            lambda i, pt: (layer, pt[i * n_seq + s, p], 0, 0))

    state_spec = pl.BlockSpec((None, n_seq) + win_state.shape[2:], lambda i, pt: (layer, i, 0, 0))
    grid_spec = pltpu.PrefetchScalarGridSpec(
        num_scalar_prefetch=1,
        grid=(db // n_seq,),
        in_specs=[page_spec(s, p) for s in range(n_seq) for p in range(n_pages)]
        + [seq(qraw), seq(qrot), seq(gates), seq(ckv), seq(slc_new), state_spec, seq(win_new),
           pl.BlockSpec(et.shape, lambda i, pt: (0, 0))]
        + [pl.BlockSpec(memory_space=pl.ANY)] * len(carried),
        out_specs=[pl.BlockSpec((n_seq, tq, NSA_WIDTH), lambda i, pt: (i, 0, 0)),
                   pl.BlockSpec((None, n_seq) + win_state.shape[2:], lambda i, pt: (layer, i, 0, 0))],
    )
    n_operands = 1 + n_seq * n_pages + 8
    return pl.pallas_call(
        functools.partial(_nsa_sample_kernel, n_seq, n_pages, past, s_new),
        grid_spec=grid_spec,
        out_shape=[jax.ShapeDtypeStruct((db, tq, NSA_WIDTH), F32),
                   jax.ShapeDtypeStruct(win_state.shape, F32)],
        input_output_aliases={n_operands + k: 1 + k for k in range(len(carried))},
        compiler_params=pltpu.CompilerParams(dimension_semantics=("parallel",),
                                             vmem_limit_bytes=VMEM_LIMIT),
        name="nsa_sample",
    )(page_table, *([pool] * (n_seq * n_pages)), qraw, qrot, gates, ckv, slc_new, win_state, win_new, et,
      *carried)


def _post_kernel(chunk, alpha, n_ffn_chunks, x_ref, o_ref, u_ref, v_ref, gw_ref, gb_ref,
                 wout_ref, ln1g_ref, ln1b_ref, wg_ref, wu_ref, wd_ref, ln2g_ref, ln2b_ref, y_ref):
    tm = x_ref.shape[0]
    c = GMLP_CHUNK
    row = lax.broadcasted_iota(jnp.int32, (c, c), 0)
    col = lax.broadcasted_iota(jnp.int32, (c, c), 1)
    causal = (col <= row) & ((row // chunk) == (col // chunk))
    w_cat = jnp.concatenate([jnp.where(causal, gw_ref[h], 0.0) for h in range(GMLP_GROUPS)],
                            axis=1).astype(BF16)
    lane_group = lax.broadcasted_iota(jnp.int32, (c, GMLP_WIDTH), 1) // (GMLP_WIDTH // GMLP_GROUPS)
    mixed = []
    for sub in range(tm // c):
        v = v_ref[sub * c:(sub + 1) * c, :]
        v_diag = jnp.concatenate([jnp.where(lane_group == h, v, 0.0) for h in range(GMLP_GROUPS)],
                                 axis=0).astype(BF16)
        s = _dot(w_cat, v_diag) + gb_ref[...]
        mixed.append(u_ref[sub * c:(sub + 1) * c, :].astype(F32) * s)
    o_gmlp = jnp.concatenate(mixed, axis=0)
    mix = jnp.concatenate([o_ref[...], o_gmlp], axis=1).astype(BF16)
    h = _dot(mix, wout_ref[...])
    x1 = _layer_norm(alpha * x_ref[...] + h, ln1g_ref[...], ln1b_ref[...])
    x1b = x1.astype(BF16)
    f = None
    fc = wg_ref.shape[1] // n_ffn_chunks
    for j in range(n_ffn_chunks):
        gate = _dot(x1b, wg_ref[:, j * fc:(j + 1) * fc])
        up = _dot(x1b, wu_ref[:, j * fc:(j + 1) * fc])
        a = (jax.nn.silu(gate) * up).astype(BF16)
        part = _dot(a, wd_ref[j * fc:(j + 1) * fc, :])
        f = part if f is None else f + part
    y_ref[...] = _layer_norm(alpha * x1 + f, ln2g_ref[...], ln2b_ref[...])


def _post(x, o_nsa, u, v, gw, gb, wout, ln1g, ln1b, wg, wu, wd, ln2g, ln2b, chunk, alpha, tm):
    n, d = x.shape
    ffn = wg.shape[1]
    n_ffn_chunks = next(k for k in (4, 2, 1) if ffn % (k * LANES) == 0)
    row = lambda width: pl.BlockSpec((tm, width), lambda i: (i, 0))
    const = lambda a: pl.BlockSpec(a.shape, lambda i: (0,) * a.ndim, pipeline_mode=pl.Buffered(1))
    consts = (gw, gb, wout, ln1g, ln1b, wg, wu, wd, ln2g, ln2b)
    return pl.pallas_call(
        functools.partial(_post_kernel, chunk, alpha, n_ffn_chunks),
        grid=(n // tm,),
        in_specs=[row(d), row(NSA_WIDTH), row(GMLP_WIDTH), row(GMLP_WIDTH)] + [const(a) for a in consts],
        out_specs=row(d),
        out_shape=jax.ShapeDtypeStruct((n, d), F32),
        compiler_params=pltpu.CompilerParams(dimension_semantics=("parallel",),
                                             vmem_limit_bytes=VMEM_LIMIT),
        name="post",
    )(x, o_nsa, u, v, *consts)


def _pack_w_in_t(w_in):
    wt = w_in.T
    o_gate = NSA_WIDTH + 3 * 2 * KV_WIDTH
    o_gmlp = o_gate + 3 * NSA_HEADS
    gate = jnp.pad(wt[o_gate:o_gmlp], ((0, LANES - 3 * NSA_HEADS), (0, 0)))
    return jnp.concatenate([wt[:o_gate], gate, wt[o_gmlp:]], axis=0).astype(BF16)


def _pack_compress(pe, w1, b1, w2):
    eye = jnp.eye(KV_HEADS, dtype=F32)
    zero = jnp.zeros_like(w1)
    w1p = jnp.stack([jnp.concatenate([w1, zero], axis=-1), jnp.concatenate([zero, w1], axis=-1)], axis=2)
    w1p = w1p.reshape(2, CMP_BLOCK * KV_WIDTH, KV_HEADS * CMP_HIDDEN)
    pep = jnp.broadcast_to(pe[:, :, None, :], (2, CMP_BLOCK, KV_HEADS, HEAD_DIM)).reshape(2, 1, -1)
    b1p = jnp.tile(b1, (1, KV_HEADS))[:, None, :]
    w2p = jnp.einsum('khd,gG->kghGd', w2, eye).reshape(2, KV_HEADS * CMP_HIDDEN, KV_WIDTH)
    return pep, w1p.astype(BF16), b1p, w2p.astype(BF16)


def _pack_gmlp(ws, bs, chunk):
    reps = GMLP_CHUNK // chunk
    gw = jnp.tile(ws[:, :chunk, :chunk], (1, reps, reps))
    gb = jnp.tile(jnp.repeat(bs[:, :chunk].T, GMLP_WIDTH // GMLP_GROUPS, axis=1), (reps, 1))
    return gw, gb


def _rope_tables(pos):
    half = HEAD_DIM // 2
    inv_freq = ROPE_THETA ** (-jnp.arange(half, dtype=F32) / half)
    ang = pos.astype(F32)[:, None] * inv_freq[None, :]
    cos, sin = jnp.cos(ang), jnp.sin(ang)
    return jnp.concatenate([cos, cos], axis=1), jnp.concatenate([-sin, sin], axis=1)


def _block_indicator(n_keys):
    key_block = np.arange(n_keys)[None, :] // SLC_BLOCK
    return jnp.asarray(key_block == np.arange(MAX_SLC_BLOCKS)[:, None], dtype=BF16)


def _feature_major(a):
    lead = a.shape[:-4]
    n = len(lead)
    perm = tuple(range(n)) + (n + 1, n + 2, n + 3, n)
    return jnp.transpose(a, perm).reshape(lead + (2 * KV_WIDTH, a.shape[-4]))


def _position_major(a):
    lead = a.shape[:-2]
    n = len(lead)
    a = a.reshape(lead + (2, KV_HEADS, HEAD_DIM, a.shape[-1]))
    return jnp.transpose(a, tuple(range(n)) + (n + 3, n, n + 1, n + 2))


def kernel(x_prompt, x_sample, cache_cmp_kv, cache_slc_kv, state_win_kv, page_table,
           w_in, cmp_pe, cmp_w1, cmp_b1, cmp_w2, gmlp_ln_g, gmlp_ln_b, gmlp_ws, gmlp_bs,
           w_out, ln1_g, ln1_b, w_gate, w_up, w_down, ln2_g, ln2_b):
    depth = w_in.shape[0]
    b, t, d = x_prompt.shape
    db, s_new, _ = x_sample.shape
    n_pages = page_table.shape[1]
    past = n_pages * PAGE_SIZE
    n_buf = state_win_kv.shape[2]
    kv_tail = (2, KV_HEADS, HEAD_DIM)
    rows_s = SAMPLE_ROWS
    assert t % SLC_KEY_TILE == 0 and t >= WINDOW + QUERY_BLOCK and t // SLC_BLOCK <= MAX_SLC_BLOCKS
    assert s_new <= rows_s and (past + rows_s - 1) // SLC_BLOCK < MAX_SLC_BLOCKS
    assert past % CMP_BLOCK == 0 and past % CMP_BLOCK + s_new < CMP_BLOCK
    assert n_buf % LANES == 0 and n_buf >= LANES
    alpha = (2 * depth) ** 0.25

    tm_p = 512
    tm_s = min(512, db * rows_s)
    seq_per_step = 2 if db % 2 == 0 else 1
    attn_seq_per_step = 4 if db % 4 == 0 else seq_per_step
    cos_p, sin_p = _rope_tables(jnp.arange(t, dtype=jnp.int32))
    cos_s, sin_s = _rope_tables(past + jnp.arange(rows_s, dtype=jnp.int32))
    lane_reps = LANES // HEAD_DIM
    cos_pl, sin_pl = jnp.tile(cos_p, (1, lane_reps)), jnp.tile(sin_p, (1, lane_reps))
    cos_pt, sin_pt = cos_p.T, sin_p.T
    cos_s = jnp.tile(cos_s, (tm_s // rows_s, lane_reps))
    sin_s = jnp.tile(sin_s, (tm_s // rows_s, lane_reps))
    et_p = _block_indicator(t)
    et_s = _block_indicator(past + LANES)
    pool_cmp = _feature_major(cache_cmp_kv)
    pool_slc = _feature_major(cache_slc_kv)
    win_state = _feature_major(state_win_kv)

    yp = x_prompt.reshape(b * t, d)
    ys = jnp.pad(x_sample, ((0, 0), (0, rows_s - s_new), (0, 0))).reshape(db * rows_s, d)
    outs = {k: [] for k in ("s_cmp", "s_slc", "s_v")}
    assert min(WINDOW, t) == tm_p
    new_kv_prompt = ()
    win_next = ()
    cmp_tile = min(128, b * t // CMP_BLOCK)

    for l in range(depth):
        wt = _pack_w_in_t(w_in[l])
        pe_p, w1_p, b1_p, w2_p = _pack_compress(cmp_pe[l], cmp_w1[l], cmp_b1[l], cmp_w2[l])
        ln_g = gmlp_ln_g[l][None, :]
        ln_b = gmlp_ln_b[l][None, :]
        tail = (w_out[l].astype(BF16), ln1_g[l][None, :], ln1_b[l][None, :], w_gate[l].astype(BF16),
                w_up[l].astype(BF16), w_down[l].astype(BF16), ln2_g[l][None, :], ln2_b[l][None, :])

        qraw, qrot, gates, u, v, cmp_kv, *new_kv_prompt, slcb, winb = _inproj_prompt(
            yp, wt, cos_pl, sin_pl, cos_pt, sin_pt, ln_g, ln_b, tm_p, b, t, depth, l, new_kv_prompt)
        ckv = _compress_prompt(cmp_kv.reshape(b * t // CMP_BLOCK, CMP_BLOCK * 2 * KV_WIDTH),
                               pe_p, w1_p, b1_p, w2_p, cmp_tile)
        r3 = lambda a: a.reshape(b, t, a.shape[-1])
        o_nsa = _nsa_prompt(r3(qraw), r3(qrot), r3(gates), ckv.reshape(b, t // SLC_BLOCK, 4 * KV_WIDTH),
                            slcb, winb, et_p)
        gw, gb = _pack_gmlp(gmlp_ws[l], gmlp_bs[l], GMLP_CHUNK)
        yp = _post(yp, o_nsa.reshape(b * t, NSA_WIDTH), u, v, gw, gb, *tail, GMLP_CHUNK, alpha, tm_p)

        qraw, qrot, gates, u, v, cmp_kv, slc_kv, win_kv = _inproj_sample(
            ys, wt, cos_s, sin_s, ln_g, ln_b, tm_s)
        ckv = _compress_sample(pool_cmp, l, page_table, pe_p, w1_p, b1_p, w2_p, attn_seq_per_step)
        s3 = lambda a: a.reshape(db, rows_s, a.shape[-1])
        o_nsa, *win_next = _nsa_sample(pool_slc, win_state, l, page_table, s3(qraw), s3(qrot), s3(gates),
                                       ckv.reshape(db, past // SLC_BLOCK, 4 * KV_WIDTH),
                                       s3(slc_kv), s3(win_kv), et_s, attn_seq_per_step, s_new, win_next)
        gw, gb = _pack_gmlp(gmlp_ws[l], gmlp_bs[l], rows_s)
        ys = _post(ys, o_nsa.reshape(db * rows_s, NSA_WIDTH), u, v, gw, gb, *tail, rows_s, alpha, tm_s)
        new = lambda a: a.reshape((db, rows_s) + a.shape[1:])[:, :s_new]
        outs["s_cmp"].append(new(cmp_kv).reshape((db, s_new) + kv_tail))
        outs["s_slc"].append(new(slc_kv).reshape((db, s_new) + kv_tail))
        outs["s_v"].append(new(v))

    y_sample = ys.reshape(db, rows_s, d)[:, :s_new]
    p_cmp, p_slc, p_win = (_position_major(a) for a in new_kv_prompt)
    return (yp.reshape(b, t, d), y_sample, p_cmp, p_slc, p_win,
            jnp.stack(outs["s_cmp"]), jnp.stack(outs["s_slc"]),
            _position_major(win_next[0]), jnp.stack(outs["s_v"]))
```

```python
import functools
import math

import numpy as np
import jax
import jax.numpy as jnp
from jax import lax
from jax.experimental import pallas as pl
from jax.experimental.pallas import tpu as pltpu

F32 = jnp.float32
BF16 = jnp.bfloat16

LANES = 128
SUBLANES = 8
HEAD_DIM = 64
NSA_HEADS = 8
KV_HEADS = 2
GROUP = NSA_HEADS // KV_HEADS
NSA_WIDTH = NSA_HEADS * HEAD_DIM
KV_WIDTH = KV_HEADS * HEAD_DIM
GMLP_WIDTH = 512
GMLP_GROUPS = 8
GMLP_CHUNK = 128
CMP_BLOCK = 32
CMP_HIDDEN = 256
CMP_PITCH = CMP_BLOCK + SUBLANES
SLC_BLOCK = 64
N_SELECT = 16
WINDOW = 512
QUERY_BLOCK = 128
RANK_TOKENS = LANES
PAGE_SIZE = 128
FORCE_BONUS = 1.0e4
ROPE_THETA = 10000.0
LN_EPS = 1e-5
NEG_INF = -1e30
MAX_SLC_BLOCKS = LANES
SAMPLE_ROWS = SUBLANES
SLC_KEY_TILE = 512
VMEM_LIMIT = 56 * 1024 * 1024
Q_SCALE = HEAD_DIM ** -0.5 * math.log2(math.e)

C_Q = 0
C_CMP = C_Q + NSA_WIDTH
C_SLC = C_CMP + 2 * KV_WIDTH
C_WIN = C_SLC + 2 * KV_WIDTH
C_GATE = C_WIN + 2 * KV_WIDTH
C_U = C_GATE + LANES
C_V = C_U + GMLP_WIDTH
C_END = C_V + GMLP_WIDTH


def _dot(a, b):
    return jnp.dot(a, b, preferred_element_type=F32)


def _dot_nt(a, b):
    return lax.dot_general(a, b, (((1,), (1,)), ((), ())), preferred_element_type=F32)


def _layer_norm(x, g, b):
    mu = jnp.mean(x, axis=-1, keepdims=True)
    xc = x - mu
    var = jnp.mean(xc * xc, axis=-1, keepdims=True)
    return xc * lax.rsqrt(var + LN_EPS) * g + b


def _softmax_masked(s, mask):
    s = jnp.where(mask, s, NEG_INF)
    m = jnp.max(s, axis=-1, keepdims=True)
    e = jnp.where(mask, jnp.exp2(s - m), 0.0)
    return e * (1.0 / jnp.maximum(jnp.sum(e, axis=-1, keepdims=True), 1e-30))


def _rope_lanes(c, cos, sin, first_half):
    partner = jnp.where(first_half, pltpu.roll(c, LANES - HEAD_DIM // 2, 1),
                        pltpu.roll(c, HEAD_DIM // 2, 1))
    return c * cos + partner * sin


def _inproj_common(xb, wt_ref, cos, sin, first_half, lng_ref, lnb_ref,
                   qraw_ref, qrot_ref, gate_ref, u_ref, v_ref):
    zq = _dot_nt(xb, wt_ref[C_Q:C_CMP, :])
    qraw_ref[...] = (zq * Q_SCALE).astype(qraw_ref.dtype)
    for c in range(NSA_WIDTH // LANES):
        qrot_ref[:, c * LANES:(c + 1) * LANES] = (_rope_lanes(
            zq[:, c * LANES:(c + 1) * LANES], cos, sin, first_half) * Q_SCALE).astype(qrot_ref.dtype)
    gate_ref[...] = jax.nn.sigmoid(_dot_nt(xb, wt_ref[C_GATE:C_U, :]))
    u_ref[...] = jax.nn.gelu(_dot_nt(xb, wt_ref[C_U:C_V, :])).astype(u_ref.dtype)
    zv = jax.nn.gelu(_dot_nt(xb, wt_ref[C_V:C_END, :]))
    v_ref[...] = _layer_norm(zv, lng_ref[...], lnb_ref[...]).astype(v_ref.dtype)


def _inproj_prompt_kernel(n_carried, x_ref, wt_ref, cos_ref, sin_ref, cost_ref, sint_ref, lng_ref,
                          lnb_ref, *refs):
    (qraw_ref, qrot_ref, gate_ref, u_ref, v_ref,
     cmp_ref, cmpt_ref, slct_ref, wint_ref, slcb_ref, winb_ref) = refs[n_carried:]
    xb = x_ref[...].astype(BF16)
    cos = cos_ref[...]
    lane = lax.broadcasted_iota(jnp.int32, cos.shape, 1)
    first_half = (lane % HEAD_DIM) < (HEAD_DIM // 2)
    _inproj_common(xb, wt_ref, cos, sin_ref[...], first_half, lng_ref, lnb_ref,
                   qraw_ref, qrot_ref, gate_ref, u_ref, v_ref)
    cmp_ref[...] = _dot_nt(xb, wt_ref[C_CMP:C_SLC, :])
    cmpt_ref[...] = _dot_nt(wt_ref[C_CMP:C_SLC, :], xb)
    cost = cost_ref[...]
    sint = sint_ref[...]
    half = HEAD_DIM // 2
    for row0, f_ref, b_ref in ((C_SLC, slct_ref, slcb_ref), (C_WIN, wint_ref, winb_ref)):
        zt = _dot_nt(wt_ref[row0:row0 + 2 * KV_WIDTH, :], xb)
        for g in range(KV_HEADS):
            kg = zt[g * HEAD_DIM:(g + 1) * HEAD_DIM, :]
            partner = jnp.concatenate([kg[half:, :], kg[:half, :]], axis=0)
            kr = kg * cost + partner * sint
            f_ref[g * HEAD_DIM:(g + 1) * HEAD_DIM, :] = kr
            b_ref[g * HEAD_DIM:(g + 1) * HEAD_DIM, :] = kr.astype(BF16)
        f_ref[KV_WIDTH:, :] = zt[KV_WIDTH:, :]
        b_ref[KV_WIDTH:, :] = zt[KV_WIDTH:, :].astype(BF16)


def _inproj_sample_kernel(x_ref, wt_ref, cos_ref, sin_ref, lng_ref, lnb_ref,
                          qraw_ref, qrot_ref, gate_ref, u_ref, v_ref, cmp_ref, slc_ref, win_ref):
    xb = x_ref[...].astype(BF16)
    cos = cos_ref[...]
    sin = sin_ref[...]
    lane = lax.broadcasted_iota(jnp.int32, cos.shape, 1)
    first_half = (lane % HEAD_DIM) < (HEAD_DIM // 2)
    _inproj_common(xb, wt_ref, cos, sin, first_half, lng_ref, lnb_ref,
                   qraw_ref, qrot_ref, gate_ref, u_ref, v_ref)
    cmp_ref[...] = _dot_nt(xb, wt_ref[C_CMP:C_SLC, :])
    for row0, f_ref in ((C_SLC, slc_ref), (C_WIN, win_ref)):
        z = _dot_nt(xb, wt_ref[row0:row0 + 2 * KV_WIDTH, :])
        f_ref[:, :KV_WIDTH] = _rope_lanes(z[:, :KV_WIDTH], cos, sin, first_half)
        f_ref[:, KV_WIDTH:] = z[:, KV_WIDTH:]


def _inproj_prompt(x, wt, cos, sin, cost, sint, ln_g, ln_b, tm, b, t, depth, layer, carried):
    n, d = x.shape
    n_tab = t // tm
    row = lambda width: pl.BlockSpec((tm, width), lambda i: (i, 0))
    const = lambda a: pl.BlockSpec(a.shape, lambda i: (0,) * a.ndim)
    tab = pl.BlockSpec((tm, LANES), lambda i: (i % n_tab, 0))
    tabt = pl.BlockSpec((HEAD_DIM, tm), lambda i: (0, i % n_tab))
    fm = pl.BlockSpec((None, 2 * KV_WIDTH, tm), lambda i: (i // n_tab, 0, i % n_tab))
    fm_layer = pl.BlockSpec((None, None, 2 * KV_WIDTH, tm), lambda i: (layer, i // n_tab, 0, i % n_tab))
    keep_layer = pl.BlockSpec((None, None, 2 * KV_WIDTH, tm), lambda i: (layer, i // n_tab, 0, 0))
    row_out = ((NSA_WIDTH, BF16), (NSA_WIDTH, BF16), (LANES, F32), (GMLP_WIDTH, BF16), (GMLP_WIDTH, BF16),
               (2 * KV_WIDTH, F32))
    n_in = 8
    n_row = len(row_out)
    stacked = lambda width: jax.ShapeDtypeStruct((depth, b, 2 * KV_WIDTH, width), F32)
    return pl.pallas_call(
        functools.partial(_inproj_prompt_kernel, len(carried)),
        grid=(n // tm,),
        in_specs=[row(d), const(wt), tab, tab, tabt, tabt, const(ln_g), const(ln_b)]
        + [pl.BlockSpec(memory_space=pl.ANY)] * len(carried),
        out_specs=[row(wd) for wd, _ in row_out] + [fm_layer, fm_layer, keep_layer, fm, fm],
        out_shape=[jax.ShapeDtypeStruct((n, wd), dt) for wd, dt in row_out]
        + [stacked(t), stacked(t), stacked(tm)]
        + [jax.ShapeDtypeStruct((b, 2 * KV_WIDTH, t), BF16)] * 2,
        input_output_aliases={n_in + k: n_row + k for k in range(len(carried))},
        compiler_params=pltpu.CompilerParams(dimension_semantics=("arbitrary",),
                                             vmem_limit_bytes=VMEM_LIMIT),
        name="inproj_prompt",
    )(x, wt, cos, sin, cost, sint, ln_g, ln_b, *carried)


def _inproj_sample(x, wt, cos, sin, ln_g, ln_b, tm):
    n, d = x.shape
    row = lambda width: pl.BlockSpec((tm, width), lambda i: (i, 0))
    const = lambda a: pl.BlockSpec(a.shape, lambda i: (0,) * a.ndim)
    widths = (NSA_WIDTH, NSA_WIDTH, LANES, GMLP_WIDTH, GMLP_WIDTH,
              2 * KV_WIDTH, 2 * KV_WIDTH, 2 * KV_WIDTH)
    return pl.pallas_call(
        _inproj_sample_kernel,
        grid=(n // tm,),
        in_specs=[row(d), const(wt), const(cos), const(sin), const(ln_g), const(ln_b)],
        out_specs=[row(wd) for wd in widths],
        out_shape=[jax.ShapeDtypeStruct((n, wd), F32) for wd in widths],
        compiler_params=pltpu.CompilerParams(dimension_semantics=("parallel",),
                                             vmem_limit_bytes=VMEM_LIMIT),
        name="inproj_sample",
    )(x, wt, cos, sin, ln_g, ln_b)


def _compress_rows(chunk, pe_ref, w1_ref, b1_ref, w2_ref):
    outs = []
    for k in range(2):
        a = jnp.concatenate([chunk(l, k) for l in range(CMP_BLOCK)], axis=1)
        a = (a + pe_ref[k]).astype(BF16)
        h = jax.nn.silu(_dot(a, w1_ref[k]) + b1_ref[k])
        outs.append(_dot(h.astype(BF16), w2_ref[k]))
    return jnp.concatenate(outs, axis=1)


def _compress_prompt_kernel(x_ref, pe_ref, w1_ref, b1_ref, w2_ref, o_ref):
    row_w = 2 * KV_WIDTH
    chunk = lambda l, k: x_ref[:, l * row_w + k * KV_WIDTH:l * row_w + (k + 1) * KV_WIDTH]
    o_ref[...] = _compress_rows(chunk, pe_ref, w1_ref, b1_ref, w2_ref)


def _compress_prompt(x, pe, w1, b1, w2, tm):
    n = x.shape[0]
    const = lambda a: pl.BlockSpec(a.shape, lambda i: (0,) * a.ndim)
    return pl.pallas_call(
        _compress_prompt_kernel,
        grid=(n // tm,),
        in_specs=[pl.BlockSpec((tm, x.shape[1]), lambda i: (i, 0)),
                  const(pe), const(w1), const(b1), const(w2)],
        out_specs=pl.BlockSpec((tm, 2 * KV_WIDTH), lambda i: (i, 0)),
        out_shape=jax.ShapeDtypeStruct((n, 2 * KV_WIDTH), F32),
        compiler_params=pltpu.CompilerParams(dimension_semantics=("parallel",),
                                             vmem_limit_bytes=VMEM_LIMIT),
        name="compress_prompt",
    )(x, pe, w1, b1, w2)


def _compress_sample_kernel(n_seq, n_pages, pt_ref, *refs):
    page_refs = refs[:n_seq * n_pages]
    pe_ref, w1_ref, b1_ref, w2_ref, o_ref, posk_ref, posv_ref = refs[n_seq * n_pages:]
    blocks_per_page = PAGE_SIZE // CMP_BLOCK
    for j, page_ref in enumerate(page_refs):
        for dst, rows in ((posk_ref, page_ref[:KV_WIDTH, :].T), (posv_ref, page_ref[KV_WIDTH:, :].T)):
            for n in range(blocks_per_page):
                r0 = (j * blocks_per_page + n) * CMP_PITCH
                dst[r0:r0 + CMP_BLOCK, :] = rows[n * CMP_BLOCK:(n + 1) * CMP_BLOCK, :]
    m = n_seq * n_pages * blocks_per_page
    chunk = lambda l, k: (posk_ref, posv_ref)[k][pl.ds(l, m, stride=CMP_PITCH), :]
    o_ref[...] = _compress_rows(chunk, pe_ref, w1_ref, b1_ref, w2_ref)


def _compress_sample(pool, layer, page_table, pe, w1, b1, w2, n_seq):
    db, n_pages = page_table.shape
    rows = n_seq * n_pages * (PAGE_SIZE // CMP_BLOCK)
    const = lambda a: pl.BlockSpec(a.shape, lambda i, pt: (0,) * a.ndim)

    def page_spec(s, p):
        return pl.BlockSpec((None, None) + pool.shape[2:],
                            lambda i, pt: (layer, pt[i * n_seq + s, p], 0, 0))

    grid_spec = pltpu.PrefetchScalarGridSpec(
        num_scalar_prefetch=1,
        grid=(db // n_seq,),
        in_specs=[page_spec(s, p) for s in range(n_seq) for p in range(n_pages)]
        + [const(pe), const(w1), const(b1), const(w2)],
        out_specs=pl.BlockSpec((rows, 2 * KV_WIDTH), lambda i, pt: (i, 0)),
        scratch_shapes=[pltpu.VMEM((rows * CMP_PITCH, KV_WIDTH), F32)] * 2,
    )
    return pl.pallas_call(
        functools.partial(_compress_sample_kernel, n_seq, n_pages),
        grid_spec=grid_spec,
        out_shape=jax.ShapeDtypeStruct((db * n_pages * (PAGE_SIZE // CMP_BLOCK), 2 * KV_WIDTH), F32),
        compiler_params=pltpu.CompilerParams(dimension_semantics=("parallel",),
                                             vmem_limit_bytes=VMEM_LIMIT),
        name="compress_sample",
    )(page_table, *([pool] * (n_seq * n_pages)), pe, w1, b1, w2)


def _stack_heads(q, tq):
    lane_lo = lax.broadcasted_iota(jnp.int32, (tq, LANES), 1) < HEAD_DIM
    pieces = []
    for hh in range(NSA_HEADS):
        g = hh // GROUP
        c = hh // 2
        chunk = q[:, c * LANES:(c + 1) * LANES]
        if hh % 2 != g:
            chunk = pltpu.roll(chunk, HEAD_DIM, 1)
        pieces.append(jnp.where(lane_lo if g == 0 else jnp.logical_not(lane_lo), chunk, 0.0))
    return jnp.concatenate(pieces, axis=0).astype(BF16)


def _cmp_branch(qraw, ckv, qpos, tq):
    nb = MAX_SLC_BLOCKS
    kc = jnp.concatenate([ckv[:, 0:KV_WIDTH], ckv[:, 2 * KV_WIDTH:3 * KV_WIDTH]], axis=0).astype(BF16)
    vc = jnp.concatenate([ckv[:, KV_WIDTH:2 * KV_WIDTH], ckv[:, 3 * KV_WIDTH:]], axis=0).astype(BF16)
    s = _dot_nt(qraw, kc).reshape(NSA_HEADS, tq, 2 * nb)
    lane = lax.broadcasted_iota(jnp.int32, (tq, 2 * nb), 1)
    blk = jnp.where(lane < nb, 2 * lane, 2 * (lane - nb) + 1)
    mask = ((blk + 1) * CMP_BLOCK - 1) <= qpos
    p = _softmax_masked(s, mask[None])
    o_cmp = _dot(p.reshape(NSA_HEADS * tq, 2 * nb).astype(BF16), vc)
    scores = []
    sblk = lax.broadcasted_iota(jnp.int32, (tq, nb), 1)
    cur = qpos // SLC_BLOCK
    valid = sblk <= cur
    forced = ((sblk == 0) | (sblk == cur) | (sblk == cur - 1)).astype(F32)
    for g in range(KV_HEADS):
        imp = p[g * GROUP]
        for r in range(1, GROUP):
            imp = imp + p[g * GROUP + r]
        imp = imp[:, :nb] + imp[:, nb:]
        scores.append(jnp.where(valid, imp + FORCE_BONUS * forced, -1.0))
    return o_cmp, scores, valid


def _rank_blocks_wide(score, st_ref, cnt_ref, n_chunks):
    st_ref[...] = score.T
    cnt_ref[...] = jnp.zeros_like(cnt_ref)
    sub = lax.broadcasted_iota(jnp.int32, (SUBLANES, LANES), 0)
    n_vregs = MAX_SLC_BLOCKS // SUBLANES
    for c in range(n_vregs):
        @pl.when(c < n_chunks)
        def _():
            tiles = [st_ref[v * SUBLANES:(v + 1) * SUBLANES, :] for v in range(n_vregs)]
            cnts = [cnt_ref[v * SUBLANES:(v + 1) * SUBLANES, :] for v in range(n_vregs)]
            for ii in range(SUBLANES):
                row = tiles[c][ii:ii + 1, :]
                for v in range(n_vregs):
                    if v < c:
                        inc = jnp.where(row > tiles[v], 1.0, 0.0)
                    elif v > c:
                        inc = jnp.where(row >= tiles[v], 1.0, 0.0)
                    else:
                        inc = jnp.where(sub > ii, jnp.where(row >= tiles[v], 1.0, 0.0),
                                        jnp.where(row > tiles[v], 1.0, 0.0))
                    cnts[v] = cnts[v] + inc
            for v in range(n_vregs):
                cnt_ref[v * SUBLANES:(v + 1) * SUBLANES, :] = cnts[v]
    return cnt_ref[...].T


def _rank_blocks_narrow(score, n_blocks):
    lane = lax.broadcasted_iota(jnp.int32, score.shape, 1)
    cnt = jnp.zeros(score.shape, F32)
    for i in range(n_blocks):
        col = score[:, i:i + 1]
        cnt = cnt + jnp.where(lane > i, jnp.where(col >= score, 1.0, 0.0),
                              jnp.where(col > score, 1.0, 0.0))
    return cnt


def _slc_branch(q_aug, kv_ref, et_ref, n_full, qpos, tq, tk):
    ones_rows = jnp.where(lax.broadcasted_iota(jnp.int32, (HEAD_DIM, tk), 0) == 0, 1.0, 0.0).astype(BF16)

    def tile(kt, carry, diagonal):
        ms, accs = carry
        off = pl.multiple_of(kt * tk, tk)
        ka = jnp.concatenate([et_ref[:, pl.ds(off, tk)], kv_ref[:KV_WIDTH, pl.ds(off, tk)]], axis=0)
        s = _dot(q_aug, ka)
        if diagonal:
            visible = (off + lax.broadcasted_iota(jnp.int32, (tq, tk), 1)) <= qpos
        ps, m_new, alphas = [], [], []
        for hh in range(NSA_HEADS):
            s_h = s[hh * tq:(hh + 1) * tq, :]
            if diagonal:
                s_h = jnp.where(visible, s_h, NEG_INF)
            m_h = jnp.maximum(ms[hh], jnp.max(s_h, axis=-1, keepdims=True))
            ps.append(jnp.exp2(s_h - m_h).astype(BF16))
            alphas.append(jnp.exp2(ms[hh] - m_h))
            m_new.append(m_h)
        acc_new = []
        for g in range(KV_HEADS):
            v0 = KV_WIDTH + g * HEAD_DIM
            va = jnp.concatenate([kv_ref[v0:v0 + HEAD_DIM, pl.ds(off, tk)], ones_rows], axis=0)
            pv = _dot_nt(jnp.concatenate(ps[g * GROUP:(g + 1) * GROUP], axis=0), va)
            alpha = jnp.concatenate(alphas[g * GROUP:(g + 1) * GROUP], axis=0)
            acc_new.append(alpha * accs[g] + pv)
        return tuple(m_new), tuple(acc_new)

    init = (tuple(jnp.full((tq, 1), NEG_INF, F32) for _ in range(NSA_HEADS)),
            tuple(jnp.zeros((GROUP * tq, KV_WIDTH), F32) for _ in range(KV_HEADS)))
    def first_group(count):
        def run(c):
            c = tile(n_full, c, True)
            for u in range(count - 1):
                c = tile(u, c, False)
            return c
        return run

    first = (n_full + 1) % 4
    carry = lax.switch(first, [first_group(4), first_group(1), first_group(2), first_group(3)], init)
    taken = jnp.where(first == 0, 3, first - 1)

    def quad(j, c):
        kt = taken + 4 * j
        for u in range(4):
            c = tile(kt + u, c, False)
        return c

    _, accs = lax.fori_loop(0, (n_full - taken) // 4, quad, carry)
    outs = [a * (1.0 / jnp.maximum(a[:, HEAD_DIM:HEAD_DIM + 1], 1e-30)) for a in accs]
    return jnp.concatenate([outs[0], pltpu.roll(outs[1], HEAD_DIM, 1)], axis=0)


def _softmax_pv(s, mask, vt, tq):
    nk = s.shape[1]
    rows = NSA_HEADS * tq
    if rows <= LANES:
        s3 = jnp.where(mask[None], s.reshape(NSA_HEADS, tq, nk), NEG_INF)
        p = jnp.exp2(s3 - jnp.max(s3, axis=-1, keepdims=True)).reshape(rows, nk).astype(BF16)
        if rows < LANES:
            p = jnp.concatenate([p, jnp.zeros((LANES - rows, nk), BF16)], axis=0)
        ones_rows = jnp.where(lax.broadcasted_iota(jnp.int32, (2 * SUBLANES, nk), 0) == 0,
                              1.0, 0.0).astype(BF16)
        ot = _dot_nt(jnp.concatenate([vt, ones_rows], axis=0), p)
        return (ot[:KV_WIDTH] * (1.0 / jnp.maximum(ot[KV_WIDTH:KV_WIDTH + 1], 1e-30))).T[:rows]
    ones_rows = jnp.where(lax.broadcasted_iota(jnp.int32, (HEAD_DIM, nk), 0) == 0, 1.0, 0.0).astype(BF16)
    outs = []
    for g in range(KV_HEADS):
        ps = []
        for hh in range(g * GROUP, (g + 1) * GROUP):
            s_h = jnp.where(mask, s[hh * tq:(hh + 1) * tq, :], NEG_INF)
            ps.append(jnp.exp2(s_h - jnp.max(s_h, axis=-1, keepdims=True)).astype(BF16))
        va = jnp.concatenate([vt[g * HEAD_DIM:(g + 1) * HEAD_DIM, :], ones_rows], axis=0)
        pv = _dot_nt(jnp.concatenate(ps, axis=0), va)
        outs.append(pv * (1.0 / jnp.maximum(pv[:, HEAD_DIM:HEAD_DIM + 1], 1e-30)))
    return jnp.concatenate([outs[0], pltpu.roll(outs[1], HEAD_DIM, 1)], axis=0)


def _slc_single(bias_rows, qrot, kt, vt, et, qpos, tq):
    nk = kt.shape[1]
    s = _dot(bias_rows, et) + _dot(qrot, kt)
    visible = lax.broadcasted_iota(jnp.int32, (tq, nk), 1) <= qpos
    return _softmax_pv(s, visible, vt, tq)


def _win_branch(qrot, kt, vt, kpos, qpos, tq):
    diff = qpos - kpos
    return _softmax_pv(_dot(qrot, kt), (diff >= 0) & (diff < WINDOW), vt, tq)


def _combine(o_cmp, o_slc, o_win, gates, tq):
    lane_lo = lax.broadcasted_iota(jnp.int32, (tq, LANES), 1) < HEAD_DIM
    heads = []
    for hh in range(NSA_HEADS):
        tot = None
        for br, o in enumerate((o_cmp, o_slc, o_win)):
            col = br * NSA_HEADS + hh
            term = gates[:, col:col + 1] * o[hh * tq:(hh + 1) * tq, :]
            tot = term if tot is None else tot + term
        heads.append(tot)
    chunks = []
    for c in range(NSA_HEADS // 2):
        a, b = heads[2 * c], heads[2 * c + 1]
        if c // 2 == 0:
            chunks.append(jnp.where(lane_lo, a, pltpu.roll(b, HEAD_DIM, 1)))
        else:
            chunks.append(jnp.where(lane_lo, pltpu.roll(a, HEAD_DIM, 1), b))
    return jnp.concatenate(chunks, axis=1)


def _selection_bias(rank, valid):
    return jnp.where((rank < float(N_SELECT)) & valid, 0.0, NEG_INF).astype(BF16)


def _pad_ckv(ckv):
    n = ckv.shape[0]
    if n == MAX_SLC_BLOCKS:
        return ckv
    return jnp.concatenate([ckv, jnp.zeros((MAX_SLC_BLOCKS - n, ckv.shape[1]), F32)], axis=0)


def _nsa_prompt_kernel(qraw_ref, qrot_ref, gate_ref, ckv_ref, slc_ref, win_ref, et_ref,
                       o_ref, st_ref, cnt_ref):
    tq = QUERY_BLOCK
    i = pl.program_id(1)
    qpos0 = i * tq
    qpos = qpos0 + lax.broadcasted_iota(jnp.int32, (tq, 1), 0)
    qraw = _stack_heads(qraw_ref[...].astype(F32), tq)
    qrot = _stack_heads(qrot_ref[...].astype(F32), tq)

    o_cmp, scores, valid = _cmp_branch(qraw, _pad_ckv(ckv_ref[...]), qpos, tq)
    n_chunks = (qpos0 + tq - 1) // (SLC_BLOCK * SUBLANES) + 1
    bias = []
    for g in range(KV_HEADS):
        parts = []
        for t0 in range(0, tq, RANK_TOKENS):
            rank = _rank_blocks_wide(scores[g][t0:t0 + RANK_TOKENS], st_ref, cnt_ref, n_chunks)
            parts.append(_selection_bias(rank, valid[t0:t0 + RANK_TOKENS]))
        bias.append(jnp.concatenate(parts, axis=0))
    bias_rows = jnp.concatenate([bias[0]] * GROUP + [bias[1]] * GROUP, axis=0)
    q_aug = jnp.concatenate([bias_rows, qrot], axis=1)
    o_slc = _slc_branch(q_aug, slc_ref, et_ref, qpos0 // SLC_KEY_TILE, qpos, tq, SLC_KEY_TILE)

    nk = WINDOW + tq
    start = pl.multiple_of(jnp.maximum(qpos0 - WINDOW, 0), tq)
    kpos = start + lax.broadcasted_iota(jnp.int32, (1, nk), 1)
    o_win = _win_branch(qrot, win_ref[:KV_WIDTH, pl.ds(start, nk)], win_ref[KV_WIDTH:, pl.ds(start, nk)],
                        kpos, qpos, tq)

    o_ref[...] = _combine(o_cmp, o_slc, o_win, gate_ref[...], tq).astype(o_ref.dtype)


def _nsa_prompt(qraw, qrot, gates, ckv, slcb, winb, et):
    b, t, _ = qraw.shape
    tq = QUERY_BLOCK
    blk = lambda width: pl.BlockSpec((None, tq, width), lambda bi, i: (bi, i, 0))
    per_batch = lambda a: pl.BlockSpec((None,) + a.shape[1:], lambda bi, i: (bi, 0, 0))
    return pl.pallas_call(
        _nsa_prompt_kernel,
        grid=(b, t // tq),
        in_specs=[blk(NSA_WIDTH), blk(NSA_WIDTH), blk(LANES), per_batch(ckv), per_batch(slcb),
                  per_batch(winb), pl.BlockSpec(et.shape, lambda bi, i: (0, 0))],
        out_specs=blk(NSA_WIDTH),
        out_shape=jax.ShapeDtypeStruct((b, t, NSA_WIDTH), BF16),
        scratch_shapes=[pltpu.VMEM((MAX_SLC_BLOCKS, RANK_TOKENS), F32)] * 2,
        compiler_params=pltpu.CompilerParams(dimension_semantics=("parallel", "arbitrary"),
                                             vmem_limit_bytes=VMEM_LIMIT),
        name="nsa_prompt",
    )(qraw, qrot, gates, ckv, slcb, winb, et)


def _nsa_sample_kernel(n_seq, n_pages, past, s_new, pt_ref, *refs):
    n_in = n_seq * n_pages
    page_refs = refs[:n_in]
    (qraw_ref, qrot_ref, gate_ref, ckv_ref, slcn_ref, wins_ref, winn_ref, et_ref) = refs[n_in:n_in + 8]
    o_ref, wout_ref = refs[-2:]
    tq = SAMPLE_ROWS
    qpos = past + lax.broadcasted_iota(jnp.int32, (tq, 1), 0)
    n_buf = wins_ref.shape[2]
    n_blocks = (past + tq - 1) // SLC_BLOCK + 1
    pad_rows = jnp.zeros((LANES - tq, 2 * KV_WIDTH), F32)
    wpos = past - n_buf + lax.broadcasted_iota(jnp.int32, (1, n_buf + LANES), 1)
    lane = lax.broadcasted_iota(jnp.int32, (2 * KV_WIDTH, LANES), 1)
    for s in range(n_seq):
        slc_new = jnp.concatenate([slcn_ref[s], pad_rows], axis=0).T
        tiles = [page_refs[s * n_pages + p][...].astype(BF16) for p in range(n_pages)]
        tiles.append(slc_new.astype(BF16))
        kt = jnp.concatenate([tl[:KV_WIDTH] for tl in tiles], axis=1)
        vt = jnp.concatenate([tl[KV_WIDTH:] for tl in tiles], axis=1)

        qraw = _stack_heads(qraw_ref[s], tq)
        qrot = _stack_heads(qrot_ref[s], tq)
        o_cmp, scores, valid = _cmp_branch(qraw, _pad_ckv(ckv_ref[s]), qpos, tq)
        bias = [_selection_bias(_rank_blocks_narrow(scores[g], n_blocks), valid)
                for g in range(KV_HEADS)]
        bias_rows = jnp.concatenate([bias[0]] * GROUP + [bias[1]] * GROUP, axis=0)
        o_slc = _slc_single(bias_rows, qrot, kt, vt, et_ref[...], qpos, tq)

        state = wins_ref[s]
        win_new = jnp.concatenate([winn_ref[s], pad_rows], axis=0).T
        kt = jnp.concatenate([state[:KV_WIDTH], win_new[:KV_WIDTH]], axis=1).astype(BF16)
        vt = jnp.concatenate([state[KV_WIDTH:], win_new[KV_WIDTH:]], axis=1).astype(BF16)
        o_win = _win_branch(qrot, kt, vt, wpos, qpos, tq)
        o_ref[s] = _combine(o_cmp, o_slc, o_win, gate_ref[s], tq)

        shifted = pltpu.roll(state, n_buf - s_new, 1)
        tail = jnp.where(lane >= LANES - s_new, pltpu.roll(win_new, LANES - s_new, 1),
                         shifted[:, n_buf - LANES:])
        wout_ref[s, :, :n_buf - LANES] = shifted[:, :n_buf - LANES]
        wout_ref[s, :, n_buf - LANES:] = tail


def _nsa_sample(pool, win_state, layer, page_table, qraw, qrot, gates, ckv, slc_new, win_new, et,
                n_seq, s_new, carried):
    db, n_pages = page_table.shape
    past = n_pages * PAGE_SIZE
    tq = SAMPLE_ROWS
    n_buf = win_state.shape[3]
    seq = lambda a: pl.BlockSpec((n_seq,) + a.shape[1:], lambda i, pt: (i, 0, 0))

    def page_spec(s, p):
        return pl.BlockSpec((None, None) + pool.shape[2:],
                            lambda i, pt: (layer, pt[i * n_seq + s, p], 0, 0))

    state_spec = pl.BlockSpec((None, n_seq) + win_state.shape[2:], lambda i, pt: (layer, i, 0, 0))
    grid_spec = pltpu.PrefetchScalarGridSpec(
        num_scalar_prefetch=1,
        grid=(db // n_seq,),
        in_specs=[page_spec(s, p) for s in range(n_seq) for p in range(n_pages)]
        + [seq(qraw), seq(qrot), seq(gates), seq(ckv), seq(slc_new), state_spec, seq(win_new),
           pl.BlockSpec(et.shape, lambda i, pt: (0, 0))]
        + [pl.BlockSpec(memory_space=pl.ANY)] * len(carried),
        out_specs=[pl.BlockSpec((n_seq, tq, NSA_WIDTH), lambda i, pt: (i, 0, 0)),
                   pl.BlockSpec((None, n_seq) + win_state.shape[2:], lambda i, pt: (layer, i, 0, 0))],
    )
    n_operands = 1 + n_seq * n_pages + 8
    return pl.pallas_call(
        functools.partial(_nsa_sample_kernel, n_seq, n_pages, past, s_new),
        grid_spec=grid_spec,
        out_shape=[jax.ShapeDtypeStruct((db, tq, NSA_WIDTH), F32),
                   jax.ShapeDtypeStruct(win_state.shape, F32)],
        input_output_aliases={n_operands + k: 1 + k for k in range(len(carried))},
        compiler_params=pltpu.CompilerParams(dimension_semantics=("parallel",),
                                             vmem_limit_bytes=VMEM_LIMIT),
        name="nsa_sample",
    )(page_table, *([pool] * (n_seq * n_pages)), qraw, qrot, gates, ckv, slc_new, win_state, win_new, et,
      *carried)


def _post_kernel(chunk, alpha, n_ffn_chunks, x_ref, o_ref, u_ref, v_ref, gw_ref, gb_ref,
                 wout_ref, ln1g_ref, ln1b_ref, wg_ref, wu_ref, wd_ref, ln2g_ref, ln2b_ref, y_ref):
    tm = x_ref.shape[0]
    c = GMLP_CHUNK
    row = lax.broadcasted_iota(jnp.int32, (c, c), 0)
    col = lax.broadcasted_iota(jnp.int32, (c, c), 1)
    causal = (col <= row) & ((row // chunk) == (col // chunk))
    w_cat = jnp.concatenate([jnp.where(causal, gw_ref[h], 0.0) for h in range(GMLP_GROUPS)],
                            axis=1).astype(BF16)
    lane_group = lax.broadcasted_iota(jnp.int32, (c, GMLP_WIDTH), 1) // (GMLP_WIDTH // GMLP_GROUPS)
    mixed = []
    for sub in range(tm // c):
        v = v_ref[sub * c:(sub + 1) * c, :]
        v_diag = jnp.concatenate([jnp.where(lane_group == h, v, 0.0) for h in range(GMLP_GROUPS)],
                                 axis=0).astype(BF16)
        s = _dot(w_cat, v_diag) + gb_ref[...]
        mixed.append(u_ref[sub * c:(sub + 1) * c, :].astype(F32) * s)
    o_gmlp = jnp.concatenate(mixed, axis=0)
    mix = jnp.concatenate([o_ref[...], o_gmlp], axis=1).astype(BF16)
    h = _dot(mix, wout_ref[...])
    x1 = _layer_norm(alpha * x_ref[...] + h, ln1g_ref[...], ln1b_ref[...])
    x1b = x1.astype(BF16)
    f = None
    fc = wg_ref.shape[1] // n_ffn_chunks
    for j in range(n_ffn_chunks):
        gate = _dot(x1b, wg_ref[:, j * fc:(j + 1) * fc])
        up = _dot(x1b, wu_ref[:, j * fc:(j + 1) * fc])
        a = (jax.nn.silu(gate) * up).astype(BF16)
        part = _dot(a, wd_ref[j * fc:(j + 1) * fc, :])
        f = part if f is None else f + part
    y_ref[...] = _layer_norm(alpha * x1 + f, ln2g_ref[...], ln2b_ref[...])


def _post(x, o_nsa, u, v, gw, gb, wout, ln1g, ln1b, wg, wu, wd, ln2g, ln2b, chunk, alpha, tm):
    n, d = x.shape
    ffn = wg.shape[1]
    n_ffn_chunks = next(k for k in (4, 2, 1) if ffn % (k * LANES) == 0)
    row = lambda width: pl.BlockSpec((tm, width), lambda i: (i, 0))
    const = lambda a: pl.BlockSpec(a.shape, lambda i: (0,) * a.ndim, pipeline_mode=pl.Buffered(1))
    consts = (gw, gb, wout, ln1g, ln1b, wg, wu, wd, ln2g, ln2b)
    return pl.pallas_call(
        functools.partial(_post_kernel, chunk, alpha, n_ffn_chunks),
        grid=(n // tm,),
        in_specs=[row(d), row(NSA_WIDTH), row(GMLP_WIDTH), row(GMLP_WIDTH)] + [const(a) for a in consts],
        out_specs=row(d),
        out_shape=jax.ShapeDtypeStruct((n, d), F32),
        compiler_params=pltpu.CompilerParams(dimension_semantics=("parallel",),
                                             vmem_limit_bytes=VMEM_LIMIT),
        name="post",
    )(x, o_nsa, u, v, *consts)


def _pack_w_in_t(w_in):
    wt = w_in.T
    o_gate = NSA_WIDTH + 3 * 2 * KV_WIDTH
    o_gmlp = o_gate + 3 * NSA_HEADS
    gate = jnp.pad(wt[o_gate:o_gmlp], ((0, LANES - 3 * NSA_HEADS), (0, 0)))
    return jnp.concatenate([wt[:o_gate], gate, wt[o_gmlp:]], axis=0).astype(BF16)


def _pack_compress(pe, w1, b1, w2):
    eye = jnp.eye(KV_HEADS, dtype=F32)
    zero = jnp.zeros_like(w1)
    w1p = jnp.stack([jnp.concatenate([w1, zero], axis=-1), jnp.concatenate([zero, w1], axis=-1)], axis=2)
    w1p = w1p.reshape(2, CMP_BLOCK * KV_WIDTH, KV_HEADS * CMP_HIDDEN)
    pep = jnp.broadcast_to(pe[:, :, None, :], (2, CMP_BLOCK, KV_HEADS, HEAD_DIM)).reshape(2, 1, -1)
    b1p = jnp.tile(b1, (1, KV_HEADS))[:, None, :]
    w2p = jnp.einsum('khd,gG->kghGd', w2, eye).reshape(2, KV_HEADS * CMP_HIDDEN, KV_WIDTH)
    return pep, w1p.astype(BF16), b1p, w2p.astype(BF16)


def _pack_gmlp(ws, bs, chunk):
    reps = GMLP_CHUNK // chunk
    gw = jnp.tile(ws[:, :chunk, :chunk], (1, reps, reps))
    gb = jnp.tile(jnp.repeat(bs[:, :chunk].T, GMLP_WIDTH // GMLP_GROUPS, axis=1), (reps, 1))
    return gw, gb


def _rope_tables(pos):
    half = HEAD_DIM // 2
    inv_freq = ROPE_THETA ** (-jnp.arange(half, dtype=F32) / half)
    ang = pos.astype(F32)[:, None] * inv_freq[None, :]
    cos, sin = jnp.cos(ang), jnp.sin(ang)
    return jnp.concatenate([cos, cos], axis=1), jnp.concatenate([-sin, sin], axis=1)


def _block_indicator(n_keys):
    key_block = np.arange(n_keys)[None, :] // SLC_BLOCK
    return jnp.asarray(key_block == np.arange(MAX_SLC_BLOCKS)[:, None], dtype=BF16)


def _feature_major(a):
    lead = a.shape[:-4]
    n = len(lead)
    perm = tuple(range(n)) + (n + 1, n + 2, n + 3, n)
    return jnp.transpose(a, perm).reshape(lead + (2 * KV_WIDTH, a.shape[-4]))


def _position_major(a):
    lead = a.shape[:-2]
    n = len(lead)
    a = a.reshape(lead + (2, KV_HEADS, HEAD_DIM, a.shape[-1]))
    return jnp.transpose(a, tuple(range(n)) + (n + 3, n, n + 1, n + 2))


def kernel(x_prompt, x_sample, cache_cmp_kv, cache_slc_kv, state_win_kv, page_table,
           w_in, cmp_pe, cmp_w1, cmp_b1, cmp_w2, gmlp_ln_g, gmlp_ln_b, gmlp_ws, gmlp_bs,
           w_out, ln1_g, ln1_b, w_gate, w_up, w_down, ln2_g, ln2_b):
    depth = w_in.shape[0]
    b, t, d = x_prompt.shape
    db, s_new, _ = x_sample.shape
    n_pages = page_table.shape[1]
    past = n_pages * PAGE_SIZE
    n_buf = state_win_kv.shape[2]
    kv_tail = (2, KV_HEADS, HEAD_DIM)
    rows_s = SAMPLE_ROWS
    assert t % SLC_KEY_TILE == 0 and t >= WINDOW + QUERY_BLOCK and t // SLC_BLOCK <= MAX_SLC_BLOCKS
    assert s_new <= rows_s and (past + rows_s - 1) // SLC_BLOCK < MAX_SLC_BLOCKS
    assert past % CMP_BLOCK == 0 and past % CMP_BLOCK + s_new < CMP_BLOCK
    assert n_buf % LANES == 0 and n_buf >= LANES
    alpha = (2 * depth) ** 0.25

    tm_p = 512
    tm_s = min(512, db * rows_s)
    seq_per_step = 2 if db % 2 == 0 else 1
    attn_seq_per_step = 4 if db % 4 == 0 else seq_per_step
    cos_p, sin_p = _rope_tables(jnp.arange(t, dtype=jnp.int32))
    cos_s, sin_s = _rope_tables(past + jnp.arange(rows_s, dtype=jnp.int32))
    lane_reps = LANES // HEAD_DIM
    cos_pl, sin_pl = jnp.tile(cos_p, (1, lane_reps)), jnp.tile(sin_p, (1, lane_reps))
    cos_pt, sin_pt = cos_p.T, sin_p.T
    cos_s = jnp.tile(cos_s, (tm_s // rows_s, lane_reps))
    sin_s = jnp.tile(sin_s, (tm_s // rows_s, lane_reps))
    et_p = _block_indicator(t)
    et_s = _block_indicator(past + LANES)
    pool_cmp = _feature_major(cache_cmp_kv)
    pool_slc = _feature_major(cache_slc_kv)
    win_state = _feature_major(state_win_kv)

    yp = x_prompt.reshape(b * t, d)
    ys = jnp.pad(x_sample, ((0, 0), (0, rows_s - s_new), (0, 0))).reshape(db * rows_s, d)
    outs = {k: [] for k in ("s_cmp", "s_slc", "s_v")}
    assert min(WINDOW, t) == tm_p
    new_kv_prompt = ()
    win_next = ()
    cmp_tile = min(128, b * t // CMP_BLOCK)

    for l in range(depth):
        wt = _pack_w_in_t(w_in[l])
        pe_p, w1_p, b1_p, w2_p = _pack_compress(cmp_pe[l], cmp_w1[l], cmp_b1[l], cmp_w2[l])
        ln_g = gmlp_ln_g[l][None, :]
        ln_b = gmlp_ln_b[l][None, :]
        tail = (w_out[l].astype(BF16), ln1_g[l][None, :], ln1_b[l][None, :], w_gate[l].astype(BF16),
                w_up[l].astype(BF16), w_down[l].astype(BF16), ln2_g[l][None, :], ln2_b[l][None, :])

        qraw, qrot, gates, u, v, cmp_kv, *new_kv_prompt, slcb, winb = _inproj_prompt(
            yp, wt, cos_pl, sin_pl, cos_pt, sin_pt, ln_g, ln_b, tm_p, b, t, depth, l, new_kv_prompt)
        ckv = _compress_prompt(cmp_kv.reshape(b * t // CMP_BLOCK, CMP_BLOCK * 2 * KV_WIDTH),
                               pe_p, w1_p, b1_p, w2_p, cmp_tile)
        r3 = lambda a: a.reshape(b, t, a.shape[-1])
        o_nsa = _nsa_prompt(r3(qraw), r3(qrot), r3(gates), ckv.reshape(b, t // SLC_BLOCK, 4 * KV_WIDTH),
                            slcb, winb, et_p)
        gw, gb = _pack_gmlp(gmlp_ws[l], gmlp_bs[l], GMLP_CHUNK)
        yp = _post(yp, o_nsa.reshape(b * t, NSA_WIDTH), u, v, gw, gb, *tail, GMLP_CHUNK, alpha, tm_p)

        qraw, qrot, gates, u, v, cmp_kv, slc_kv, win_kv = _inproj_sample(
            ys, wt, cos_s, sin_s, ln_g, ln_b, tm_s)
        ckv = _compress_sample(pool_cmp, l, page_table, pe_p, w1_p, b1_p, w2_p, attn_seq_per_step)
        s3 = lambda a: a.reshape(db, rows_s, a.shape[-1])
        o_nsa, *win_next = _nsa_sample(pool_slc, win_state, l, page_table, s3(qraw), s3(qrot), s3(gates),
                                       ckv.reshape(db, past // SLC_BLOCK, 4 * KV_WIDTH),
                                       s3(slc_kv), s3(win_kv), et_s, attn_seq_per_step, s_new, win_next)
        gw, gb = _pack_gmlp(gmlp_ws[l], gmlp_bs[l], rows_s)
        ys = _post(ys, o_nsa.reshape(db * rows_s, NSA_WIDTH), u, v, gw, gb, *tail, rows_s, alpha, tm_s)
        new = lambda a: a.reshape((db, rows_s) + a.shape[1:])[:, :s_new]
        outs["s_cmp"].append(new(cmp_kv).reshape((db, s_new) + kv_tail))
        outs["s_slc"].append(new(slc_kv).reshape((db, s_new) + kv_tail))
        outs["s_v"].append(new(v))

    y_sample = ys.reshape(db, rows_s, d)[:, :s_new]
    p_cmp, p_slc, p_win = (_position_major(a) for a in new_kv_prompt)
    return (yp.reshape(b, t, d), y_sample, p_cmp, p_slc, p_win,
            jnp.stack(outs["s_cmp"]), jnp.stack(outs["s_slc"]),
            _position_major(win_next[0]), jnp.stack(outs["s_v"]))
```

```python
import functools
import math

import numpy as np
import jax
import jax.numpy as jnp
from jax import lax
from jax.experimental import pallas as pl
from jax.experimental.pallas import tpu as pltpu

F32 = jnp.float32
BF16 = jnp.bfloat16

LANES = 128
SUBLANES = 8
HEAD_DIM = 64
NSA_HEADS = 8
KV_HEADS = 2
GROUP = NSA_HEADS // KV_HEADS
NSA_WIDTH = NSA_HEADS * HEAD_DIM
KV_WIDTH = KV_HEADS * HEAD_DIM
GMLP_WIDTH = 512
GMLP_GROUPS = 8
GMLP_CHUNK = 128
CMP_BLOCK = 32
CMP_HIDDEN = 256
CMP_PITCH = CMP_BLOCK + SUBLANES
SLC_BLOCK = 64
N_SELECT = 16
WINDOW = 512
QUERY_BLOCK = 128
RANK_TOKENS = LANES
RANK_TILE_GROUP = 4
PAGE_SIZE = 128
FORCE_BONUS = 1.0e4
ROPE_THETA = 10000.0
LN_EPS = 1e-5
NEG_INF = -1e30
MAX_SLC_BLOCKS = LANES
SAMPLE_ROWS = SUBLANES
SLC_KEY_TILE = 512
VMEM_LIMIT = 56 * 1024 * 1024
Q_SCALE = HEAD_DIM ** -0.5 * math.log2(math.e)

C_Q = 0
C_CMP = C_Q + NSA_WIDTH
C_SLC = C_CMP + 2 * KV_WIDTH
C_WIN = C_SLC + 2 * KV_WIDTH
C_GATE = C_WIN + 2 * KV_WIDTH
C_U = C_GATE + LANES
C_V = C_U + GMLP_WIDTH
C_END = C_V + GMLP_WIDTH


def _dot(a, b):
    return jnp.dot(a, b, preferred_element_type=F32)


def _dot_nt(a, b):
    return lax.dot_general(a, b, (((1,), (1,)), ((), ())), preferred_element_type=F32)


def _layer_norm(x, g, b):
    mu = jnp.mean(x, axis=-1, keepdims=True)
    xc = x - mu
    var = jnp.mean(xc * xc, axis=-1, keepdims=True)
    return xc * lax.rsqrt(var + LN_EPS) * g + b


def _softmax_masked(s, mask):
    s = jnp.where(mask, s, NEG_INF)
    m = jnp.max(s, axis=-1, keepdims=True)
    e = jnp.where(mask, jnp.exp2(s - m), 0.0)
    return e * (1.0 / jnp.maximum(jnp.sum(e, axis=-1, keepdims=True), 1e-30))


def _rope_lanes(c, cos, sin, first_half):
    partner = jnp.where(first_half, pltpu.roll(c, LANES - HEAD_DIM // 2, 1),
                        pltpu.roll(c, HEAD_DIM // 2, 1))
    return c * cos + partner * sin


def _inproj_common(xb, wt_ref, cos, sin, first_half, lng_ref, lnb_ref,
                   qraw_ref, qrot_ref, gate_ref, u_ref, v_ref):
    zq = _dot_nt(xb, wt_ref[C_Q:C_CMP, :])
    qraw_ref[...] = (zq * Q_SCALE).astype(qraw_ref.dtype)
    for c in range(NSA_WIDTH // LANES):
        qrot_ref[:, c * LANES:(c + 1) * LANES] = (_rope_lanes(
            zq[:, c * LANES:(c + 1) * LANES], cos, sin, first_half) * Q_SCALE).astype(qrot_ref.dtype)
    gate_ref[...] = jax.nn.sigmoid(_dot_nt(xb, wt_ref[C_GATE:C_U, :]))
    u_ref[...] = jax.nn.gelu(_dot_nt(xb, wt_ref[C_U:C_V, :])).astype(u_ref.dtype)
    zv = jax.nn.gelu(_dot_nt(xb, wt_ref[C_V:C_END, :]))
    v_ref[...] = _layer_norm(zv, lng_ref[...], lnb_ref[...]).astype(v_ref.dtype)


def _inproj_prompt_kernel(n_carried, x_ref, wt_ref, cos_ref, sin_ref, cost_ref, sint_ref, lng_ref,
                          lnb_ref, *refs):
    (qraw_ref, qrot_ref, gate_ref, u_ref, v_ref,
     cmp_ref, cmpt_ref, slct_ref, wint_ref, slcb_ref, winb_ref) = refs[n_carried:]
    xb = x_ref[...].astype(BF16)
    cos = cos_ref[...]
    lane = lax.broadcasted_iota(jnp.int32, cos.shape, 1)
    first_half = (lane % HEAD_DIM) < (HEAD_DIM // 2)
    _inproj_common(xb, wt_ref, cos, sin_ref[...], first_half, lng_ref, lnb_ref,
                   qraw_ref, qrot_ref, gate_ref, u_ref, v_ref)
    cmp_ref[...] = _dot_nt(xb, wt_ref[C_CMP:C_SLC, :])
    cmpt_ref[...] = _dot_nt(wt_ref[C_CMP:C_SLC, :], xb)
    cost = cost_ref[...]
    sint = sint_ref[...]
    half = HEAD_DIM // 2
    for row0, f_ref, b_ref in ((C_SLC, slct_ref, slcb_ref), (C_WIN, wint_ref, winb_ref)):
        zt = _dot_nt(wt_ref[row0:row0 + 2 * KV_WIDTH, :], xb)
        for g in range(KV_HEADS):
            kg = zt[g * HEAD_DIM:(g + 1) * HEAD_DIM, :]
            partner = jnp.concatenate([kg[half:, :], kg[:half, :]], axis=0)
            kr = kg * cost + partner * sint
            f_ref[g * HEAD_DIM:(g + 1) * HEAD_DIM, :] = kr
            b_ref[g * HEAD_DIM:(g + 1) * HEAD_DIM, :] = kr.astype(BF16)
        f_ref[KV_WIDTH:, :] = zt[KV_WIDTH:, :]
        b_ref[KV_WIDTH:, :] = zt[KV_WIDTH:, :].astype(BF16)


def _inproj_sample_kernel(x_ref, wt_ref, cos_ref, sin_ref, lng_ref, lnb_ref,
                          qraw_ref, qrot_ref, gate_ref, u_ref, v_ref, cmp_ref, slc_ref, win_ref):
    xb = x_ref[...].astype(BF16)
    cos = cos_ref[...]
    sin = sin_ref[...]
    lane = lax.broadcasted_iota(jnp.int32, cos.shape, 1)
    first_half = (lane % HEAD_DIM) < (HEAD_DIM // 2)
    _inproj_common(xb, wt_ref, cos, sin, first_half, lng_ref, lnb_ref,
                   qraw_ref, qrot_ref, gate_ref, u_ref, v_ref)
    cmp_ref[...] = _dot_nt(xb, wt_ref[C_CMP:C_SLC, :])
    for row0, f_ref in ((C_SLC, slc_ref), (C_WIN, win_ref)):
        z = _dot_nt(xb, wt_ref[row0:row0 + 2 * KV_WIDTH, :])
        f_ref[:, :KV_WIDTH] = _rope_lanes(z[:, :KV_WIDTH], cos, sin, first_half)
        f_ref[:, KV_WIDTH:] = z[:, KV_WIDTH:]


def _inproj_prompt(x, wt, cos, sin, cost, sint, ln_g, ln_b, tm, b, t, depth, layer, carried):
    n, d = x.shape
    n_tab = t // tm
    row = lambda width: pl.BlockSpec((tm, width), lambda i: (i, 0))
    const = lambda a: pl.BlockSpec(a.shape, lambda i: (0,) * a.ndim)
    tab = pl.BlockSpec((tm, LANES), lambda i: (i % n_tab, 0))
    tabt = pl.BlockSpec((HEAD_DIM, tm), lambda i: (0, i % n_tab))
    fm = pl.BlockSpec((None, 2 * KV_WIDTH, tm), lambda i: (i // n_tab, 0, i % n_tab))
    fm_layer = pl.BlockSpec((None, None, 2 * KV_WIDTH, tm), lambda i: (layer, i // n_tab, 0, i % n_tab))
    keep_layer = pl.BlockSpec((None, None, 2 * KV_WIDTH, tm), lambda i: (layer, i // n_tab, 0, 0))
    row_out = ((NSA_WIDTH, BF16), (NSA_WIDTH, BF16), (LANES, F32), (GMLP_WIDTH, BF16), (GMLP_WIDTH, BF16),
               (2 * KV_WIDTH, F32))
    n_in = 8
    n_row = len(row_out)
    stacked = lambda width: jax.ShapeDtypeStruct((depth, b, 2 * KV_WIDTH, width), F32)
    return pl.pallas_call(
        functools.partial(_inproj_prompt_kernel, len(carried)),
        grid=(n // tm,),
        in_specs=[row(d), const(wt), tab, tab, tabt, tabt, const(ln_g), const(ln_b)]
        + [pl.BlockSpec(memory_space=pl.ANY)] * len(carried),
        out_specs=[row(wd) for wd, _ in row_out] + [fm_layer, fm_layer, keep_layer, fm, fm],
        out_shape=[jax.ShapeDtypeStruct((n, wd), dt) for wd, dt in row_out]
        + [stacked(t), stacked(t), stacked(tm)]
        + [jax.ShapeDtypeStruct((b, 2 * KV_WIDTH, t), BF16)] * 2,
        input_output_aliases={n_in + k: n_row + k for k in range(len(carried))},
        compiler_params=pltpu.CompilerParams(dimension_semantics=("arbitrary",),
                                             vmem_limit_bytes=VMEM_LIMIT),
        name="inproj_prompt",
    )(x, wt, cos, sin, cost, sint, ln_g, ln_b, *carried)


def _inproj_sample(x, wt, cos, sin, ln_g, ln_b, tm):
    n, d = x.shape
    row = lambda width: pl.BlockSpec((tm, width), lambda i: (i, 0))
    const = lambda a: pl.BlockSpec(a.shape, lambda i: (0,) * a.ndim)
    widths = (NSA_WIDTH, NSA_WIDTH, LANES, GMLP_WIDTH, GMLP_WIDTH,
              2 * KV_WIDTH, 2 * KV_WIDTH, 2 * KV_WIDTH)
    return pl.pallas_call(
        _inproj_sample_kernel,
        grid=(n // tm,),
        in_specs=[row(d), const(wt), const(cos), const(sin), const(ln_g), const(ln_b)],
        out_specs=[row(wd) for wd in widths],
        out_shape=[jax.ShapeDtypeStruct((n, wd), F32) for wd in widths],
        compiler_params=pltpu.CompilerParams(dimension_semantics=("parallel",),
                                             vmem_limit_bytes=VMEM_LIMIT),
        name="inproj_sample",
    )(x, wt, cos, sin, ln_g, ln_b)


def _compress_rows(chunk, pe_ref, w1_ref, b1_ref, w2_ref):
    outs = []
    for k in range(2):
        a = jnp.concatenate([chunk(l, k) for l in range(CMP_BLOCK)], axis=1)
        a = (a + pe_ref[k]).astype(BF16)
        h = jax.nn.silu(_dot(a, w1_ref[k]) + b1_ref[k])
        outs.append(_dot(h.astype(BF16), w2_ref[k]))
    return jnp.concatenate(outs, axis=1)


def _compress_prompt_kernel(x_ref, pe_ref, w1_ref, b1_ref, w2_ref, o_ref):
    row_w = 2 * KV_WIDTH
    chunk = lambda l, k: x_ref[:, l * row_w + k * KV_WIDTH:l * row_w + (k + 1) * KV_WIDTH]
    o_ref[...] = _compress_rows(chunk, pe_ref, w1_ref, b1_ref, w2_ref)


def _compress_prompt(x, pe, w1, b1, w2, tm):
    n = x.shape[0]
    const = lambda a: pl.BlockSpec(a.shape, lambda i: (0,) * a.ndim)
    return pl.pallas_call(
        _compress_prompt_kernel,
        grid=(n // tm,),
        in_specs=[pl.BlockSpec((tm, x.shape[1]), lambda i: (i, 0)),
                  const(pe), const(w1), const(b1), const(w2)],
        out_specs=pl.BlockSpec((tm, 2 * KV_WIDTH), lambda i: (i, 0)),
        out_shape=jax.ShapeDtypeStruct((n, 2 * KV_WIDTH), F32),
        compiler_params=pltpu.CompilerParams(dimension_semantics=("parallel",),
                                             vmem_limit_bytes=VMEM_LIMIT),
        name="compress_prompt",
    )(x, pe, w1, b1, w2)


def _compress_sample_kernel(n_seq, n_pages, pt_ref, *refs):
    page_refs = refs[:n_seq * n_pages]
    pe_ref, w1_ref, b1_ref, w2_ref, o_ref, posk_ref, posv_ref = refs[n_seq * n_pages:]
    blocks_per_page = PAGE_SIZE // CMP_BLOCK
    for j, page_ref in enumerate(page_refs):
        for dst, rows in ((posk_ref, page_ref[:KV_WIDTH, :].T), (posv_ref, page_ref[KV_WIDTH:, :].T)):
            for n in range(blocks_per_page):
                r0 = (j * blocks_per_page + n) * CMP_PITCH
                dst[r0:r0 + CMP_BLOCK, :] = rows[n * CMP_BLOCK:(n + 1) * CMP_BLOCK, :]
    m = n_seq * n_pages * blocks_per_page
    chunk = lambda l, k: (posk_ref, posv_ref)[k][pl.ds(l, m, stride=CMP_PITCH), :]
    o_ref[...] = _compress_rows(chunk, pe_ref, w1_ref, b1_ref, w2_ref)


def _compress_sample(pool, layer, page_table, pe, w1, b1, w2, n_seq):
    db, n_pages = page_table.shape
    rows = n_seq * n_pages * (PAGE_SIZE // CMP_BLOCK)
    const = lambda a: pl.BlockSpec(a.shape, lambda i, pt: (0,) * a.ndim)

    def page_spec(s, p):
        return pl.BlockSpec((None, None) + pool.shape[2:],
                            lambda i, pt: (layer, pt[i * n_seq + s, p], 0, 0))

    grid_spec = pltpu.PrefetchScalarGridSpec(
        num_scalar_prefetch=1,
        grid=(db // n_seq,),
        in_specs=[page_spec(s, p) for s in range(n_seq) for p in range(n_pages)]
        + [const(pe), const(w1), const(b1), const(w2)],
        out_specs=pl.BlockSpec((rows, 2 * KV_WIDTH), lambda i, pt: (i, 0)),
        scratch_shapes=[pltpu.VMEM((rows * CMP_PITCH, KV_WIDTH), F32)] * 2,
    )
    return pl.pallas_call(
        functools.partial(_compress_sample_kernel, n_seq, n_pages),
        grid_spec=grid_spec,
        out_shape=jax.ShapeDtypeStruct((db * n_pages * (PAGE_SIZE // CMP_BLOCK), 2 * KV_WIDTH), F32),
        compiler_params=pltpu.CompilerParams(dimension_semantics=("parallel",),
                                             vmem_limit_bytes=VMEM_LIMIT),
        name="compress_sample",
    )(page_table, *([pool] * (n_seq * n_pages)), pe, w1, b1, w2)


def _stack_heads(q, tq):
    lane_lo = lax.broadcasted_iota(jnp.int32, (tq, LANES), 1) < HEAD_DIM
    pieces = []
    for hh in range(NSA_HEADS):
        g = hh // GROUP
        c = hh // 2
        chunk = q[:, c * LANES:(c + 1) * LANES]
        if hh % 2 != g:
            chunk = pltpu.roll(chunk, HEAD_DIM, 1)
        pieces.append(jnp.where(lane_lo if g == 0 else jnp.logical_not(lane_lo), chunk, 0.0))
    return jnp.concatenate(pieces, axis=0).astype(BF16)


def _cmp_branch(qraw, ckv, qpos, tq):
    nb = MAX_SLC_BLOCKS
    kc = jnp.concatenate([ckv[:, 0:KV_WIDTH], ckv[:, 2 * KV_WIDTH:3 * KV_WIDTH]], axis=0).astype(BF16)
    vc = jnp.concatenate([ckv[:, KV_WIDTH:2 * KV_WIDTH], ckv[:, 3 * KV_WIDTH:]], axis=0).astype(BF16)
    s = _dot_nt(qraw, kc).reshape(NSA_HEADS, tq, 2 * nb)
    lane = lax.broadcasted_iota(jnp.int32, (tq, 2 * nb), 1)
    blk = jnp.where(lane < nb, 2 * lane, 2 * (lane - nb) + 1)
    mask = ((blk + 1) * CMP_BLOCK - 1) <= qpos
    p = _softmax_masked(s, mask[None])
    o_cmp = _dot(p.reshape(NSA_HEADS * tq, 2 * nb).astype(BF16), vc)
    scores = []
    sblk = lax.broadcasted_iota(jnp.int32, (tq, nb), 1)
    cur = qpos // SLC_BLOCK
    valid = sblk <= cur
    forced = ((sblk == 0) | (sblk == cur) | (sblk == cur - 1)).astype(F32)
    for g in range(KV_HEADS):
        imp = p[g * GROUP]
        for r in range(1, GROUP):
            imp = imp + p[g * GROUP + r]
        imp = imp[:, :nb] + imp[:, nb:]
        scores.append(jnp.where(valid, imp + FORCE_BONUS * forced, -1.0))
    return o_cmp, scores, valid


def _rank_blocks_wide(score, st_ref, cnt_ref, n_chunks):
    st_ref[...] = score.T
    cnt_ref[...] = jnp.zeros_like(cnt_ref)
    sub = lax.broadcasted_iota(jnp.int32, (SUBLANES, LANES), 0)
    n_vregs = MAX_SLC_BLOCKS // SUBLANES
    def count(c, v_lo, v_hi):
        cand = st_ref[c * SUBLANES:(c + 1) * SUBLANES, :]
        tiles = {v: st_ref[v * SUBLANES:(v + 1) * SUBLANES, :] for v in range(v_lo, v_hi)}
        cnts = {v: cnt_ref[v * SUBLANES:(v + 1) * SUBLANES, :] for v in range(v_lo, v_hi)}
        for ii in range(SUBLANES):
            row = cand[ii:ii + 1, :]
            for v in range(v_lo, v_hi):
                if v < c:
                    inc = jnp.where(row > tiles[v], 1.0, 0.0)
                elif v > c:
                    inc = jnp.where(row >= tiles[v], 1.0, 0.0)
                else:
                    inc = jnp.where(sub > ii, jnp.where(row >= tiles[v], 1.0, 0.0),
                                    jnp.where(row > tiles[v], 1.0, 0.0))
                cnts[v] = cnts[v] + inc
        for v in range(v_lo, v_hi):
            cnt_ref[v * SUBLANES:(v + 1) * SUBLANES, :] = cnts[v]

    for c in range(n_vregs):
        @pl.when(c < n_chunks)
        def _():
            for v_lo in range(0, n_vregs, RANK_TILE_GROUP):
                v_hi = v_lo + RANK_TILE_GROUP
                if v_lo <= c:
                    count(c, v_lo, v_hi)
                else:
                    pl.when(v_lo < n_chunks)(functools.partial(count, c, v_lo, v_hi))
    return cnt_ref[...].T


def _rank_blocks_narrow(score, n_blocks):
    lane = lax.broadcasted_iota(jnp.int32, score.shape, 1)
    cnt = jnp.zeros(score.shape, F32)
    for i in range(n_blocks):
        col = score[:, i:i + 1]
        cnt = cnt + jnp.where(lane > i, jnp.where(col >= score, 1.0, 0.0),
                              jnp.where(col > score, 1.0, 0.0))
    return cnt


def _slc_branch(q_aug, kv_ref, et_ref, n_full, qpos, tq, tk):
    ones_rows = jnp.where(lax.broadcasted_iota(jnp.int32, (HEAD_DIM, tk), 0) == 0, 1.0, 0.0).astype(BF16)

    def tile(kt, carry, diagonal):
        ms, accs = carry
        off = pl.multiple_of(kt * tk, tk)
        ka = jnp.concatenate([et_ref[:, pl.ds(off, tk)], kv_ref[:KV_WIDTH, pl.ds(off, tk)]], axis=0)
        s = _dot(q_aug, ka)
        if diagonal:
            visible = (off + lax.broadcasted_iota(jnp.int32, (tq, tk), 1)) <= qpos
        ps, m_new, alphas = [], [], []
        for hh in range(NSA_HEADS):
            s_h = s[hh * tq:(hh + 1) * tq, :]
            if diagonal:
                s_h = jnp.where(visible, s_h, NEG_INF)
            m_h = jnp.maximum(ms[hh], jnp.max(s_h, axis=-1, keepdims=True))
            ps.append(jnp.exp2(s_h - m_h).astype(BF16))
            alphas.append(jnp.exp2(ms[hh] - m_h))
            m_new.append(m_h)
        acc_new = []
        for g in range(KV_HEADS):
            v0 = KV_WIDTH + g * HEAD_DIM
            va = jnp.concatenate([kv_ref[v0:v0 + HEAD_DIM, pl.ds(off, tk)], ones_rows], axis=0)
            pv = _dot_nt(jnp.concatenate(ps[g * GROUP:(g + 1) * GROUP], axis=0), va)
            alpha = jnp.concatenate(alphas[g * GROUP:(g + 1) * GROUP], axis=0)
            acc_new.append(alpha * accs[g] + pv)
        return tuple(m_new), tuple(acc_new)

    init = (tuple(jnp.full((tq, 1), NEG_INF, F32) for _ in range(NSA_HEADS)),
            tuple(jnp.zeros((GROUP * tq, KV_WIDTH), F32) for _ in range(KV_HEADS)))
    def first_group(count):
        def run(c):
            c = tile(n_full, c, True)
            for u in range(count - 1):
                c = tile(u, c, False)
            return c
        return run

    first = (n_full + 1) % 4
    carry = lax.switch(first, [first_group(4), first_group(1), first_group(2), first_group(3)], init)
    taken = jnp.where(first == 0, 3, first - 1)

    def quad(j, c):
        kt = taken + 4 * j
        for u in range(4):
            c = tile(kt + u, c, False)
        return c

    _, accs = lax.fori_loop(0, (n_full - taken) // 4, quad, carry)
    outs = [a * (1.0 / jnp.maximum(a[:, HEAD_DIM:HEAD_DIM + 1], 1e-30)) for a in accs]
    return jnp.concatenate([outs[0], pltpu.roll(outs[1], HEAD_DIM, 1)], axis=0)


def _softmax_pv(s, mask, vt, tq):
    nk = s.shape[1]
    rows = NSA_HEADS * tq
    if rows <= LANES:
        s3 = jnp.where(mask[None], s.reshape(NSA_HEADS, tq, nk), NEG_INF)
        p = jnp.exp2(s3 - jnp.max(s3, axis=-1, keepdims=True)).reshape(rows, nk).astype(BF16)
        if rows < LANES:
            p = jnp.concatenate([p, jnp.zeros((LANES - rows, nk), BF16)], axis=0)
        ones_rows = jnp.where(lax.broadcasted_iota(jnp.int32, (2 * SUBLANES, nk), 0) == 0,
                              1.0, 0.0).astype(BF16)
        ot = _dot_nt(jnp.concatenate([vt, ones_rows], axis=0), p)
        return (ot[:KV_WIDTH] * (1.0 / jnp.maximum(ot[KV_WIDTH:KV_WIDTH + 1], 1e-30))).T[:rows]
    ones_rows = jnp.where(lax.broadcasted_iota(jnp.int32, (HEAD_DIM, nk), 0) == 0, 1.0, 0.0).astype(BF16)
    outs = []
    for g in range(KV_HEADS):
        ps = []
        for hh in range(g * GROUP, (g + 1) * GROUP):
            s_h = jnp.where(mask, s[hh * tq:(hh + 1) * tq, :], NEG_INF)
            ps.append(jnp.exp2(s_h - jnp.max(s_h, axis=-1, keepdims=True)).astype(BF16))
        va = jnp.concatenate([vt[g * HEAD_DIM:(g + 1) * HEAD_DIM, :], ones_rows], axis=0)
        pv = _dot_nt(jnp.concatenate(ps, axis=0), va)
        outs.append(pv * (1.0 / jnp.maximum(pv[:, HEAD_DIM:HEAD_DIM + 1], 1e-30)))
    return jnp.concatenate([outs[0], pltpu.roll(outs[1], HEAD_DIM, 1)], axis=0)


def _slc_single(bias_rows, qrot, kt, vt, et, qpos, tq):
    nk = kt.shape[1]
    s = _dot(bias_rows, et) + _dot(qrot, kt)
    visible = lax.broadcasted_iota(jnp.int32, (tq, nk), 1) <= qpos
    return _softmax_pv(s, visible, vt, tq)


def _win_branch(qrot, kt, vt, kpos, qpos, tq):
    diff = qpos - kpos
    return _softmax_pv(_dot(qrot, kt), (diff >= 0) & (diff < WINDOW), vt, tq)


def _combine(o_cmp, o_slc, o_win, gates, tq):
    lane_lo = lax.broadcasted_iota(jnp.int32, (tq, LANES), 1) < HEAD_DIM
    heads = []
    for hh in range(NSA_HEADS):
        tot = None
        for br, o in enumerate((o_cmp, o_slc, o_win)):
            col = br * NSA_HEADS + hh
            term = gates[:, col:col + 1] * o[hh * tq:(hh + 1) * tq, :]
            tot = term if tot is None else tot + term
        heads.append(tot)
    chunks = []
    for c in range(NSA_HEADS // 2):
        a, b = heads[2 * c], heads[2 * c + 1]
        if c // 2 == 0:
            chunks.append(jnp.where(lane_lo, a, pltpu.roll(b, HEAD_DIM, 1)))
        else:
            chunks.append(jnp.where(lane_lo, pltpu.roll(a, HEAD_DIM, 1), b))
    return jnp.concatenate(chunks, axis=1)


def _selection_bias(rank, valid):
    return jnp.where((rank < float(N_SELECT)) & valid, 0.0, NEG_INF).astype(BF16)


def _pad_ckv(ckv):
    n = ckv.shape[0]
    if n == MAX_SLC_BLOCKS:
        return ckv
    return jnp.concatenate([ckv, jnp.zeros((MAX_SLC_BLOCKS - n, ckv.shape[1]), F32)], axis=0)


def _nsa_prompt_kernel(qraw_ref, qrot_ref, gate_ref, ckv_ref, slc_ref, win_ref, et_ref,
                       o_ref, st_ref, cnt_ref):
    tq = QUERY_BLOCK
    i = pl.program_id(1)
    qpos0 = i * tq
    qpos = qpos0 + lax.broadcasted_iota(jnp.int32, (tq, 1), 0)
    qraw = _stack_heads(qraw_ref[...].astype(F32), tq)
    qrot = _stack_heads(qrot_ref[...].astype(F32), tq)

    o_cmp, scores, valid = _cmp_branch(qraw, _pad_ckv(ckv_ref[...]), qpos, tq)
    n_chunks = (qpos0 + tq - 1) // (SLC_BLOCK * SUBLANES) + 1
    bias = []
    for g in range(KV_HEADS):
        parts = []
        for t0 in range(0, tq, RANK_TOKENS):
            rank = _rank_blocks_wide(scores[g][t0:t0 + RANK_TOKENS], st_ref, cnt_ref, n_chunks)
            parts.append(_selection_bias(rank, valid[t0:t0 + RANK_TOKENS]))
        bias.append(jnp.concatenate(parts, axis=0))
    bias_rows = jnp.concatenate([bias[0]] * GROUP + [bias[1]] * GROUP, axis=0)
    q_aug = jnp.concatenate([bias_rows, qrot], axis=1)
    o_slc = _slc_branch(q_aug, slc_ref, et_ref, qpos0 // SLC_KEY_TILE, qpos, tq, SLC_KEY_TILE)

    nk = WINDOW + tq
    start = pl.multiple_of(jnp.maximum(qpos0 - WINDOW, 0), tq)
    kpos = start + lax.broadcasted_iota(jnp.int32, (1, nk), 1)
    o_win = _win_branch(qrot, win_ref[:KV_WIDTH, pl.ds(start, nk)], win_ref[KV_WIDTH:, pl.ds(start, nk)],
                        kpos, qpos, tq)

    o_ref[...] = _combine(o_cmp, o_slc, o_win, gate_ref[...], tq).astype(o_ref.dtype)


def _nsa_prompt(qraw, qrot, gates, ckv, slcb, winb, et):
    b, t, _ = qraw.shape
    tq = QUERY_BLOCK
    blk = lambda width: pl.BlockSpec((None, tq, width), lambda bi, i: (bi, i, 0))
    per_batch = lambda a: pl.BlockSpec((None,) + a.shape[1:], lambda bi, i: (bi, 0, 0))
    return pl.pallas_call(
        _nsa_prompt_kernel,
        grid=(b, t // tq),
        in_specs=[blk(NSA_WIDTH), blk(NSA_WIDTH), blk(LANES), per_batch(ckv), per_batch(slcb),
                  per_batch(winb), pl.BlockSpec(et.shape, lambda bi, i: (0, 0))],
        out_specs=blk(NSA_WIDTH),
        out_shape=jax.ShapeDtypeStruct((b, t, NSA_WIDTH), BF16),
        scratch_shapes=[pltpu.VMEM((MAX_SLC_BLOCKS, RANK_TOKENS), F32)] * 2,
        compiler_params=pltpu.CompilerParams(dimension_semantics=("parallel", "arbitrary"),
                                             vmem_limit_bytes=VMEM_LIMIT),
        name="nsa_prompt",
    )(qraw, qrot, gates, ckv, slcb, winb, et)


def _nsa_sample_kernel(n_seq, n_pages, past, s_new, pt_ref, *refs):
    n_in = n_seq * n_pages
    page_refs = refs[:n_in]
    (qraw_ref, qrot_ref, gate_ref, ckv_ref, slcn_ref, wins_ref, winn_ref, et_ref) = refs[n_in:n_in + 8]
    o_ref, wout_ref = refs[-2:]
    tq = SAMPLE_ROWS
    qpos = past + lax.broadcasted_iota(jnp.int32, (tq, 1), 0)
    n_buf = wins_ref.shape[2]
    n_blocks = (past + tq - 1) // SLC_BLOCK + 1
    pad_rows = jnp.zeros((LANES - tq, 2 * KV_WIDTH), F32)
    wpos = past - n_buf + lax.broadcasted_iota(jnp.int32, (1, n_buf + LANES), 1)
    lane = lax.broadcasted_iota(jnp.int32, (2 * KV_WIDTH, LANES), 1)
    for s in range(n_seq):
        slc_new = jnp.concatenate([slcn_ref[s], pad_rows], axis=0).T
        tiles = [page_refs[s * n_pages + p][...].astype(BF16) for p in range(n_pages)]
        tiles.append(slc_new.astype(BF16))
        kt = jnp.concatenate([tl[:KV_WIDTH] for tl in tiles], axis=1)
        vt = jnp.concatenate([tl[KV_WIDTH:] for tl in tiles], axis=1)

        qraw = _stack_heads(qraw_ref[s], tq)
        qrot = _stack_heads(qrot_ref[s], tq)
        o_cmp, scores, valid = _cmp_branch(qraw, _pad_ckv(ckv_ref[s]), qpos, tq)
        bias = [_selection_bias(_rank_blocks_narrow(scores[g], n_blocks), valid)
                for g in range(KV_HEADS)]
        bias_rows = jnp.concatenate([bias[0]] * GROUP + [bias[1]] * GROUP, axis=0)
        o_slc = _slc_single(bias_rows, qrot, kt, vt, et_ref[...], qpos, tq)

        state = wins_ref[s]
        win_new = jnp.concatenate([winn_ref[s], pad_rows], axis=0).T
        kt = jnp.concatenate([state[:KV_WIDTH], win_new[:KV_WIDTH]], axis=1).astype(BF16)
        vt = jnp.concatenate([state[KV_WIDTH:], win_new[KV_WIDTH:]], axis=1).astype(BF16)
        o_win = _win_branch(qrot, kt, vt, wpos, qpos, tq)
        o_ref[s] = _combine(o_cmp, o_slc, o_win, gate_ref[s], tq)

        shifted = pltpu.roll(state, n_buf - s_new, 1)
        tail = jnp.where(lane >= LANES - s_new, pltpu.roll(win_new, LANES - s_new, 1),
                         shifted[:, n_buf - LANES:])
        wout_ref[s, :, :n_buf - LANES] = shifted[:, :n_buf - LANES]
        wout_ref[s, :, n_buf - LANES:] = tail


def _nsa_sample(pool, win_state, layer, page_table, qraw, qrot, gates, ckv, slc_new, win_new, et,
                n_seq, s_new, carried):
    db, n_pages = page_table.shape
    past = n_pages * PAGE_SIZE
    tq = SAMPLE_ROWS
    n_buf = win_state.shape[3]
    seq = lambda a: pl.BlockSpec((n_seq,) + a.shape[1:], lambda i, pt: (i, 0, 0))

    def page_spec(s, p):
        return pl.BlockSpec((None, None) + pool.shape[2:],
                            lambda i, pt: (layer, pt[i * n_seq + s, p], 0, 0))

    state_spec = pl.BlockSpec((None, n_seq) + win_state.shape[2:], lambda i, pt: (layer, i, 0, 0))
    grid_spec = pltpu.PrefetchScalarGridSpec(
        num_scalar_prefetch=1,
        grid=(db // n_seq,),
        in_specs=[page_spec(s, p) for s in range(n_seq) for p in range(n_pages)]
        + [seq(qraw), seq(qrot), seq(gates), seq(ckv), seq(slc_new), state_spec, seq(win_new),
           pl.BlockSpec(et.shape, lambda i, pt: (0, 0))]
        + [pl.BlockSpec(memory_space=pl.ANY)] * len(carried),
        out_specs=[pl.BlockSpec((n_seq, tq, NSA_WIDTH), lambda i, pt: (i, 0, 0)),
                   pl.BlockSpec((None, n_seq) + win_state.shape[2:], lambda i, pt: (layer, i, 0, 0))],
    )
    n_operands = 1 + n_seq * n_pages + 8
    return pl.pallas_call(
        functools.partial(_nsa_sample_kernel, n_seq, n_pages, past, s_new),
        grid_spec=grid_spec,
        out_shape=[jax.ShapeDtypeStruct((db, tq, NSA_WIDTH), F32),
                   jax.ShapeDtypeStruct(win_state.shape, F32)],
        input_output_aliases={n_operands + k: 1 + k for k in range(len(carried))},
        compiler_params=pltpu.CompilerParams(dimension_semantics=("parallel",),
                                             vmem_limit_bytes=VMEM_LIMIT),
        name="nsa_sample",
    )(page_table, *([pool] * (n_seq * n_pages)), qraw, qrot, gates, ckv, slc_new, win_state, win_new, et,
      *carried)


def _post_kernel(chunk, alpha, n_ffn_chunks, x_ref, o_ref, u_ref, v_ref, gw_ref, gb_ref,
                 wout_ref, ln1g_ref, ln1b_ref, wg_ref, wu_ref, wd_ref, ln2g_ref, ln2b_ref, y_ref):
    tm = x_ref.shape[0]
    c = GMLP_CHUNK
    row = lax.broadcasted_iota(jnp.int32, (c, c), 0)
    col = lax.broadcasted_iota(jnp.int32, (c, c), 1)
    causal = (col <= row) & ((row // chunk) == (col // chunk))
    w_cat = jnp.concatenate([jnp.where(causal, gw_ref[h], 0.0) for h in range(GMLP_GROUPS)],
                            axis=1).astype(BF16)
    lane_group = lax.broadcasted_iota(jnp.int32, (c, GMLP_WIDTH), 1) // (GMLP_WIDTH // GMLP_GROUPS)
    mixed = []
    for sub in range(tm // c):
        v = v_ref[sub * c:(sub + 1) * c, :]
        v_diag = jnp.concatenate([jnp.where(lane_group == h, v, 0.0) for h in range(GMLP_GROUPS)],
                                 axis=0).astype(BF16)
        s = _dot(w_cat, v_diag) + gb_ref[...]
        mixed.append(u_ref[sub * c:(sub + 1) * c, :].astype(F32) * s)
    o_gmlp = jnp.concatenate(mixed, axis=0)
    mix = jnp.concatenate([o_ref[...], o_gmlp], axis=1).astype(BF16)
    h = _dot(mix, wout_ref[...])
    x1 = _layer_norm(alpha * x_ref[...] + h, ln1g_ref[...], ln1b_ref[...])
    x1b = x1.astype(BF16)
    f = None
    fc = wg_ref.shape[1] // n_ffn_chunks
    for j in range(n_ffn_chunks):
        gate = _dot(x1b, wg_ref[:, j * fc:(j + 1) * fc])
        up = _dot(x1b, wu_ref[:, j * fc:(j + 1) * fc])
        a = (jax.nn.silu(gate) * up).astype(BF16)
        part = _dot(a, wd_ref[j * fc:(j + 1) * fc, :])
        f = part if f is None else f + part
    y_ref[...] = _layer_norm(alpha * x1 + f, ln2g_ref[...], ln2b_ref[...])


def _post(x, o_nsa, u, v, gw, gb, wout, ln1g, ln1b, wg, wu, wd, ln2g, ln2b, chunk, alpha, tm):
    n, d = x.shape
    ffn = wg.shape[1]
    n_ffn_chunks = next(k for k in (4, 2, 1) if ffn % (k * LANES) == 0)
    row = lambda width: pl.BlockSpec((tm, width), lambda i: (i, 0))
    const = lambda a: pl.BlockSpec(a.shape, lambda i: (0,) * a.ndim, pipeline_mode=pl.Buffered(1))
    consts = (gw, gb, wout, ln1g, ln1b, wg, wu, wd, ln2g, ln2b)
    return pl.pallas_call(
        functools.partial(_post_kernel, chunk, alpha, n_ffn_chunks),
        grid=(n // tm,),
        in_specs=[row(d), row(NSA_WIDTH), row(GMLP_WIDTH), row(GMLP_WIDTH)] + [const(a) for a in consts],
        out_specs=row(d),
        out_shape=jax.ShapeDtypeStruct((n, d), F32),
        compiler_params=pltpu.CompilerParams(dimension_semantics=("parallel",),
                                             vmem_limit_bytes=VMEM_LIMIT),
        name="post",
    )(x, o_nsa, u, v, *consts)


def _pack_w_in_t(w_in):
    wt = w_in.T
    o_gate = NSA_WIDTH + 3 * 2 * KV_WIDTH
    o_gmlp = o_gate + 3 * NSA_HEADS
    gate = jnp.pad(wt[o_gate:o_gmlp], ((0, LANES - 3 * NSA_HEADS), (0, 0)))
    return jnp.concatenate([wt[:o_gate], gate, wt[o_gmlp:]], axis=0).astype(BF16)


def _pack_compress(pe, w1, b1, w2):
    eye = jnp.eye(KV_HEADS, dtype=F32)
    zero = jnp.zeros_like(w1)
    w1p = jnp.stack([jnp.concatenate([w1, zero], axis=-1), jnp.concatenate([zero, w1], axis=-1)], axis=2)
    w1p = w1p.reshape(2, CMP_BLOCK * KV_WIDTH, KV_HEADS * CMP_HIDDEN)
    pep = jnp.broadcast_to(pe[:, :, None, :], (2, CMP_BLOCK, KV_HEADS, HEAD_DIM)).reshape(2, 1, -1)
    b1p = jnp.tile(b1, (1, KV_HEADS))[:, None, :]
    w2p = jnp.einsum('khd,gG->kghGd', w2, eye).reshape(2, KV_HEADS * CMP_HIDDEN, KV_WIDTH)
    return pep, w1p.astype(BF16), b1p, w2p.astype(BF16)


def _pack_gmlp(ws, bs, chunk):
    reps = GMLP_CHUNK // chunk
    gw = jnp.tile(ws[:, :chunk, :chunk], (1, reps, reps))
    gb = jnp.tile(jnp.repeat(bs[:, :chunk].T, GMLP_WIDTH // GMLP_GROUPS, axis=1), (reps, 1))
    return gw, gb


def _rope_tables(pos):
    half = HEAD_DIM // 2
    inv_freq = ROPE_THETA ** (-jnp.arange(half, dtype=F32) / half)
    ang = pos.astype(F32)[:, None] * inv_freq[None, :]
    cos, sin = jnp.cos(ang), jnp.sin(ang)
    return jnp.concatenate([cos, cos], axis=1), jnp.concatenate([-sin, sin], axis=1)


def _block_indicator(n_keys):
    key_block = np.arange(n_keys)[None, :] // SLC_BLOCK
    return jnp.asarray(key_block == np.arange(MAX_SLC_BLOCKS)[:, None], dtype=BF16)


def _feature_major(a):
    lead = a.shape[:-4]
    n = len(lead)
    perm = tuple(range(n)) + (n + 1, n + 2, n + 3, n)
    return jnp.transpose(a, perm).reshape(lead + (2 * KV_WIDTH, a.shape[-4]))


def _position_major(a):
    lead = a.shape[:-2]
    n = len(lead)
    a = a.reshape(lead + (2, KV_HEADS, HEAD_DIM, a.shape[-1]))
    return jnp.transpose(a, tuple(range(n)) + (n + 3, n, n + 1, n + 2))


def kernel(x_prompt, x_sample, cache_cmp_kv, cache_slc_kv, state_win_kv, page_table,
           w_in, cmp_pe, cmp_w1, cmp_b1, cmp_w2, gmlp_ln_g, gmlp_ln_b, gmlp_ws, gmlp_bs,
           w_out, ln1_g, ln1_b, w_gate, w_up, w_down, ln2_g, ln2_b):
    depth = w_in.shape[0]
    b, t, d = x_prompt.shape
    db, s_new, _ = x_sample.shape
    n_pages = page_table.shape[1]
    past = n_pages * PAGE_SIZE
    n_buf = state_win_kv.shape[2]
    kv_tail = (2, KV_HEADS, HEAD_DIM)
    rows_s = SAMPLE_ROWS
    assert t % SLC_KEY_TILE == 0 and t >= WINDOW + QUERY_BLOCK and t // SLC_BLOCK <= MAX_SLC_BLOCKS
    assert s_new <= rows_s and (past + rows_s - 1) // SLC_BLOCK < MAX_SLC_BLOCKS
    assert past % CMP_BLOCK == 0 and past % CMP_BLOCK + s_new < CMP_BLOCK
    assert n_buf % LANES == 0 and n_buf >= LANES
    alpha = (2 * depth) ** 0.25

    tm_p = 512
    tm_s = min(512, db * rows_s)
    seq_per_step = 2 if db % 2 == 0 else 1
    attn_seq_per_step = 4 if db % 4 == 0 else seq_per_step
    cos_p, sin_p = _rope_tables(jnp.arange(t, dtype=jnp.int32))
    cos_s, sin_s = _rope_tables(past + jnp.arange(rows_s, dtype=jnp.int32))
    lane_reps = LANES // HEAD_DIM
    cos_pl, sin_pl = jnp.tile(cos_p, (1, lane_reps)), jnp.tile(sin_p, (1, lane_reps))
    cos_pt, sin_pt = cos_p.T, sin_p.T
    cos_s = jnp.tile(cos_s, (tm_s // rows_s, lane_reps))
    sin_s = jnp.tile(sin_s, (tm_s // rows_s, lane_reps))
    et_p = _block_indicator(t)
    et_s = _block_indicator(past + LANES)
    pool_cmp = _feature_major(cache_cmp_kv)
    pool_slc = _feature_major(cache_slc_kv)
    win_state = _feature_major(state_win_kv)

    yp = x_prompt.reshape(b * t, d)
    ys = jnp.pad(x_sample, ((0, 0), (0, rows_s - s_new), (0, 0))).reshape(db * rows_s, d)
    outs = {k: [] for k in ("s_cmp", "s_slc", "s_v")}
    assert min(WINDOW, t) == tm_p
    new_kv_prompt = ()
    win_next = ()
    cmp_tile = min(128, b * t // CMP_BLOCK)

    for l in range(depth):
        wt = _pack_w_in_t(w_in[l])
        pe_p, w1_p, b1_p, w2_p = _pack_compress(cmp_pe[l], cmp_w1[l], cmp_b1[l], cmp_w2[l])
        ln_g = gmlp_ln_g[l][None, :]
        ln_b = gmlp_ln_b[l][None, :]
        tail = (w_out[l].astype(BF16), ln1_g[l][None, :], ln1_b[l][None, :], w_gate[l].astype(BF16),
                w_up[l].astype(BF16), w_down[l].astype(BF16), ln2_g[l][None, :], ln2_b[l][None, :])

        qraw, qrot, gates, u, v, cmp_kv, *new_kv_prompt, slcb, winb = _inproj_prompt(
            yp, wt, cos_pl, sin_pl, cos_pt, sin_pt, ln_g, ln_b, tm_p, b, t, depth, l, new_kv_prompt)
        ckv = _compress_prompt(cmp_kv.reshape(b * t // CMP_BLOCK, CMP_BLOCK * 2 * KV_WIDTH),
                               pe_p, w1_p, b1_p, w2_p, cmp_tile)
        r3 = lambda a: a.reshape(b, t, a.shape[-1])
        o_nsa = _nsa_prompt(r3(qraw), r3(qrot), r3(gates), ckv.reshape(b, t // SLC_BLOCK, 4 * KV_WIDTH),
                            slcb, winb, et_p)
        gw, gb = _pack_gmlp(gmlp_ws[l], gmlp_bs[l], GMLP_CHUNK)
        yp = _post(yp, o_nsa.reshape(b * t, NSA_WIDTH), u, v, gw, gb, *tail, GMLP_CHUNK, alpha, tm_p)

        qraw, qrot, gates, u, v, cmp_kv, slc_kv, win_kv = _inproj_sample(
            ys, wt, cos_s, sin_s, ln_g, ln_b, tm_s)
        ckv = _compress_sample(pool_cmp, l, page_table, pe_p, w1_p, b1_p, w2_p, attn_seq_per_step)
        s3 = lambda a: a.reshape(db, rows_s, a.shape[-1])
        o_nsa, *win_next = _nsa_sample(pool_slc, win_state, l, page_table, s3(qraw), s3(qrot), s3(gates),
                                       ckv.reshape(db, past // SLC_BLOCK, 4 * KV_WIDTH),
                                       s3(slc_kv), s3(win_kv), et_s, attn_seq_per_step, s_new, win_next)
        gw, gb = _pack_gmlp(gmlp_ws[l], gmlp_bs[l], rows_s)
        ys = _post(ys, o_nsa.reshape(db * rows_s, NSA_WIDTH), u, v, gw, gb, *tail, rows_s, alpha, tm_s)
        new = lambda a: a.reshape((db, rows_s) + a.shape[1:])[:, :s_new]
        outs["s_cmp"].append(new(cmp_kv).reshape((db, s_new) + kv_tail))
        outs["s_slc"].append(new(slc_kv).reshape((db, s_new) + kv_tail))
        outs["s_v"].append(new(v))

    y_sample = ys.reshape(db, rows_s, d)[:, :s_new]
    p_cmp, p_slc, p_win = (_position_major(a) for a in new_kv_prompt)
    return (yp.reshape(b, t, d), y_sample, p_cmp, p_slc, p_win,
            jnp.stack(outs["s_cmp"]), jnp.stack(outs["s_slc"]),
            _position_major(win_next[0]), jnp.stack(outs["s_v"]))
```

```python
import functools
import math

import numpy as np
import jax
import jax.numpy as jnp
from jax import lax
from jax.experimental import pallas as pl
from jax.experimental.pallas import tpu as pltpu

F32 = jnp.float32
BF16 = jnp.bfloat16

LANES = 128
SUBLANES = 8
HEAD_DIM = 64
NSA_HEADS = 8
KV_HEADS = 2
GROUP = NSA_HEADS // KV_HEADS
NSA_WIDTH = NSA_HEADS * HEAD_DIM
KV_WIDTH = KV_HEADS * HEAD_DIM
GMLP_WIDTH = 512
GMLP_GROUPS = 8
GMLP_CHUNK = 128
CMP_BLOCK = 32
CMP_HIDDEN = 256
CMP_PITCH = CMP_BLOCK + SUBLANES
SLC_BLOCK = 64
N_SELECT = 16
WINDOW = 512
QUERY_BLOCK = 128
RANK_TOKENS = LANES
RANK_TILE_GROUP = 4
PAGE_SIZE = 128
FORCE_BONUS = 1.0e4
ROPE_THETA = 10000.0
LN_EPS = 1e-5
NEG_INF = -1e30
MAX_SLC_BLOCKS = LANES
SAMPLE_ROWS = SUBLANES
SLC_KEY_TILE = 512
VMEM_LIMIT = 56 * 1024 * 1024
Q_SCALE = HEAD_DIM ** -0.5 * math.log2(math.e)

C_Q = 0
C_CMP = C_Q + NSA_WIDTH
C_SLC = C_CMP + 2 * KV_WIDTH
C_WIN = C_SLC + 2 * KV_WIDTH
C_GATE = C_WIN + 2 * KV_WIDTH
C_U = C_GATE + LANES
C_V = C_U + GMLP_WIDTH
C_END = C_V + GMLP_WIDTH


def _dot(a, b):
    return jnp.dot(a, b, preferred_element_type=F32)


def _dot_nt(a, b):
    return lax.dot_general(a, b, (((1,), (1,)), ((), ())), preferred_element_type=F32)


def _layer_norm(x, g, b):
    mu = jnp.mean(x, axis=-1, keepdims=True)
    xc = x - mu
    var = jnp.mean(xc * xc, axis=-1, keepdims=True)
    return xc * lax.rsqrt(var + LN_EPS) * g + b


def _softmax_masked(s, mask):
    s = jnp.where(mask, s, NEG_INF)
    m = jnp.max(s, axis=-1, keepdims=True)
    e = jnp.where(mask, jnp.exp2(s - m), 0.0)
    return e * (1.0 / jnp.maximum(jnp.sum(e, axis=-1, keepdims=True), 1e-30))


def _rope_lanes(c, cos, sin, first_half):
    partner = jnp.where(first_half, pltpu.roll(c, LANES - HEAD_DIM // 2, 1),
                        pltpu.roll(c, HEAD_DIM // 2, 1))
    return c * cos + partner * sin


def _inproj_common(xb, wt_ref, cos, sin, first_half, lng_ref, lnb_ref,
                   qraw_ref, qrot_ref, gate_ref, u_ref, v_ref):
    zq = _dot_nt(xb, wt_ref[C_Q:C_CMP, :])
    qraw_ref[...] = (zq * Q_SCALE).astype(qraw_ref.dtype)
    for c in range(NSA_WIDTH // LANES):
        qrot_ref[:, c * LANES:(c + 1) * LANES] = (_rope_lanes(
            zq[:, c * LANES:(c + 1) * LANES], cos, sin, first_half) * Q_SCALE).astype(qrot_ref.dtype)
    gate_ref[...] = jax.nn.sigmoid(_dot_nt(xb, wt_ref[C_GATE:C_U, :]))
    u_ref[...] = jax.nn.gelu(_dot_nt(xb, wt_ref[C_U:C_V, :])).astype(u_ref.dtype)
    zv = jax.nn.gelu(_dot_nt(xb, wt_ref[C_V:C_END, :]))
    v_ref[...] = _layer_norm(zv, lng_ref[...], lnb_ref[...]).astype(v_ref.dtype)


def _inproj_prompt_kernel(n_carried, x_ref, wt_ref, cos_ref, sin_ref, cost_ref, sint_ref, lng_ref,
                          lnb_ref, *refs):
    (qraw_ref, qrot_ref, gate_ref, u_ref, v_ref,
     cmp_ref, cmpt_ref, slct_ref, wint_ref, slcb_ref, winb_ref) = refs[n_carried:]
    xb = x_ref[...].astype(BF16)
    cos = cos_ref[...]
    lane = lax.broadcasted_iota(jnp.int32, cos.shape, 1)
    first_half = (lane % HEAD_DIM) < (HEAD_DIM // 2)
    _inproj_common(xb, wt_ref, cos, sin_ref[...], first_half, lng_ref, lnb_ref,
                   qraw_ref, qrot_ref, gate_ref, u_ref, v_ref)
    cmp_ref[...] = _dot_nt(xb, wt_ref[C_CMP:C_SLC, :])
    cmpt_ref[...] = _dot_nt(wt_ref[C_CMP:C_SLC, :], xb)
    cost = cost_ref[...]
    sint = sint_ref[...]
    half = HEAD_DIM // 2
    for row0, f_ref, b_ref in ((C_SLC, slct_ref, slcb_ref), (C_WIN, wint_ref, winb_ref)):
        zt = _dot_nt(wt_ref[row0:row0 + 2 * KV_WIDTH, :], xb)
        for g in range(KV_HEADS):
            kg = zt[g * HEAD_DIM:(g + 1) * HEAD_DIM, :]
            partner = jnp.concatenate([kg[half:, :], kg[:half, :]], axis=0)
            kr = kg * cost + partner * sint
            f_ref[g * HEAD_DIM:(g + 1) * HEAD_DIM, :] = kr
            b_ref[g * HEAD_DIM:(g + 1) * HEAD_DIM, :] = kr.astype(BF16)
        f_ref[KV_WIDTH:, :] = zt[KV_WIDTH:, :]
        b_ref[KV_WIDTH:, :] = zt[KV_WIDTH:, :].astype(BF16)


def _inproj_sample_kernel(x_ref, wt_ref, cos_ref, sin_ref, lng_ref, lnb_ref,
                          qraw_ref, qrot_ref, gate_ref, u_ref, v_ref, cmp_ref, slc_ref, win_ref):
    xb = x_ref[...].astype(BF16)
    cos = cos_ref[...]
    sin = sin_ref[...]
    lane = lax.broadcasted_iota(jnp.int32, cos.shape, 1)
    first_half = (lane % HEAD_DIM) < (HEAD_DIM // 2)
    _inproj_common(xb, wt_ref, cos, sin, first_half, lng_ref, lnb_ref,
                   qraw_ref, qrot_ref, gate_ref, u_ref, v_ref)
    cmp_ref[...] = _dot_nt(xb, wt_ref[C_CMP:C_SLC, :])
    for row0, f_ref in ((C_SLC, slc_ref), (C_WIN, win_ref)):
        z = _dot_nt(xb, wt_ref[row0:row0 + 2 * KV_WIDTH, :])
        f_ref[:, :KV_WIDTH] = _rope_lanes(z[:, :KV_WIDTH], cos, sin, first_half)
        f_ref[:, KV_WIDTH:] = z[:, KV_WIDTH:]


def _inproj_prompt(x, wt, cos, sin, cost, sint, ln_g, ln_b, tm, b, t, depth, layer, carried):
    n, d = x.shape
    n_tab = t // tm
    row = lambda width: pl.BlockSpec((tm, width), lambda i: (i, 0))
    const = lambda a: pl.BlockSpec(a.shape, lambda i: (0,) * a.ndim)
    tab = pl.BlockSpec((tm, LANES), lambda i: (i % n_tab, 0))
    tabt = pl.BlockSpec((HEAD_DIM, tm), lambda i: (0, i % n_tab))
    fm = pl.BlockSpec((None, 2 * KV_WIDTH, tm), lambda i: (i // n_tab, 0, i % n_tab))
    fm_layer = pl.BlockSpec((None, None, 2 * KV_WIDTH, tm), lambda i: (layer, i // n_tab, 0, i % n_tab))
    keep_layer = pl.BlockSpec((None, None, 2 * KV_WIDTH, tm), lambda i: (layer, i // n_tab, 0, 0))
    row_out = ((NSA_WIDTH, BF16), (NSA_WIDTH, BF16), (LANES, F32), (GMLP_WIDTH, BF16), (GMLP_WIDTH, BF16),
               (2 * KV_WIDTH, F32))
    n_in = 8
    n_row = len(row_out)
    stacked = lambda width: jax.ShapeDtypeStruct((depth, b, 2 * KV_WIDTH, width), F32)
    return pl.pallas_call(
        functools.partial(_inproj_prompt_kernel, len(carried)),
        grid=(n // tm,),
        in_specs=[row(d), const(wt), tab, tab, tabt, tabt, const(ln_g), const(ln_b)]
        + [pl.BlockSpec(memory_space=pl.ANY)] * len(carried),
        out_specs=[row(wd) for wd, _ in row_out] + [fm_layer, fm_layer, keep_layer, fm, fm],
        out_shape=[jax.ShapeDtypeStruct((n, wd), dt) for wd, dt in row_out]
        + [stacked(t), stacked(t), stacked(tm)]
        + [jax.ShapeDtypeStruct((b, 2 * KV_WIDTH, t), BF16)] * 2,
        input_output_aliases={n_in + k: n_row + k for k in range(len(carried))},
        compiler_params=pltpu.CompilerParams(dimension_semantics=("arbitrary",),
                                             vmem_limit_bytes=VMEM_LIMIT),
        name="inproj_prompt",
    )(x, wt, cos, sin, cost, sint, ln_g, ln_b, *carried)


def _inproj_sample(x, wt, cos, sin, ln_g, ln_b, tm):
    n, d = x.shape
    row = lambda width: pl.BlockSpec((tm, width), lambda i: (i, 0))
    const = lambda a: pl.BlockSpec(a.shape, lambda i: (0,) * a.ndim)
    widths = (NSA_WIDTH, NSA_WIDTH, LANES, GMLP_WIDTH, GMLP_WIDTH,
              2 * KV_WIDTH, 2 * KV_WIDTH, 2 * KV_WIDTH)
    return pl.pallas_call(
        _inproj_sample_kernel,
        grid=(n // tm,),
        in_specs=[row(d), const(wt), const(cos), const(sin), const(ln_g), const(ln_b)],
        out_specs=[row(wd) for wd in widths],
        out_shape=[jax.ShapeDtypeStruct((n, wd), F32) for wd in widths],
        compiler_params=pltpu.CompilerParams(dimension_semantics=("parallel",),
                                             vmem_limit_bytes=VMEM_LIMIT),
        name="inproj_sample",
    )(x, wt, cos, sin, ln_g, ln_b)


def _compress_rows(chunk, pe_ref, w1_ref, b1_ref, w2_ref):
    outs = []
    for k in range(2):
        a = jnp.concatenate([chunk(l, k) for l in range(CMP_BLOCK)], axis=1)
        a = (a + pe_ref[k]).astype(BF16)
        h = jax.nn.silu(_dot(a, w1_ref[k]) + b1_ref[k])
        outs.append(_dot(h.astype(BF16), w2_ref[k]))
    return jnp.concatenate(outs, axis=1)


def _compress_prompt_kernel(x_ref, pe_ref, w1_ref, b1_ref, w2_ref, o_ref, posk_ref, posv_ref):
    tm = o_ref.shape[0]
    gap = jnp.zeros((tm, CMP_PITCH - CMP_BLOCK, KV_WIDTH), F32)
    for k, dst in enumerate((posk_ref, posv_ref)):
        rows = x_ref[:, k * KV_WIDTH:(k + 1) * KV_WIDTH].reshape(tm, CMP_BLOCK, KV_WIDTH)
        dst[...] = jnp.concatenate([rows, gap], axis=1).reshape(tm * CMP_PITCH, KV_WIDTH)
    chunk = lambda l, k: (posk_ref, posv_ref)[k][pl.ds(l, tm, stride=CMP_PITCH), :]
    o_ref[...] = _compress_rows(chunk, pe_ref, w1_ref, b1_ref, w2_ref)


def _compress_prompt(x, pe, w1, b1, w2, tm):
    n = x.shape[0] // CMP_BLOCK
    const = lambda a: pl.BlockSpec(a.shape, lambda i: (0,) * a.ndim)
    return pl.pallas_call(
        _compress_prompt_kernel,
        grid=(n // tm,),
        in_specs=[pl.BlockSpec((tm * CMP_BLOCK, x.shape[1]), lambda i: (i, 0)),
                  const(pe), const(w1), const(b1), const(w2)],
        out_specs=pl.BlockSpec((tm, 2 * KV_WIDTH), lambda i: (i, 0)),
        out_shape=jax.ShapeDtypeStruct((n, 2 * KV_WIDTH), F32),
        scratch_shapes=[pltpu.VMEM((tm * CMP_PITCH, KV_WIDTH), F32)] * 2,
        compiler_params=pltpu.CompilerParams(dimension_semantics=("parallel",),
                                             vmem_limit_bytes=VMEM_LIMIT),
        name="compress_prompt",
    )(x, pe, w1, b1, w2)


def _compress_sample_kernel(n_seq, n_pages, pt_ref, *refs):
    page_refs = refs[:n_seq * n_pages]
    pe_ref, w1_ref, b1_ref, w2_ref, o_ref, posk_ref, posv_ref = refs[n_seq * n_pages:]
    blocks_per_page = PAGE_SIZE // CMP_BLOCK
    for j, page_ref in enumerate(page_refs):
        for dst, rows in ((posk_ref, page_ref[:KV_WIDTH, :].T), (posv_ref, page_ref[KV_WIDTH:, :].T)):
            for n in range(blocks_per_page):
                r0 = (j * blocks_per_page + n) * CMP_PITCH
                dst[r0:r0 + CMP_BLOCK, :] = rows[n * CMP_BLOCK:(n + 1) * CMP_BLOCK, :]
    m = n_seq * n_pages * blocks_per_page
    chunk = lambda l, k: (posk_ref, posv_ref)[k][pl.ds(l, m, stride=CMP_PITCH), :]
    o_ref[...] = _compress_rows(chunk, pe_ref, w1_ref, b1_ref, w2_ref)


def _compress_sample(pool, layer, page_table, pe, w1, b1, w2, n_seq):
    db, n_pages = page_table.shape
    rows = n_seq * n_pages * (PAGE_SIZE // CMP_BLOCK)
    const = lambda a: pl.BlockSpec(a.shape, lambda i, pt: (0,) * a.ndim)

    def page_spec(s, p):
        return pl.BlockSpec((None, None) + pool.shape[2:],
                            lambda i, pt: (layer, pt[i * n_seq + s, p], 0, 0))

    grid_spec = pltpu.PrefetchScalarGridSpec(
        num_scalar_prefetch=1,
        grid=(db // n_seq,),
        in_specs=[page_spec(s, p) for s in range(n_seq) for p in range(n_pages)]
        + [const(pe), const(w1), const(b1), const(w2)],
        out_specs=pl.BlockSpec((rows, 2 * KV_WIDTH), lambda i, pt: (i, 0)),
        scratch_shapes=[pltpu.VMEM((rows * CMP_PITCH, KV_WIDTH), F32)] * 2,
    )
    return pl.pallas_call(
        functools.partial(_compress_sample_kernel, n_seq, n_pages),
        grid_spec=grid_spec,
        out_shape=jax.ShapeDtypeStruct((db * n_pages * (PAGE_SIZE // CMP_BLOCK), 2 * KV_WIDTH), F32),
        compiler_params=pltpu.CompilerParams(dimension_semantics=("parallel",),
                                             vmem_limit_bytes=VMEM_LIMIT),
        name="compress_sample",
    )(page_table, *([pool] * (n_seq * n_pages)), pe, w1, b1, w2)


def _stack_heads(q, tq):
    lane_lo = lax.broadcasted_iota(jnp.int32, (tq, LANES), 1) < HEAD_DIM
    pieces = []
    for hh in range(NSA_HEADS):
        g = hh // GROUP
        c = hh // 2
        chunk = q[:, c * LANES:(c + 1) * LANES]
        if hh % 2 != g:
            chunk = pltpu.roll(chunk, HEAD_DIM, 1)
        pieces.append(jnp.where(lane_lo if g == 0 else jnp.logical_not(lane_lo), chunk, 0.0))
    return jnp.concatenate(pieces, axis=0).astype(BF16)


def _cmp_branch(qraw, ckv, qpos, tq):
    nb = MAX_SLC_BLOCKS
    kc = jnp.concatenate([ckv[:, 0:KV_WIDTH], ckv[:, 2 * KV_WIDTH:3 * KV_WIDTH]], axis=0).astype(BF16)
    vc = jnp.concatenate([ckv[:, KV_WIDTH:2 * KV_WIDTH], ckv[:, 3 * KV_WIDTH:]], axis=0).astype(BF16)
    s = _dot_nt(qraw, kc).reshape(NSA_HEADS, tq, 2 * nb)
    lane = lax.broadcasted_iota(jnp.int32, (tq, 2 * nb), 1)
    blk = jnp.where(lane < nb, 2 * lane, 2 * (lane - nb) + 1)
    mask = ((blk + 1) * CMP_BLOCK - 1) <= qpos
    p = _softmax_masked(s, mask[None])
    o_cmp = _dot(p.reshape(NSA_HEADS * tq, 2 * nb).astype(BF16), vc)
    scores = []
    sblk = lax.broadcasted_iota(jnp.int32, (tq, nb), 1)
    cur = qpos // SLC_BLOCK
    valid = sblk <= cur
    forced = ((sblk == 0) | (sblk == cur) | (sblk == cur - 1)).astype(F32)
    for g in range(KV_HEADS):
        imp = p[g * GROUP]
        for r in range(1, GROUP):
            imp = imp + p[g * GROUP + r]
        imp = imp[:, :nb] + imp[:, nb:]
        scores.append(jnp.where(valid, imp + FORCE_BONUS * forced, -1.0))
    return o_cmp, scores, valid


def _rank_blocks_wide(score, st_ref, cnt_ref, n_chunks):
    st_ref[...] = score.T
    cnt_ref[...] = jnp.zeros_like(cnt_ref)
    sub = lax.broadcasted_iota(jnp.int32, (SUBLANES, LANES), 0)
    n_vregs = MAX_SLC_BLOCKS // SUBLANES
    def count(c, v_lo, v_hi):
        cand = st_ref[c * SUBLANES:(c + 1) * SUBLANES, :]
        tiles = {v: st_ref[v * SUBLANES:(v + 1) * SUBLANES, :] for v in range(v_lo, v_hi)}
        cnts = {v: cnt_ref[v * SUBLANES:(v + 1) * SUBLANES, :] for v in range(v_lo, v_hi)}
        for ii in range(SUBLANES):
            row = cand[ii:ii + 1, :]
            for v in range(v_lo, v_hi):
                if v < c:
                    inc = jnp.where(row > tiles[v], 1.0, 0.0)
                elif v > c:
                    inc = jnp.where(row >= tiles[v], 1.0, 0.0)
                else:
                    inc = jnp.where(sub > ii, jnp.where(row >= tiles[v], 1.0, 0.0),
                                    jnp.where(row > tiles[v], 1.0, 0.0))
                cnts[v] = cnts[v] + inc
        for v in range(v_lo, v_hi):
            cnt_ref[v * SUBLANES:(v + 1) * SUBLANES, :] = cnts[v]

    for c in range(n_vregs):
        @pl.when(c < n_chunks)
        def _():
            for v_lo in range(0, n_vregs, RANK_TILE_GROUP):
                v_hi = v_lo + RANK_TILE_GROUP
                if v_lo <= c:
                    count(c, v_lo, v_hi)
                else:
                    pl.when(v_lo < n_chunks)(functools.partial(count, c, v_lo, v_hi))
    return cnt_ref[...].T


def _rank_blocks_narrow(score, n_blocks):
    lane = lax.broadcasted_iota(jnp.int32, score.shape, 1)
    cnt = jnp.zeros(score.shape, F32)
    for i in range(n_blocks):
        col = score[:, i:i + 1]
        cnt = cnt + jnp.where(lane > i, jnp.where(col >= score, 1.0, 0.0),
                              jnp.where(col > score, 1.0, 0.0))
    return cnt


def _slc_branch(q_aug, kv_ref, et_ref, n_full, qpos, tq, tk):
    ones_rows = jnp.where(lax.broadcasted_iota(jnp.int32, (HEAD_DIM, tk), 0) == 0, 1.0, 0.0).astype(BF16)

    def tile(kt, carry, diagonal):
        ms, accs = carry
        off = pl.multiple_of(kt * tk, tk)
        ka = jnp.concatenate([et_ref[:, pl.ds(off, tk)], kv_ref[:KV_WIDTH, pl.ds(off, tk)]], axis=0)
        s = _dot(q_aug, ka)
        if diagonal:
            visible = (off + lax.broadcasted_iota(jnp.int32, (tq, tk), 1)) <= qpos
        ps, m_new, alphas = [], [], []
        for hh in range(NSA_HEADS):
            s_h = s[hh * tq:(hh + 1) * tq, :]
            if diagonal:
                s_h = jnp.where(visible, s_h, NEG_INF)
            m_h = jnp.maximum(ms[hh], jnp.max(s_h, axis=-1, keepdims=True))
            ps.append(jnp.exp2(s_h - m_h).astype(BF16))
            alphas.append(jnp.exp2(ms[hh] - m_h))
            m_new.append(m_h)
        acc_new = []
        for g in range(KV_HEADS):
            v0 = KV_WIDTH + g * HEAD_DIM
            va = jnp.concatenate([kv_ref[v0:v0 + HEAD_DIM, pl.ds(off, tk)], ones_rows], axis=0)
            pv = _dot_nt(jnp.concatenate(ps[g * GROUP:(g + 1) * GROUP], axis=0), va)
            alpha = jnp.concatenate(alphas[g * GROUP:(g + 1) * GROUP], axis=0)
            acc_new.append(alpha * accs[g] + pv)
        return tuple(m_new), tuple(acc_new)

    init = (tuple(jnp.full((tq, 1), NEG_INF, F32) for _ in range(NSA_HEADS)),
            tuple(jnp.zeros((GROUP * tq, KV_WIDTH), F32) for _ in range(KV_HEADS)))
    def first_group(count):
        def run(c):
            c = tile(n_full, c, True)
            for u in range(count - 1):
                c = tile(u, c, False)
            return c
        return run

    first = (n_full + 1) % 4
    carry = lax.switch(first, [first_group(4), first_group(1), first_group(2), first_group(3)], init)
    taken = jnp.where(first == 0, 3, first - 1)

    def quad(j, c):
        kt = taken + 4 * j
        for u in range(4):
            c = tile(kt + u, c, False)
        return c

    _, accs = lax.fori_loop(0, (n_full - taken) // 4, quad, carry)
    outs = [a * (1.0 / jnp.maximum(a[:, HEAD_DIM:HEAD_DIM + 1], 1e-30)) for a in accs]
    return jnp.concatenate([outs[0], pltpu.roll(outs[1], HEAD_DIM, 1)], axis=0)


def _softmax_pv(s, mask, vt, tq):
    nk = s.shape[1]
    rows = NSA_HEADS * tq
    if rows <= LANES:
        s3 = jnp.where(mask[None], s.reshape(NSA_HEADS, tq, nk), NEG_INF)
        p = jnp.exp2(s3 - jnp.max(s3, axis=-1, keepdims=True)).reshape(rows, nk).astype(BF16)
        if rows < LANES:
            p = jnp.concatenate([p, jnp.zeros((LANES - rows, nk), BF16)], axis=0)
        ones_rows = jnp.where(lax.broadcasted_iota(jnp.int32, (2 * SUBLANES, nk), 0) == 0,
                              1.0, 0.0).astype(BF16)
        ot = _dot_nt(jnp.concatenate([vt, ones_rows], axis=0), p)
        return (ot[:KV_WIDTH] * (1.0 / jnp.maximum(ot[KV_WIDTH:KV_WIDTH + 1], 1e-30))).T[:rows]
    ones_rows = jnp.where(lax.broadcasted_iota(jnp.int32, (HEAD_DIM, nk), 0) == 0, 1.0, 0.0).astype(BF16)
    outs = []
    for g in range(KV_HEADS):
        ps = []
        for hh in range(g * GROUP, (g + 1) * GROUP):
            s_h = jnp.where(mask, s[hh * tq:(hh + 1) * tq, :], NEG_INF)
            ps.append(jnp.exp2(s_h - jnp.max(s_h, axis=-1, keepdims=True)).astype(BF16))
        va = jnp.concatenate([vt[g * HEAD_DIM:(g + 1) * HEAD_DIM, :], ones_rows], axis=0)
        pv = _dot_nt(jnp.concatenate(ps, axis=0), va)
        outs.append(pv * (1.0 / jnp.maximum(pv[:, HEAD_DIM:HEAD_DIM + 1], 1e-30)))
    return jnp.concatenate([outs[0], pltpu.roll(outs[1], HEAD_DIM, 1)], axis=0)


def _slc_single(bias_rows, qrot, kt, vt, et, qpos, tq):
    nk = kt.shape[1]
    s = _dot(bias_rows, et) + _dot(qrot, kt)
    visible = lax.broadcasted_iota(jnp.int32, (tq, nk), 1) <= qpos
    return _softmax_pv(s, visible, vt, tq)


def _win_branch(qrot, kt, vt, kpos, qpos, tq):
    diff = qpos - kpos
    return _softmax_pv(_dot(qrot, kt), (diff >= 0) & (diff < WINDOW), vt, tq)


def _combine(o_cmp, o_slc, o_win, gates, tq):
    lane_lo = lax.broadcasted_iota(jnp.int32, (tq, LANES), 1) < HEAD_DIM
    heads = []
    for hh in range(NSA_HEADS):
        tot = None
        for br, o in enumerate((o_cmp, o_slc, o_win)):
            col = br * NSA_HEADS + hh
            term = gates[:, col:col + 1] * o[hh * tq:(hh + 1) * tq, :]
            tot = term if tot is None else tot + term
        heads.append(tot)
    chunks = []
    for c in range(NSA_HEADS // 2):
        a, b = heads[2 * c], heads[2 * c + 1]
        if c // 2 == 0:
            chunks.append(jnp.where(lane_lo, a, pltpu.roll(b, HEAD_DIM, 1)))
        else:
            chunks.append(jnp.where(lane_lo, pltpu.roll(a, HEAD_DIM, 1), b))
    return jnp.concatenate(chunks, axis=1)


def _selection_bias(rank, valid):
    return jnp.where((rank < float(N_SELECT)) & valid, 0.0, NEG_INF).astype(BF16)


def _pad_ckv(ckv):
    n = ckv.shape[0]
    if n == MAX_SLC_BLOCKS:
        return ckv
    return jnp.concatenate([ckv, jnp.zeros((MAX_SLC_BLOCKS - n, ckv.shape[1]), F32)], axis=0)


def _nsa_prompt_kernel(qraw_ref, qrot_ref, gate_ref, ckv_ref, slc_ref, win_ref, et_ref,
                       o_ref, st_ref, cnt_ref):
    tq = QUERY_BLOCK
    i = pl.program_id(1)
    qpos0 = i * tq
    qpos = qpos0 + lax.broadcasted_iota(jnp.int32, (tq, 1), 0)
    qraw = _stack_heads(qraw_ref[...].astype(F32), tq)
    qrot = _stack_heads(qrot_ref[...].astype(F32), tq)

    o_cmp, scores, valid = _cmp_branch(qraw, _pad_ckv(ckv_ref[...]), qpos, tq)
    n_chunks = (qpos0 + tq - 1) // (SLC_BLOCK * SUBLANES) + 1
    bias = []
    for g in range(KV_HEADS):
        parts = []
        for t0 in range(0, tq, RANK_TOKENS):
            rank = _rank_blocks_wide(scores[g][t0:t0 + RANK_TOKENS], st_ref, cnt_ref, n_chunks)
            parts.append(_selection_bias(rank, valid[t0:t0 + RANK_TOKENS]))
        bias.append(jnp.concatenate(parts, axis=0))
    bias_rows = jnp.concatenate([bias[0]] * GROUP + [bias[1]] * GROUP, axis=0)
    q_aug = jnp.concatenate([bias_rows, qrot], axis=1)
    o_slc = _slc_branch(q_aug, slc_ref, et_ref, qpos0 // SLC_KEY_TILE, qpos, tq, SLC_KEY_TILE)

    nk = WINDOW + tq
    start = pl.multiple_of(jnp.maximum(qpos0 - WINDOW, 0), tq)
    kpos = start + lax.broadcasted_iota(jnp.int32, (1, nk), 1)
    o_win = _win_branch(qrot, win_ref[:KV_WIDTH, pl.ds(start, nk)], win_ref[KV_WIDTH:, pl.ds(start, nk)],
                        kpos, qpos, tq)

    o_ref[...] = _combine(o_cmp, o_slc, o_win, gate_ref[...], tq).astype(o_ref.dtype)


def _nsa_prompt(qraw, qrot, gates, ckv, slcb, winb, et):
    b, t, _ = qraw.shape
    tq = QUERY_BLOCK
    blk = lambda width: pl.BlockSpec((None, tq, width), lambda bi, i: (bi, i, 0))
    per_batch = lambda a: pl.BlockSpec((None,) + a.shape[1:], lambda bi, i: (bi, 0, 0))
    return pl.pallas_call(
        _nsa_prompt_kernel,
        grid=(b, t // tq),
        in_specs=[blk(NSA_WIDTH), blk(NSA_WIDTH), blk(LANES), per_batch(ckv), per_batch(slcb),
                  per_batch(winb), pl.BlockSpec(et.shape, lambda bi, i: (0, 0))],
        out_specs=blk(NSA_WIDTH),
        out_shape=jax.ShapeDtypeStruct((b, t, NSA_WIDTH), BF16),
        scratch_shapes=[pltpu.VMEM((MAX_SLC_BLOCKS, RANK_TOKENS), F32)] * 2,
        compiler_params=pltpu.CompilerParams(dimension_semantics=("parallel", "arbitrary"),
                                             vmem_limit_bytes=VMEM_LIMIT),
        name="nsa_prompt",
    )(qraw, qrot, gates, ckv, slcb, winb, et)


def _nsa_sample_kernel(n_seq, n_pages, past, s_new, pt_ref, *refs):
    n_in = n_seq * n_pages
    page_refs = refs[:n_in]
    (qraw_ref, qrot_ref, gate_ref, ckv_ref, slcn_ref, wins_ref, winn_ref, et_ref) = refs[n_in:n_in + 8]
    o_ref, wout_ref = refs[-2:]
    tq = SAMPLE_ROWS
    qpos = past + lax.broadcasted_iota(jnp.int32, (tq, 1), 0)
    n_buf = wins_ref.shape[2]
    n_blocks = (past + tq - 1) // SLC_BLOCK + 1
    pad_rows = jnp.zeros((LANES - tq, 2 * KV_WIDTH), F32)
    wpos = past - n_buf + lax.broadcasted_iota(jnp.int32, (1, n_buf + LANES), 1)
    lane = lax.broadcasted_iota(jnp.int32, (2 * KV_WIDTH, LANES), 1)
    for s in range(n_seq):
        slc_new = jnp.concatenate([slcn_ref[s], pad_rows], axis=0).T
        tiles = [page_refs[s * n_pages + p][...].astype(BF16) for p in range(n_pages)]
        tiles.append(slc_new.astype(BF16))
        kt = jnp.concatenate([tl[:KV_WIDTH] for tl in tiles], axis=1)
        vt = jnp.concatenate([tl[KV_WIDTH:] for tl in tiles], axis=1)

        qraw = _stack_heads(qraw_ref[s], tq)
        qrot = _stack_heads(qrot_ref[s], tq)
        o_cmp, scores, valid = _cmp_branch(qraw, _pad_ckv(ckv_ref[s]), qpos, tq)
        bias = [_selection_bias(_rank_blocks_narrow(scores[g], n_blocks), valid)
                for g in range(KV_HEADS)]
        bias_rows = jnp.concatenate([bias[0]] * GROUP + [bias[1]] * GROUP, axis=0)
        o_slc = _slc_single(bias_rows, qrot, kt, vt, et_ref[...], qpos, tq)

        state = wins_ref[s]
        win_new = jnp.concatenate([winn_ref[s], pad_rows], axis=0).T
        kt = jnp.concatenate([state[:KV_WIDTH], win_new[:KV_WIDTH]], axis=1).astype(BF16)
        vt = jnp.concatenate([state[KV_WIDTH:], win_new[KV_WIDTH:]], axis=1).astype(BF16)
        o_win = _win_branch(qrot, kt, vt, wpos, qpos, tq)
        o_ref[s] = _combine(o_cmp, o_slc, o_win, gate_ref[s], tq)

        shifted = pltpu.roll(state, n_buf - s_new, 1)
        tail = jnp.where(lane >= LANES - s_new, pltpu.roll(win_new, LANES - s_new, 1),
                         shifted[:, n_buf - LANES:])
        wout_ref[s, :, :n_buf - LANES] = shifted[:, :n_buf - LANES]
        wout_ref[s, :, n_buf - LANES:] = tail


def _nsa_sample(pool, win_state, layer, page_table, qraw, qrot, gates, ckv, slc_new, win_new, et,
                n_seq, s_new, carried):
    db, n_pages = page_table.shape
    past = n_pages * PAGE_SIZE
    tq = SAMPLE_ROWS
    n_buf = win_state.shape[3]
    seq = lambda a: pl.BlockSpec((n_seq,) + a.shape[1:], lambda i, pt: (i, 0, 0))

    def page_spec(s, p):
        return pl.BlockSpec((None, None) + pool.shape[2:],
                            lambda i, pt: (layer, pt[i * n_seq + s, p], 0, 0))

    state_spec = pl.BlockSpec((None, n_seq) + win_state.shape[2:], lambda i, pt: (layer, i, 0, 0))
    grid_spec = pltpu.PrefetchScalarGridSpec(
        num_scalar_prefetch=1,
        grid=(db // n_seq,),
        in_specs=[page_spec(s, p) for s in range(n_seq) for p in range(n_pages)]
        + [seq(qraw), seq(qrot), seq(gates), seq(ckv), seq(slc_new), state_spec, seq(win_new),
           pl.BlockSpec(et.shape, lambda i, pt: (0, 0))]
        + [pl.BlockSpec(memory_space=pl.ANY)] * len(carried),
        out_specs=[pl.BlockSpec((n_seq, tq, NSA_WIDTH), lambda i, pt: (i, 0, 0)),
                   pl.BlockSpec((None, n_seq) + win_state.shape[2:], lambda i, pt: (layer, i, 0, 0))],
    )
    n_operands = 1 + n_seq * n_pages + 8
    return pl.pallas_call(
        functools.partial(_nsa_sample_kernel, n_seq, n_pages, past, s_new),
        grid_spec=grid_spec,
        out_shape=[jax.ShapeDtypeStruct((db, tq, NSA_WIDTH), F32),
                   jax.ShapeDtypeStruct(win_state.shape, F32)],
        input_output_aliases={n_operands + k: 1 + k for k in range(len(carried))},
        compiler_params=pltpu.CompilerParams(dimension_semantics=("parallel",),
                                             vmem_limit_bytes=VMEM_LIMIT),
        name="nsa_sample",
    )(page_table, *([pool] * (n_seq * n_pages)), qraw, qrot, gates, ckv, slc_new, win_state, win_new, et,
      *carried)


def _post_kernel(chunk, alpha, n_ffn_chunks, x_ref, o_ref, u_ref, v_ref, gw_ref, gb_ref,
                 wout_ref, ln1g_ref, ln1b_ref, wg_ref, wu_ref, wd_ref, ln2g_ref, ln2b_ref, y_ref):
    tm = x_ref.shape[0]
    c = GMLP_CHUNK
    row = lax.broadcasted_iota(jnp.int32, (c, c), 0)
    col = lax.broadcasted_iota(jnp.int32, (c, c), 1)
    causal = (col <= row) & ((row // chunk) == (col // chunk))
    w_cat = jnp.concatenate([jnp.where(causal, gw_ref[h], 0.0) for h in range(GMLP_GROUPS)],
                            axis=1).astype(BF16)
    lane_group = lax.broadcasted_iota(jnp.int32, (c, GMLP_WIDTH), 1) // (GMLP_WIDTH // GMLP_GROUPS)
    mixed = []
    for sub in range(tm // c):
        v = v_ref[sub * c:(sub + 1) * c, :]
        v_diag = jnp.concatenate([jnp.where(lane_group == h, v, 0.0) for h in range(GMLP_GROUPS)],
                                 axis=0).astype(BF16)
        s = _dot(w_cat, v_diag) + gb_ref[...]
        mixed.append(u_ref[sub * c:(sub + 1) * c, :].astype(F32) * s)
    o_gmlp = jnp.concatenate(mixed, axis=0)
    mix = jnp.concatenate([o_ref[...], o_gmlp], axis=1).astype(BF16)
    h = _dot(mix, wout_ref[...])
    x1 = _layer_norm(alpha * x_ref[...] + h, ln1g_ref[...], ln1b_ref[...])
    x1b = x1.astype(BF16)
    f = None
    fc = wg_ref.shape[1] // n_ffn_chunks
    for j in range(n_ffn_chunks):
        gate = _dot(x1b, wg_ref[:, j * fc:(j + 1) * fc])
        up = _dot(x1b, wu_ref[:, j * fc:(j + 1) * fc])
        a = (jax.nn.silu(gate) * up).astype(BF16)
        part = _dot(a, wd_ref[j * fc:(j + 1) * fc, :])
        f = part if f is None else f + part
    y_ref[...] = _layer_norm(alpha * x1 + f, ln2g_ref[...], ln2b_ref[...])


def _post(x, o_nsa, u, v, gw, gb, wout, ln1g, ln1b, wg, wu, wd, ln2g, ln2b, chunk, alpha, tm):
    n, d = x.shape
    ffn = wg.shape[1]
    n_ffn_chunks = next(k for k in (4, 2, 1) if ffn % (k * LANES) == 0)
    row = lambda width: pl.BlockSpec((tm, width), lambda i: (i, 0))
    const = lambda a: pl.BlockSpec(a.shape, lambda i: (0,) * a.ndim, pipeline_mode=pl.Buffered(1))
    consts = (gw, gb, wout, ln1g, ln1b, wg, wu, wd, ln2g, ln2b)
    return pl.pallas_call(
        functools.partial(_post_kernel, chunk, alpha, n_ffn_chunks),
        grid=(n // tm,),
        in_specs=[row(d), row(NSA_WIDTH), row(GMLP_WIDTH), row(GMLP_WIDTH)] + [const(a) for a in consts],
        out_specs=row(d),
        out_shape=jax.ShapeDtypeStruct((n, d), F32),
        compiler_params=pltpu.CompilerParams(dimension_semantics=("parallel",),
                                             vmem_limit_bytes=VMEM_LIMIT),
        name="post",
    )(x, o_nsa, u, v, *consts)


def _pack_w_in_t(w_in):
    wt = w_in.T
    o_gate = NSA_WIDTH + 3 * 2 * KV_WIDTH
    o_gmlp = o_gate + 3 * NSA_HEADS
    gate = jnp.pad(wt[o_gate:o_gmlp], ((0, LANES - 3 * NSA_HEADS), (0, 0)))
    return jnp.concatenate([wt[:o_gate], gate, wt[o_gmlp:]], axis=0).astype(BF16)


def _pack_compress(pe, w1, b1, w2):
    eye = jnp.eye(KV_HEADS, dtype=F32)
    zero = jnp.zeros_like(w1)
    w1p = jnp.stack([jnp.concatenate([w1, zero], axis=-1), jnp.concatenate([zero, w1], axis=-1)], axis=2)
    w1p = w1p.reshape(2, CMP_BLOCK * KV_WIDTH, KV_HEADS * CMP_HIDDEN)
    pep = jnp.broadcast_to(pe[:, :, None, :], (2, CMP_BLOCK, KV_HEADS, HEAD_DIM)).reshape(2, 1, -1)
    b1p = jnp.tile(b1, (1, KV_HEADS))[:, None, :]
    w2p = jnp.einsum('khd,gG->kghGd', w2, eye).reshape(2, KV_HEADS * CMP_HIDDEN, KV_WIDTH)
    return pep, w1p.astype(BF16), b1p, w2p.astype(BF16)


def _pack_gmlp(ws, bs, chunk):
    reps = GMLP_CHUNK // chunk
    gw = jnp.tile(ws[:, :chunk, :chunk], (1, reps, reps))
    gb = jnp.tile(jnp.repeat(bs[:, :chunk].T, GMLP_WIDTH // GMLP_GROUPS, axis=1), (reps, 1))
    return gw, gb


def _rope_tables(pos):
    half = HEAD_DIM // 2
    inv_freq = ROPE_THETA ** (-jnp.arange(half, dtype=F32) / half)
    ang = pos.astype(F32)[:, None] * inv_freq[None, :]
    cos, sin = jnp.cos(ang), jnp.sin(ang)
    return jnp.concatenate([cos, cos], axis=1), jnp.concatenate([-sin, sin], axis=1)


def _block_indicator(n_keys):
    key_block = np.arange(n_keys)[None, :] // SLC_BLOCK
    return jnp.asarray(key_block == np.arange(MAX_SLC_BLOCKS)[:, None], dtype=BF16)


def _feature_major(a):
    lead = a.shape[:-4]
    n = len(lead)
    perm = tuple(range(n)) + (n + 1, n + 2, n + 3, n)
    return jnp.transpose(a, perm).reshape(lead + (2 * KV_WIDTH, a.shape[-4]))


def _position_major(a):
    lead = a.shape[:-2]
    n = len(lead)
    a = a.reshape(lead + (2, KV_HEADS, HEAD_DIM, a.shape[-1]))
    return jnp.transpose(a, tuple(range(n)) + (n + 3, n, n + 1, n + 2))


def kernel(x_prompt, x_sample, cache_cmp_kv, cache_slc_kv, state_win_kv, page_table,
           w_in, cmp_pe, cmp_w1, cmp_b1, cmp_w2, gmlp_ln_g, gmlp_ln_b, gmlp_ws, gmlp_bs,
           w_out, ln1_g, ln1_b, w_gate, w_up, w_down, ln2_g, ln2_b):
    depth = w_in.shape[0]
    b, t, d = x_prompt.shape
    db, s_new, _ = x_sample.shape
    n_pages = page_table.shape[1]
    past = n_pages * PAGE_SIZE
    n_buf = state_win_kv.shape[2]
    kv_tail = (2, KV_HEADS, HEAD_DIM)
    rows_s = SAMPLE_ROWS
    assert t % SLC_KEY_TILE == 0 and t >= WINDOW + QUERY_BLOCK and t // SLC_BLOCK <= MAX_SLC_BLOCKS
    assert s_new <= rows_s and (past + rows_s - 1) // SLC_BLOCK < MAX_SLC_BLOCKS
    assert past % CMP_BLOCK == 0 and past % CMP_BLOCK + s_new < CMP_BLOCK
    assert n_buf % LANES == 0 and n_buf >= LANES
    alpha = (2 * depth) ** 0.25

    tm_p = 512
    tm_s = min(512, db * rows_s)
    seq_per_step = 2 if db % 2 == 0 else 1
    attn_seq_per_step = 4 if db % 4 == 0 else seq_per_step
    cos_p, sin_p = _rope_tables(jnp.arange(t, dtype=jnp.int32))
    cos_s, sin_s = _rope_tables(past + jnp.arange(rows_s, dtype=jnp.int32))
    lane_reps = LANES // HEAD_DIM
    cos_pl, sin_pl = jnp.tile(cos_p, (1, lane_reps)), jnp.tile(sin_p, (1, lane_reps))
    cos_pt, sin_pt = cos_p.T, sin_p.T
    cos_s = jnp.tile(cos_s, (tm_s // rows_s, lane_reps))
    sin_s = jnp.tile(sin_s, (tm_s // rows_s, lane_reps))
    et_p = _block_indicator(t)
    et_s = _block_indicator(past + LANES)
    pool_cmp = _feature_major(cache_cmp_kv)
    pool_slc = _feature_major(cache_slc_kv)
    win_state = _feature_major(state_win_kv)

    yp = x_prompt.reshape(b * t, d)
    ys = jnp.pad(x_sample, ((0, 0), (0, rows_s - s_new), (0, 0))).reshape(db * rows_s, d)
    outs = {k: [] for k in ("s_cmp", "s_slc", "s_v")}
    assert min(WINDOW, t) == tm_p
    new_kv_prompt = ()
    win_next = ()
    cmp_tile = min(128, b * t // CMP_BLOCK)

    for l in range(depth):
        wt = _pack_w_in_t(w_in[l])
        pe_p, w1_p, b1_p, w2_p = _pack_compress(cmp_pe[l], cmp_w1[l], cmp_b1[l], cmp_w2[l])
        ln_g = gmlp_ln_g[l][None, :]
        ln_b = gmlp_ln_b[l][None, :]
        tail = (w_out[l].astype(BF16), ln1_g[l][None, :], ln1_b[l][None, :], w_gate[l].astype(BF16),
                w_up[l].astype(BF16), w_down[l].astype(BF16), ln2_g[l][None, :], ln2_b[l][None, :])

        qraw, qrot, gates, u, v, cmp_kv, *new_kv_prompt, slcb, winb = _inproj_prompt(
            yp, wt, cos_pl, sin_pl, cos_pt, sin_pt, ln_g, ln_b, tm_p, b, t, depth, l, new_kv_prompt)
        ckv = _compress_prompt(cmp_kv, pe_p, w1_p, b1_p, w2_p, cmp_tile)
        r3 = lambda a: a.reshape(b, t, a.shape[-1])
        o_nsa = _nsa_prompt(r3(qraw), r3(qrot), r3(gates), ckv.reshape(b, t // SLC_BLOCK, 4 * KV_WIDTH),
                            slcb, winb, et_p)
        gw, gb = _pack_gmlp(gmlp_ws[l], gmlp_bs[l], GMLP_CHUNK)
        yp = _post(yp, o_nsa.reshape(b * t, NSA_WIDTH), u, v, gw, gb, *tail, GMLP_CHUNK, alpha, tm_p)

        qraw, qrot, gates, u, v, cmp_kv, slc_kv, win_kv = _inproj_sample(
            ys, wt, cos_s, sin_s, ln_g, ln_b, tm_s)
        ckv = _compress_sample(pool_cmp, l, page_table, pe_p, w1_p, b1_p, w2_p, attn_seq_per_step)
        s3 = lambda a: a.reshape(db, rows_s, a.shape[-1])
        o_nsa, *win_next = _nsa_sample(pool_slc, win_state, l, page_table, s3(qraw), s3(qrot), s3(gates),
                                       ckv.reshape(db, past // SLC_BLOCK, 4 * KV_WIDTH),
                                       s3(slc_kv), s3(win_kv), et_s, attn_seq_per_step, s_new, win_next)
        gw, gb = _pack_gmlp(gmlp_ws[l], gmlp_bs[l], rows_s)
        ys = _post(ys, o_nsa.reshape(db * rows_s, NSA_WIDTH), u, v, gw, gb, *tail, rows_s, alpha, tm_s)
        new = lambda a: a.reshape((db, rows_s) + a.shape[1:])[:, :s_new]
        outs["s_cmp"].append(new(cmp_kv).reshape((db, s_new) + kv_tail))
        outs["s_slc"].append(new(slc_kv).reshape((db, s_new) + kv_tail))
        outs["s_v"].append(new(v))

    y_sample = ys.reshape(db, rows_s, d)[:, :s_new]
    p_cmp, p_slc, p_win = (_position_major(a) for a in new_kv_prompt)
    return (yp.reshape(b, t, d), y_sample, p_cmp, p_slc, p_win,
            jnp.stack(outs["s_cmp"]), jnp.stack(outs["s_slc"]),
            _position_major(win_next[0]), jnp.stack(outs["s_v"]))
```

```python
import functools
import math

import numpy as np
import jax
import jax.numpy as jnp
from jax import lax
from jax.experimental import pallas as pl
from jax.experimental.pallas import tpu as pltpu

F32 = jnp.float32
BF16 = jnp.bfloat16

LANES = 128
SUBLANES = 8
HEAD_DIM = 64
NSA_HEADS = 8
KV_HEADS = 2
GROUP = NSA_HEADS // KV_HEADS
NSA_WIDTH = NSA_HEADS * HEAD_DIM
KV_WIDTH = KV_HEADS * HEAD_DIM
GMLP_WIDTH = 512
GMLP_GROUPS = 8
GMLP_CHUNK = 128
CMP_BLOCK = 32
CMP_HIDDEN = 256
CMP_PITCH = CMP_BLOCK + SUBLANES
SLC_BLOCK = 64
N_SELECT = 16
WINDOW = 512
QUERY_BLOCK = 128
RANK_TOKENS = LANES
RANK_TILE_GROUP = 4
PAGE_SIZE = 128
FORCE_BONUS = 1.0e4
ROPE_THETA = 10000.0
LN_EPS = 1e-5
NEG_INF = -1e30
MAX_SLC_BLOCKS = LANES
SAMPLE_ROWS = SUBLANES
SLC_KEY_TILE = 1024
VMEM_LIMIT = 56 * 1024 * 1024
Q_SCALE = HEAD_DIM ** -0.5 * math.log2(math.e)

C_Q = 0
C_CMP = C_Q + NSA_WIDTH
C_SLC = C_CMP + 2 * KV_WIDTH
C_WIN = C_SLC + 2 * KV_WIDTH
C_GATE = C_WIN + 2 * KV_WIDTH
C_U = C_GATE + LANES
C_V = C_U + GMLP_WIDTH
C_END = C_V + GMLP_WIDTH


def _dot(a, b):
    return jnp.dot(a, b, preferred_element_type=F32)


def _dot_nt(a, b):
    return lax.dot_general(a, b, (((1,), (1,)), ((), ())), preferred_element_type=F32)


def _layer_norm(x, g, b):
    mu = jnp.mean(x, axis=-1, keepdims=True)
    xc = x - mu
    var = jnp.mean(xc * xc, axis=-1, keepdims=True)
    return xc * lax.rsqrt(var + LN_EPS) * g + b


def _softmax_masked(s, mask):
    s = jnp.where(mask, s, NEG_INF)
    m = jnp.max(s, axis=-1, keepdims=True)
    e = jnp.where(mask, jnp.exp2(s - m), 0.0)
    return e * (1.0 / jnp.maximum(jnp.sum(e, axis=-1, keepdims=True), 1e-30))


def _rope_lanes(c, cos, sin, first_half):
    partner = jnp.where(first_half, pltpu.roll(c, LANES - HEAD_DIM // 2, 1),
                        pltpu.roll(c, HEAD_DIM // 2, 1))
    return c * cos + partner * sin


def _inproj_common(xb, wt_ref, cos, sin, first_half, lng_ref, lnb_ref,
                   qraw_ref, qrot_ref, gate_ref, u_ref, v_ref):
    zq = _dot_nt(xb, wt_ref[C_Q:C_CMP, :])
    qraw_ref[...] = (zq * Q_SCALE).astype(qraw_ref.dtype)
    for c in range(NSA_WIDTH // LANES):
        qrot_ref[:, c * LANES:(c + 1) * LANES] = (_rope_lanes(
            zq[:, c * LANES:(c + 1) * LANES], cos, sin, first_half) * Q_SCALE).astype(qrot_ref.dtype)
    gate_ref[...] = jax.nn.sigmoid(_dot_nt(xb, wt_ref[C_GATE:C_U, :]))
    u_ref[...] = jax.nn.gelu(_dot_nt(xb, wt_ref[C_U:C_V, :])).astype(u_ref.dtype)
    zv = jax.nn.gelu(_dot_nt(xb, wt_ref[C_V:C_END, :]))
    v_ref[...] = _layer_norm(zv, lng_ref[...], lnb_ref[...]).astype(v_ref.dtype)


def _inproj_prompt_kernel(n_carried, x_ref, wt_ref, cos_ref, sin_ref, cost_ref, sint_ref, lng_ref,
                          lnb_ref, *refs):
    (qraw_ref, qrot_ref, gate_ref, u_ref, v_ref,
     cmp_ref, cmpt_ref, slct_ref, wint_ref, slcb_ref, winb_ref) = refs[n_carried:]
    xb = x_ref[...].astype(BF16)
    cos = cos_ref[...]
    lane = lax.broadcasted_iota(jnp.int32, cos.shape, 1)
    first_half = (lane % HEAD_DIM) < (HEAD_DIM // 2)
    _inproj_common(xb, wt_ref, cos, sin_ref[...], first_half, lng_ref, lnb_ref,
                   qraw_ref, qrot_ref, gate_ref, u_ref, v_ref)
    cmp_ref[...] = _dot_nt(xb, wt_ref[C_CMP:C_SLC, :])
    cmpt_ref[...] = _dot_nt(wt_ref[C_CMP:C_SLC, :], xb)
    cost = cost_ref[...]
    sint = sint_ref[...]
    half = HEAD_DIM // 2
    for row0, f_ref, b_ref in ((C_SLC, slct_ref, slcb_ref), (C_WIN, wint_ref, winb_ref)):
        zt = _dot_nt(wt_ref[row0:row0 + 2 * KV_WIDTH, :], xb)
        for g in range(KV_HEADS):
            kg = zt[g * HEAD_DIM:(g + 1) * HEAD_DIM, :]
            partner = jnp.concatenate([kg[half:, :], kg[:half, :]], axis=0)
            kr = kg * cost + partner * sint
            f_ref[g * HEAD_DIM:(g + 1) * HEAD_DIM, :] = kr
            b_ref[g * HEAD_DIM:(g + 1) * HEAD_DIM, :] = kr.astype(BF16)
        f_ref[KV_WIDTH:, :] = zt[KV_WIDTH:, :]
        b_ref[KV_WIDTH:, :] = zt[KV_WIDTH:, :].astype(BF16)


def _inproj_sample_kernel(x_ref, wt_ref, cos_ref, sin_ref, lng_ref, lnb_ref,
                          qraw_ref, qrot_ref, gate_ref, u_ref, v_ref, cmp_ref, slc_ref, win_ref):
    xb = x_ref[...].astype(BF16)
    cos = cos_ref[...]
    sin = sin_ref[...]
    lane = lax.broadcasted_iota(jnp.int32, cos.shape, 1)
    first_half = (lane % HEAD_DIM) < (HEAD_DIM // 2)
    _inproj_common(xb, wt_ref, cos, sin, first_half, lng_ref, lnb_ref,
                   qraw_ref, qrot_ref, gate_ref, u_ref, v_ref)
    cmp_ref[...] = _dot_nt(xb, wt_ref[C_CMP:C_SLC, :])
    for row0, f_ref in ((C_SLC, slc_ref), (C_WIN, win_ref)):
        z = _dot_nt(xb, wt_ref[row0:row0 + 2 * KV_WIDTH, :])
        f_ref[:, :KV_WIDTH] = _rope_lanes(z[:, :KV_WIDTH], cos, sin, first_half)
        f_ref[:, KV_WIDTH:] = z[:, KV_WIDTH:]


def _inproj_prompt(x, wt, cos, sin, cost, sint, ln_g, ln_b, tm, b, t, depth, layer, carried):
    n, d = x.shape
    n_tab = t // tm
    row = lambda width: pl.BlockSpec((tm, width), lambda i: (i, 0))
    const = lambda a: pl.BlockSpec(a.shape, lambda i: (0,) * a.ndim)
    tab = pl.BlockSpec((tm, LANES), lambda i: (i % n_tab, 0))
    tabt = pl.BlockSpec((HEAD_DIM, tm), lambda i: (0, i % n_tab))
    fm = pl.BlockSpec((None, 2 * KV_WIDTH, tm), lambda i: (i // n_tab, 0, i % n_tab))
    fm_layer = pl.BlockSpec((None, None, 2 * KV_WIDTH, tm), lambda i: (layer, i // n_tab, 0, i % n_tab))
    keep_layer = pl.BlockSpec((None, None, 2 * KV_WIDTH, tm), lambda i: (layer, i // n_tab, 0, 0))
    row_out = ((NSA_WIDTH, BF16), (NSA_WIDTH, BF16), (LANES, F32), (GMLP_WIDTH, BF16), (GMLP_WIDTH, BF16),
               (2 * KV_WIDTH, F32))
    n_in = 8
    n_row = len(row_out)
    stacked = lambda width: jax.ShapeDtypeStruct((depth, b, 2 * KV_WIDTH, width), F32)
    return pl.pallas_call(
        functools.partial(_inproj_prompt_kernel, len(carried)),
        grid=(n // tm,),
        in_specs=[row(d), const(wt), tab, tab, tabt, tabt, const(ln_g), const(ln_b)]
        + [pl.BlockSpec(memory_space=pl.ANY)] * len(carried),
        out_specs=[row(wd) for wd, _ in row_out] + [fm_layer, fm_layer, keep_layer, fm, fm],
        out_shape=[jax.ShapeDtypeStruct((n, wd), dt) for wd, dt in row_out]
        + [stacked(t), stacked(t), stacked(tm)]
        + [jax.ShapeDtypeStruct((b, 2 * KV_WIDTH, t), BF16)] * 2,
        input_output_aliases={n_in + k: n_row + k for k in range(len(carried))},
        compiler_params=pltpu.CompilerParams(dimension_semantics=("arbitrary",),
                                             vmem_limit_bytes=VMEM_LIMIT),
        name="inproj_prompt",
    )(x, wt, cos, sin, cost, sint, ln_g, ln_b, *carried)


def _inproj_sample(x, wt, cos, sin, ln_g, ln_b, tm):
    n, d = x.shape
    row = lambda width: pl.BlockSpec((tm, width), lambda i: (i, 0))
    const = lambda a: pl.BlockSpec(a.shape, lambda i: (0,) * a.ndim)
    widths = (NSA_WIDTH, NSA_WIDTH, LANES, GMLP_WIDTH, GMLP_WIDTH,
              2 * KV_WIDTH, 2 * KV_WIDTH, 2 * KV_WIDTH)
    return pl.pallas_call(
        _inproj_sample_kernel,
        grid=(n // tm,),
        in_specs=[row(d), const(wt), const(cos), const(sin), const(ln_g), const(ln_b)],
        out_specs=[row(wd) for wd in widths],
        out_shape=[jax.ShapeDtypeStruct((n, wd), F32) for wd in widths],
        compiler_params=pltpu.CompilerParams(dimension_semantics=("parallel",),
                                             vmem_limit_bytes=VMEM_LIMIT),
        name="inproj_sample",
    )(x, wt, cos, sin, ln_g, ln_b)


def _compress_rows(chunk, pe_ref, w1_ref, b1_ref, w2_ref):
    outs = []
    for k in range(2):
        a = jnp.concatenate([chunk(l, k) for l in range(CMP_BLOCK)], axis=1)
        a = (a + pe_ref[k]).astype(BF16)
        h = jax.nn.silu(_dot(a, w1_ref[k]) + b1_ref[k])
        outs.append(_dot(h.astype(BF16), w2_ref[k]))
    return jnp.concatenate(outs, axis=1)


def _compress_prompt_kernel(x_ref, pe_ref, w1_ref, b1_ref, w2_ref, o_ref, posk_ref, posv_ref):
    tm = o_ref.shape[0]
    gap = jnp.zeros((tm, CMP_PITCH - CMP_BLOCK, KV_WIDTH), F32)
    for k, dst in enumerate((posk_ref, posv_ref)):
        rows = x_ref[:, k * KV_WIDTH:(k + 1) * KV_WIDTH].reshape(tm, CMP_BLOCK, KV_WIDTH)
        dst[...] = jnp.concatenate([rows, gap], axis=1).reshape(tm * CMP_PITCH, KV_WIDTH)
    chunk = lambda l, k: (posk_ref, posv_ref)[k][pl.ds(l, tm, stride=CMP_PITCH), :]
    o_ref[...] = _compress_rows(chunk, pe_ref, w1_ref, b1_ref, w2_ref)


def _compress_prompt(x, pe, w1, b1, w2, tm):
    n = x.shape[0] // CMP_BLOCK
    const = lambda a: pl.BlockSpec(a.shape, lambda i: (0,) * a.ndim)
    return pl.pallas_call(
        _compress_prompt_kernel,
        grid=(n // tm,),
        in_specs=[pl.BlockSpec((tm * CMP_BLOCK, x.shape[1]), lambda i: (i, 0)),
                  const(pe), const(w1), const(b1), const(w2)],
        out_specs=pl.BlockSpec((tm, 2 * KV_WIDTH), lambda i: (i, 0)),
        out_shape=jax.ShapeDtypeStruct((n, 2 * KV_WIDTH), F32),
        scratch_shapes=[pltpu.VMEM((tm * CMP_PITCH, KV_WIDTH), F32)] * 2,
        compiler_params=pltpu.CompilerParams(dimension_semantics=("parallel",),
                                             vmem_limit_bytes=VMEM_LIMIT),
        name="compress_prompt",
    )(x, pe, w1, b1, w2)


def _compress_sample_kernel(n_seq, n_pages, pt_ref, *refs):
    page_refs = refs[:n_seq * n_pages]
    pe_ref, w1_ref, b1_ref, w2_ref, o_ref, posk_ref, posv_ref = refs[n_seq * n_pages:]
    blocks_per_page = PAGE_SIZE // CMP_BLOCK
    for j, page_ref in enumerate(page_refs):
        for dst, rows in ((posk_ref, page_ref[:KV_WIDTH, :].T), (posv_ref, page_ref[KV_WIDTH:, :].T)):
            for n in range(blocks_per_page):
                r0 = (j * blocks_per_page + n) * CMP_PITCH
                dst[r0:r0 + CMP_BLOCK, :] = rows[n * CMP_BLOCK:(n + 1) * CMP_BLOCK, :]
    m = n_seq * n_pages * blocks_per_page
    chunk = lambda l, k: (posk_ref, posv_ref)[k][pl.ds(l, m, stride=CMP_PITCH), :]
    o_ref[...] = _compress_rows(chunk, pe_ref, w1_ref, b1_ref, w2_ref)


def _compress_sample(pool, layer, page_table, pe, w1, b1, w2, n_seq):
    db, n_pages = page_table.shape
    rows = n_seq * n_pages * (PAGE_SIZE // CMP_BLOCK)
    const = lambda a: pl.BlockSpec(a.shape, lambda i, pt: (0,) * a.ndim)

    def page_spec(s, p):
        return pl.BlockSpec((None, None) + pool.shape[2:],
                            lambda i, pt: (layer, pt[i * n_seq + s, p], 0, 0))

    grid_spec = pltpu.PrefetchScalarGridSpec(
        num_scalar_prefetch=1,
        grid=(db // n_seq,),
        in_specs=[page_spec(s, p) for s in range(n_seq) for p in range(n_pages)]
        + [const(pe), const(w1), const(b1), const(w2)],
        out_specs=pl.BlockSpec((rows, 2 * KV_WIDTH), lambda i, pt: (i, 0)),
        scratch_shapes=[pltpu.VMEM((rows * CMP_PITCH, KV_WIDTH), F32)] * 2,
    )
    return pl.pallas_call(
        functools.partial(_compress_sample_kernel, n_seq, n_pages),
        grid_spec=grid_spec,
        out_shape=jax.ShapeDtypeStruct((db * n_pages * (PAGE_SIZE // CMP_BLOCK), 2 * KV_WIDTH), F32),
        compiler_params=pltpu.CompilerParams(dimension_semantics=("parallel",),
                                             vmem_limit_bytes=VMEM_LIMIT),
        name="compress_sample",
    )(page_table, *([pool] * (n_seq * n_pages)), pe, w1, b1, w2)


def _stack_heads(q, tq):
    lane_lo = lax.broadcasted_iota(jnp.int32, (tq, LANES), 1) < HEAD_DIM
    pieces = []
    for hh in range(NSA_HEADS):
        g = hh // GROUP
        c = hh // 2
        chunk = q[:, c * LANES:(c + 1) * LANES]
        if hh % 2 != g:
            chunk = pltpu.roll(chunk, HEAD_DIM, 1)
        pieces.append(jnp.where(lane_lo if g == 0 else jnp.logical_not(lane_lo), chunk, 0.0))
    return jnp.concatenate(pieces, axis=0).astype(BF16)


def _cmp_branch(qraw, ckv, qpos, tq):
    nb = MAX_SLC_BLOCKS
    kc = jnp.concatenate([ckv[:, 0:KV_WIDTH], ckv[:, 2 * KV_WIDTH:3 * KV_WIDTH]], axis=0).astype(BF16)
    vc = jnp.concatenate([ckv[:, KV_WIDTH:2 * KV_WIDTH], ckv[:, 3 * KV_WIDTH:]], axis=0).astype(BF16)
    s = _dot_nt(qraw, kc).reshape(NSA_HEADS, tq, 2 * nb)
    lane = lax.broadcasted_iota(jnp.int32, (tq, 2 * nb), 1)
    blk = jnp.where(lane < nb, 2 * lane, 2 * (lane - nb) + 1)
    mask = ((blk + 1) * CMP_BLOCK - 1) <= qpos
    p = _softmax_masked(s, mask[None])
    o_cmp = _dot(p.reshape(NSA_HEADS * tq, 2 * nb).astype(BF16), vc)
    scores = []
    sblk = lax.broadcasted_iota(jnp.int32, (tq, nb), 1)
    cur = qpos // SLC_BLOCK
    valid = sblk <= cur
    forced = ((sblk == 0) | (sblk == cur) | (sblk == cur - 1)).astype(F32)
    for g in range(KV_HEADS):
        imp = p[g * GROUP]
        for r in range(1, GROUP):
            imp = imp + p[g * GROUP + r]
        imp = imp[:, :nb] + imp[:, nb:]
        scores.append(jnp.where(valid, imp + FORCE_BONUS * forced, -1.0))
    return o_cmp, scores, valid


def _rank_blocks_wide(score, st_ref, cnt_ref, n_chunks):
    st_ref[...] = score.T
    cnt_ref[...] = jnp.zeros_like(cnt_ref)
    sub = lax.broadcasted_iota(jnp.int32, (SUBLANES, LANES), 0)
    n_vregs = MAX_SLC_BLOCKS // SUBLANES
    def count(c, v_lo, v_hi):
        cand = st_ref[c * SUBLANES:(c + 1) * SUBLANES, :]
        tiles = {v: st_ref[v * SUBLANES:(v + 1) * SUBLANES, :] for v in range(v_lo, v_hi)}
        cnts = {v: cnt_ref[v * SUBLANES:(v + 1) * SUBLANES, :] for v in range(v_lo, v_hi)}
        for ii in range(SUBLANES):
            row = cand[ii:ii + 1, :]
            for v in range(v_lo, v_hi):
                if v < c:
                    inc = jnp.where(row > tiles[v], 1.0, 0.0)
                elif v > c:
                    inc = jnp.where(row >= tiles[v], 1.0, 0.0)
                else:
                    inc = jnp.where(sub > ii, jnp.where(row >= tiles[v], 1.0, 0.0),
                                    jnp.where(row > tiles[v], 1.0, 0.0))
                cnts[v] = cnts[v] + inc
        for v in range(v_lo, v_hi):
            cnt_ref[v * SUBLANES:(v + 1) * SUBLANES, :] = cnts[v]

    for c in range(n_vregs):
        @pl.when(c < n_chunks)
        def _():
            for v_lo in range(0, n_vregs, RANK_TILE_GROUP):
                v_hi = v_lo + RANK_TILE_GROUP
                if v_lo <= c:
                    count(c, v_lo, v_hi)
                else:
                    pl.when(v_lo < n_chunks)(functools.partial(count, c, v_lo, v_hi))
    return cnt_ref[...].T


def _rank_blocks_narrow(score, n_blocks):
    lane = lax.broadcasted_iota(jnp.int32, score.shape, 1)
    cnt = jnp.zeros(score.shape, F32)
    for i in range(n_blocks):
        col = score[:, i:i + 1]
        cnt = cnt + jnp.where(lane > i, jnp.where(col >= score, 1.0, 0.0),
                              jnp.where(col > score, 1.0, 0.0))
    return cnt


def _slc_branch(q_aug, kv_ref, et_ref, n_full, qpos, tq, tk):
    ones_rows = jnp.where(lax.broadcasted_iota(jnp.int32, (HEAD_DIM, tk), 0) == 0, 1.0, 0.0).astype(BF16)

    def tile(kt, carry, diagonal):
        ms, accs = carry
        off = pl.multiple_of(kt * tk, tk)
        ka = jnp.concatenate([et_ref[:, pl.ds(off, tk)], kv_ref[:KV_WIDTH, pl.ds(off, tk)]], axis=0)
        s = _dot(q_aug, ka)
        if diagonal:
            visible = (off + lax.broadcasted_iota(jnp.int32, (tq, tk), 1)) <= qpos
        ps, m_new, alphas = [], [], []
        for hh in range(NSA_HEADS):
            s_h = s[hh * tq:(hh + 1) * tq, :]
            if diagonal:
                s_h = jnp.where(visible, s_h, NEG_INF)
            m_h = jnp.maximum(ms[hh], jnp.max(s_h, axis=-1, keepdims=True))
            ps.append(jnp.exp2(s_h - m_h).astype(BF16))
            alphas.append(jnp.exp2(ms[hh] - m_h))
            m_new.append(m_h)
        acc_new = []
        for g in range(KV_HEADS):
            v0 = KV_WIDTH + g * HEAD_DIM
            va = jnp.concatenate([kv_ref[v0:v0 + HEAD_DIM, pl.ds(off, tk)], ones_rows], axis=0)
            pv = _dot_nt(jnp.concatenate(ps[g * GROUP:(g + 1) * GROUP], axis=0), va)
            alpha = jnp.concatenate(alphas[g * GROUP:(g + 1) * GROUP], axis=0)
            acc_new.append(alpha * accs[g] + pv)
        return tuple(m_new), tuple(acc_new)

    init = (tuple(jnp.full((tq, 1), NEG_INF, F32) for _ in range(NSA_HEADS)),
            tuple(jnp.zeros((GROUP * tq, KV_WIDTH), F32) for _ in range(KV_HEADS)))
    def first_group(count):
        def run(c):
            c = tile(n_full, c, True)
            for u in range(count - 1):
                c = tile(u, c, False)
            return c
        return run

    first = (n_full + 1) % 4
    carry = lax.switch(first, [first_group(4), first_group(1), first_group(2), first_group(3)], init)
    taken = jnp.where(first == 0, 3, first - 1)

    def quad(j, c):
        kt = taken + 4 * j
        for u in range(4):
            c = tile(kt + u, c, False)
        return c

    _, accs = lax.fori_loop(0, (n_full - taken) // 4, quad, carry)
    outs = [a * (1.0 / jnp.maximum(a[:, HEAD_DIM:HEAD_DIM + 1], 1e-30)) for a in accs]
    return jnp.concatenate([outs[0], pltpu.roll(outs[1], HEAD_DIM, 1)], axis=0)


def _softmax_pv(s, mask, vt, tq):
    nk = s.shape[1]
    rows = NSA_HEADS * tq
    if rows <= LANES:
        s3 = jnp.where(mask[None], s.reshape(NSA_HEADS, tq, nk), NEG_INF)
        p = jnp.exp2(s3 - jnp.max(s3, axis=-1, keepdims=True)).reshape(rows, nk).astype(BF16)
        if rows < LANES:
            p = jnp.concatenate([p, jnp.zeros((LANES - rows, nk), BF16)], axis=0)
        ones_rows = jnp.where(lax.broadcasted_iota(jnp.int32, (2 * SUBLANES, nk), 0) == 0,
                              1.0, 0.0).astype(BF16)
        ot = _dot_nt(jnp.concatenate([vt, ones_rows], axis=0), p)
        return (ot[:KV_WIDTH] * (1.0 / jnp.maximum(ot[KV_WIDTH:KV_WIDTH + 1], 1e-30))).T[:rows]
    ones_rows = jnp.where(lax.broadcasted_iota(jnp.int32, (HEAD_DIM, nk), 0) == 0, 1.0, 0.0).astype(BF16)
    outs = []
    for g in range(KV_HEADS):
        ps = []
        for hh in range(g * GROUP, (g + 1) * GROUP):
            s_h = jnp.where(mask, s[hh * tq:(hh + 1) * tq, :], NEG_INF)
            ps.append(jnp.exp2(s_h - jnp.max(s_h, axis=-1, keepdims=True)).astype(BF16))
        va = jnp.concatenate([vt[g * HEAD_DIM:(g + 1) * HEAD_DIM, :], ones_rows], axis=0)
        pv = _dot_nt(jnp.concatenate(ps, axis=0), va)
        outs.append(pv * (1.0 / jnp.maximum(pv[:, HEAD_DIM:HEAD_DIM + 1], 1e-30)))
    return jnp.concatenate([outs[0], pltpu.roll(outs[1], HEAD_DIM, 1)], axis=0)


def _slc_single(bias_rows, qrot, kt, vt, et, qpos, tq):
    nk = kt.shape[1]
    s = _dot(bias_rows, et) + _dot(qrot, kt)
    visible = lax.broadcasted_iota(jnp.int32, (tq, nk), 1) <= qpos
    return _softmax_pv(s, visible, vt, tq)


def _win_branch(qrot, kt, vt, kpos, qpos, tq):
    diff = qpos - kpos
    return _softmax_pv(_dot(qrot, kt), (diff >= 0) & (diff < WINDOW), vt, tq)


def _combine(o_cmp, o_slc, o_win, gates, tq):
    lane_lo = lax.broadcasted_iota(jnp.int32, (tq, LANES), 1) < HEAD_DIM
    heads = []
    for hh in range(NSA_HEADS):
        tot = None
        for br, o in enumerate((o_cmp, o_slc, o_win)):
            col = br * NSA_HEADS + hh
            term = gates[:, col:col + 1] * o[hh * tq:(hh + 1) * tq, :]
            tot = term if tot is None else tot + term
        heads.append(tot)
    chunks = []
    for c in range(NSA_HEADS // 2):
        a, b = heads[2 * c], heads[2 * c + 1]
        if c // 2 == 0:
            chunks.append(jnp.where(lane_lo, a, pltpu.roll(b, HEAD_DIM, 1)))
        else:
            chunks.append(jnp.where(lane_lo, pltpu.roll(a, HEAD_DIM, 1), b))
    return jnp.concatenate(chunks, axis=1)


def _selection_bias(rank, valid):
    return jnp.where((rank < float(N_SELECT)) & valid, 0.0, NEG_INF).astype(BF16)


def _pad_ckv(ckv):
    n = ckv.shape[0]
    if n == MAX_SLC_BLOCKS:
        return ckv
    return jnp.concatenate([ckv, jnp.zeros((MAX_SLC_BLOCKS - n, ckv.shape[1]), F32)], axis=0)


def _nsa_prompt_kernel(qraw_ref, qrot_ref, gate_ref, ckv_ref, slc_ref, win_ref, et_ref,
                       o_ref, st_ref, cnt_ref):
    tq = QUERY_BLOCK
    i = pl.program_id(1)
    qpos0 = i * tq
    qpos = qpos0 + lax.broadcasted_iota(jnp.int32, (tq, 1), 0)
    qraw = _stack_heads(qraw_ref[...].astype(F32), tq)
    qrot = _stack_heads(qrot_ref[...].astype(F32), tq)

    o_cmp, scores, valid = _cmp_branch(qraw, _pad_ckv(ckv_ref[...]), qpos, tq)
    n_chunks = (qpos0 + tq - 1) // (SLC_BLOCK * SUBLANES) + 1
    bias = []
    for g in range(KV_HEADS):
        parts = []
        for t0 in range(0, tq, RANK_TOKENS):
            rank = _rank_blocks_wide(scores[g][t0:t0 + RANK_TOKENS], st_ref, cnt_ref, n_chunks)
            parts.append(_selection_bias(rank, valid[t0:t0 + RANK_TOKENS]))
        bias.append(jnp.concatenate(parts, axis=0))
    bias_rows = jnp.concatenate([bias[0]] * GROUP + [bias[1]] * GROUP, axis=0)
    q_aug = jnp.concatenate([bias_rows, qrot], axis=1)
    o_slc = _slc_branch(q_aug, slc_ref, et_ref, qpos0 // SLC_KEY_TILE, qpos, tq, SLC_KEY_TILE)

    nk = WINDOW + tq
    start = pl.multiple_of(jnp.maximum(qpos0 - WINDOW, 0), tq)
    kpos = start + lax.broadcasted_iota(jnp.int32, (1, nk), 1)
    o_win = _win_branch(qrot, win_ref[:KV_WIDTH, pl.ds(start, nk)], win_ref[KV_WIDTH:, pl.ds(start, nk)],
                        kpos, qpos, tq)

    o_ref[...] = _combine(o_cmp, o_slc, o_win, gate_ref[...], tq).astype(o_ref.dtype)


def _nsa_prompt(qraw, qrot, gates, ckv, slcb, winb, et):
    b, t, _ = qraw.shape
    tq = QUERY_BLOCK
    blk = lambda width: pl.BlockSpec((None, tq, width), lambda bi, i: (bi, i, 0))
    per_batch = lambda a: pl.BlockSpec((None,) + a.shape[1:], lambda bi, i: (bi, 0, 0))
    return pl.pallas_call(
        _nsa_prompt_kernel,
        grid=(b, t // tq),
        in_specs=[blk(NSA_WIDTH), blk(NSA_WIDTH), blk(LANES), per_batch(ckv), per_batch(slcb),
                  per_batch(winb), pl.BlockSpec(et.shape, lambda bi, i: (0, 0))],
        out_specs=blk(NSA_WIDTH),
        out_shape=jax.ShapeDtypeStruct((b, t, NSA_WIDTH), BF16),
        scratch_shapes=[pltpu.VMEM((MAX_SLC_BLOCKS, RANK_TOKENS), F32)] * 2,
        compiler_params=pltpu.CompilerParams(dimension_semantics=("parallel", "arbitrary"),
                                             vmem_limit_bytes=VMEM_LIMIT),
        name="nsa_prompt",
    )(qraw, qrot, gates, ckv, slcb, winb, et)


def _nsa_sample_kernel(n_seq, n_pages, past, s_new, pt_ref, *refs):
    n_in = n_seq * n_pages
    page_refs = refs[:n_in]
    (qraw_ref, qrot_ref, gate_ref, ckv_ref, slcn_ref, wins_ref, winn_ref, et_ref) = refs[n_in:n_in + 8]
    o_ref, wout_ref = refs[-2:]
    tq = SAMPLE_ROWS
    qpos = past + lax.broadcasted_iota(jnp.int32, (tq, 1), 0)
    n_buf = wins_ref.shape[2]
    n_blocks = (past + tq - 1) // SLC_BLOCK + 1
    pad_rows = jnp.zeros((LANES - tq, 2 * KV_WIDTH), F32)
    wpos = past - n_buf + lax.broadcasted_iota(jnp.int32, (1, n_buf + LANES), 1)
    lane = lax.broadcasted_iota(jnp.int32, (2 * KV_WIDTH, LANES), 1)
    for s in range(n_seq):
        slc_new = jnp.concatenate([slcn_ref[s], pad_rows], axis=0).T
        tiles = [page_refs[s * n_pages + p][...].astype(BF16) for p in range(n_pages)]
        tiles.append(slc_new.astype(BF16))
        kt = jnp.concatenate([tl[:KV_WIDTH] for tl in tiles], axis=1)
        vt = jnp.concatenate([tl[KV_WIDTH:] for tl in tiles], axis=1)

        qraw = _stack_heads(qraw_ref[s], tq)
        qrot = _stack_heads(qrot_ref[s], tq)
        o_cmp, scores, valid = _cmp_branch(qraw, _pad_ckv(ckv_ref[s]), qpos, tq)
        bias = [_selection_bias(_rank_blocks_narrow(scores[g], n_blocks), valid)
                for g in range(KV_HEADS)]
        bias_rows = jnp.concatenate([bias[0]] * GROUP + [bias[1]] * GROUP, axis=0)
        o_slc = _slc_single(bias_rows, qrot, kt, vt, et_ref[...], qpos, tq)

        state = wins_ref[s]
        win_new = jnp.concatenate([winn_ref[s], pad_rows], axis=0).T
        kt = jnp.concatenate([state[:KV_WIDTH], win_new[:KV_WIDTH]], axis=1).astype(BF16)
        vt = jnp.concatenate([state[KV_WIDTH:], win_new[KV_WIDTH:]], axis=1).astype(BF16)
        o_win = _win_branch(qrot, kt, vt, wpos, qpos, tq)
        o_ref[s] = _combine(o_cmp, o_slc, o_win, gate_ref[s], tq)

        shifted = pltpu.roll(state, n_buf - s_new, 1)
        tail = jnp.where(lane >= LANES - s_new, pltpu.roll(win_new, LANES - s_new, 1),
                         shifted[:, n_buf - LANES:])
        wout_ref[s, :, :n_buf - LANES] = shifted[:, :n_buf - LANES]
        wout_ref[s, :, n_buf - LANES:] = tail


def _nsa_sample(pool, win_state, layer, page_table, qraw, qrot, gates, ckv, slc_new, win_new, et,
                n_seq, s_new, carried):
    db, n_pages = page_table.shape
    past = n_pages * PAGE_SIZE
    tq = SAMPLE_ROWS
    n_buf = win_state.shape[3]
    seq = lambda a: pl.BlockSpec((n_seq,) + a.shape[1:], lambda i, pt: (i, 0, 0))

    def page_spec(s, p):
        return pl.BlockSpec((None, None) + pool.shape[2:],
                            lambda i, pt: (layer, pt[i * n_seq + s, p], 0, 0))

    state_spec = pl.BlockSpec((None, n_seq) + win_state.shape[2:], lambda i, pt: (layer, i, 0, 0))
    grid_spec = pltpu.PrefetchScalarGridSpec(
        num_scalar_prefetch=1,
        grid=(db // n_seq,),
        in_specs=[page_spec(s, p) for s in range(n_seq) for p in range(n_pages)]
        + [seq(qraw), seq(qrot), seq(gates), seq(ckv), seq(slc_new), state_spec, seq(win_new),
           pl.BlockSpec(et.shape, lambda i, pt: (0, 0))]
        + [pl.BlockSpec(memory_space=pl.ANY)] * len(carried),
        out_specs=[pl.BlockSpec((n_seq, tq, NSA_WIDTH), lambda i, pt: (i, 0, 0)),
                   pl.BlockSpec((None, n_seq) + win_state.shape[2:], lambda i, pt: (layer, i, 0, 0))],
    )
    n_operands = 1 + n_seq * n_pages + 8
    return pl.pallas_call(
        functools.partial(_nsa_sample_kernel, n_seq, n_pages, past, s_new),
        grid_spec=grid_spec,
        out_shape=[jax.ShapeDtypeStruct((db, tq, NSA_WIDTH), F32),
                   jax.ShapeDtypeStruct(win_state.shape, F32)],
        input_output_aliases={n_operands + k: 1 + k for k in range(len(carried))},
        compiler_params=pltpu.CompilerParams(dimension_semantics=("parallel",),
                                             vmem_limit_bytes=VMEM_LIMIT),
        name="nsa_sample",
    )(page_table, *([pool] * (n_seq * n_pages)), qraw, qrot, gates, ckv, slc_new, win_state, win_new, et,
      *carried)


def _post_kernel(chunk, alpha, n_ffn_chunks, x_ref, o_ref, u_ref, v_ref, gw_ref, gb_ref,
                 wout_ref, ln1g_ref, ln1b_ref, wg_ref, wu_ref, wd_ref, ln2g_ref, ln2b_ref, y_ref):
    tm = x_ref.shape[0]
    c = GMLP_CHUNK
    row = lax.broadcasted_iota(jnp.int32, (c, c), 0)
    col = lax.broadcasted_iota(jnp.int32, (c, c), 1)
    causal = (col <= row) & ((row // chunk) == (col // chunk))
    w_cat = jnp.concatenate([jnp.where(causal, gw_ref[h], 0.0) for h in range(GMLP_GROUPS)],
                            axis=1).astype(BF16)
    lane_group = lax.broadcasted_iota(jnp.int32, (c, GMLP_WIDTH), 1) // (GMLP_WIDTH // GMLP_GROUPS)
    mixed = []
    for sub in range(tm // c):
        v = v_ref[sub * c:(sub + 1) * c, :]
        v_diag = jnp.concatenate([jnp.where(lane_group == h, v, 0.0) for h in range(GMLP_GROUPS)],
                                 axis=0).astype(BF16)
        s = _dot(w_cat, v_diag) + gb_ref[...]
        mixed.append(u_ref[sub * c:(sub + 1) * c, :].astype(F32) * s)
    o_gmlp = jnp.concatenate(mixed, axis=0)
    mix = jnp.concatenate([o_ref[...], o_gmlp], axis=1).astype(BF16)
    h = _dot(mix, wout_ref[...])
    x1 = _layer_norm(alpha * x_ref[...] + h, ln1g_ref[...], ln1b_ref[...])
    x1b = x1.astype(BF16)
    f = None
    fc = wg_ref.shape[1] // n_ffn_chunks
    for j in range(n_ffn_chunks):
        gate = _dot(x1b, wg_ref[:, j * fc:(j + 1) * fc])
        up = _dot(x1b, wu_ref[:, j * fc:(j + 1) * fc])
        a = (jax.nn.silu(gate) * up).astype(BF16)
        part = _dot(a, wd_ref[j * fc:(j + 1) * fc, :])
        f = part if f is None else f + part
    y_ref[...] = _layer_norm(alpha * x1 + f, ln2g_ref[...], ln2b_ref[...])


def _post(x, o_nsa, u, v, gw, gb, wout, ln1g, ln1b, wg, wu, wd, ln2g, ln2b, chunk, alpha, tm):
    n, d = x.shape
    ffn = wg.shape[1]
    n_ffn_chunks = next(k for k in (4, 2, 1) if ffn % (k * LANES) == 0)
    row = lambda width: pl.BlockSpec((tm, width), lambda i: (i, 0))
    const = lambda a: pl.BlockSpec(a.shape, lambda i: (0,) * a.ndim, pipeline_mode=pl.Buffered(1))
    consts = (gw, gb, wout, ln1g, ln1b, wg, wu, wd, ln2g, ln2b)
    return pl.pallas_call(
        functools.partial(_post_kernel, chunk, alpha, n_ffn_chunks),
        grid=(n // tm,),
        in_specs=[row(d), row(NSA_WIDTH), row(GMLP_WIDTH), row(GMLP_WIDTH)] + [const(a) for a in consts],
        out_specs=row(d),
        out_shape=jax.ShapeDtypeStruct((n, d), F32),
        compiler_params=pltpu.CompilerParams(dimension_semantics=("parallel",),
                                             vmem_limit_bytes=VMEM_LIMIT),
        name="post",
    )(x, o_nsa, u, v, *consts)


def _pack_w_in_t(w_in):
    wt = w_in.T
    o_gate = NSA_WIDTH + 3 * 2 * KV_WIDTH
    o_gmlp = o_gate + 3 * NSA_HEADS
    gate = jnp.pad(wt[o_gate:o_gmlp], ((0, LANES - 3 * NSA_HEADS), (0, 0)))
    return jnp.concatenate([wt[:o_gate], gate, wt[o_gmlp:]], axis=0).astype(BF16)


def _pack_compress(pe, w1, b1, w2):
    eye = jnp.eye(KV_HEADS, dtype=F32)
    zero = jnp.zeros_like(w1)
    w1p = jnp.stack([jnp.concatenate([w1, zero], axis=-1), jnp.concatenate([zero, w1], axis=-1)], axis=2)
    w1p = w1p.reshape(2, CMP_BLOCK * KV_WIDTH, KV_HEADS * CMP_HIDDEN)
    pep = jnp.broadcast_to(pe[:, :, None, :], (2, CMP_BLOCK, KV_HEADS, HEAD_DIM)).reshape(2, 1, -1)
    b1p = jnp.tile(b1, (1, KV_HEADS))[:, None, :]
    w2p = jnp.einsum('khd,gG->kghGd', w2, eye).reshape(2, KV_HEADS * CMP_HIDDEN, KV_WIDTH)
    return pep, w1p.astype(BF16), b1p, w2p.astype(BF16)


def _pack_gmlp(ws, bs, chunk):
    reps = GMLP_CHUNK // chunk
    gw = jnp.tile(ws[:, :chunk, :chunk], (1, reps, reps))
    gb = jnp.tile(jnp.repeat(bs[:, :chunk].T, GMLP_WIDTH // GMLP_GROUPS, axis=1), (reps, 1))
    return gw, gb


def _rope_tables(pos):
    half = HEAD_DIM // 2
    inv_freq = ROPE_THETA ** (-jnp.arange(half, dtype=F32) / half)
    ang = pos.astype(F32)[:, None] * inv_freq[None, :]
    cos, sin = jnp.cos(ang), jnp.sin(ang)
    return jnp.concatenate([cos, cos], axis=1), jnp.concatenate([-sin, sin], axis=1)


def _block_indicator(n_keys):
    key_block = np.arange(n_keys)[None, :] // SLC_BLOCK
    return jnp.asarray(key_block == np.arange(MAX_SLC_BLOCKS)[:, None], dtype=BF16)


def _feature_major(a):
    lead = a.shape[:-4]
    n = len(lead)
    perm = tuple(range(n)) + (n + 1, n + 2, n + 3, n)
    return jnp.transpose(a, perm).reshape(lead + (2 * KV_WIDTH, a.shape[-4]))


def _position_major(a):
    lead = a.shape[:-2]
    n = len(lead)
    a = a.reshape(lead + (2, KV_HEADS, HEAD_DIM, a.shape[-1]))
    return jnp.transpose(a, tuple(range(n)) + (n + 3, n, n + 1, n + 2))


def kernel(x_prompt, x_sample, cache_cmp_kv, cache_slc_kv, state_win_kv, page_table,
           w_in, cmp_pe, cmp_w1, cmp_b1, cmp_w2, gmlp_ln_g, gmlp_ln_b, gmlp_ws, gmlp_bs,
           w_out, ln1_g, ln1_b, w_gate, w_up, w_down, ln2_g, ln2_b):
    depth = w_in.shape[0]
    b, t, d = x_prompt.shape
    db, s_new, _ = x_sample.shape
    n_pages = page_table.shape[1]
    past = n_pages * PAGE_SIZE
    n_buf = state_win_kv.shape[2]
    kv_tail = (2, KV_HEADS, HEAD_DIM)
    rows_s = SAMPLE_ROWS
    assert t % SLC_KEY_TILE == 0 and t >= WINDOW + QUERY_BLOCK and t // SLC_BLOCK <= MAX_SLC_BLOCKS
    assert s_new <= rows_s and (past + rows_s - 1) // SLC_BLOCK < MAX_SLC_BLOCKS
    assert past % CMP_BLOCK == 0 and past % CMP_BLOCK + s_new < CMP_BLOCK
    assert n_buf % LANES == 0 and n_buf >= LANES
    alpha = (2 * depth) ** 0.25

    tm_p = 512
    tm_s = min(512, db * rows_s)
    seq_per_step = 2 if db % 2 == 0 else 1
    attn_seq_per_step = 4 if db % 4 == 0 else seq_per_step
    cos_p, sin_p = _rope_tables(jnp.arange(t, dtype=jnp.int32))
    cos_s, sin_s = _rope_tables(past + jnp.arange(rows_s, dtype=jnp.int32))
    lane_reps = LANES // HEAD_DIM
    cos_pl, sin_pl = jnp.tile(cos_p, (1, lane_reps)), jnp.tile(sin_p, (1, lane_reps))
    cos_pt, sin_pt = cos_p.T, sin_p.T
    cos_s = jnp.tile(cos_s, (tm_s // rows_s, lane_reps))
    sin_s = jnp.tile(sin_s, (tm_s // rows_s, lane_reps))
    et_p = _block_indicator(t)
    et_s = _block_indicator(past + LANES)
    pool_cmp = _feature_major(cache_cmp_kv)
    pool_slc = _feature_major(cache_slc_kv)
    win_state = _feature_major(state_win_kv)

    yp = x_prompt.reshape(b * t, d)
    ys = jnp.pad(x_sample, ((0, 0), (0, rows_s - s_new), (0, 0))).reshape(db * rows_s, d)
    outs = {k: [] for k in ("s_cmp", "s_slc", "s_v")}
    assert min(WINDOW, t) == tm_p
    new_kv_prompt = ()
    win_next = ()
    cmp_tile = min(128, b * t // CMP_BLOCK)

    for l in range(depth):
        wt = _pack_w_in_t(w_in[l])
        pe_p, w1_p, b1_p, w2_p = _pack_compress(cmp_pe[l], cmp_w1[l], cmp_b1[l], cmp_w2[l])
        ln_g = gmlp_ln_g[l][None, :]
        ln_b = gmlp_ln_b[l][None, :]
        tail = (w_out[l].astype(BF16), ln1_g[l][None, :], ln1_b[l][None, :], w_gate[l].astype(BF16),
                w_up[l].astype(BF16), w_down[l].astype(BF16), ln2_g[l][None, :], ln2_b[l][None, :])

        qraw, qrot, gates, u, v, cmp_kv, *new_kv_prompt, slcb, winb = _inproj_prompt(
            yp, wt, cos_pl, sin_pl, cos_pt, sin_pt, ln_g, ln_b, tm_p, b, t, depth, l, new_kv_prompt)
        ckv = _compress_prompt(cmp_kv, pe_p, w1_p, b1_p, w2_p, cmp_tile)
        r3 = lambda a: a.reshape(b, t, a.shape[-1])
        o_nsa = _nsa_prompt(r3(qraw), r3(qrot), r3(gates), ckv.reshape(b, t // SLC_BLOCK, 4 * KV_WIDTH),
                            slcb, winb, et_p)
        gw, gb = _pack_gmlp(gmlp_ws[l], gmlp_bs[l], GMLP_CHUNK)
        yp = _post(yp, o_nsa.reshape(b * t, NSA_WIDTH), u, v, gw, gb, *tail, GMLP_CHUNK, alpha, tm_p)

        qraw, qrot, gates, u, v, cmp_kv, slc_kv, win_kv = _inproj_sample(
            ys, wt, cos_s, sin_s, ln_g, ln_b, tm_s)
        ckv = _compress_sample(pool_cmp, l, page_table, pe_p, w1_p, b1_p, w2_p, attn_seq_per_step)
        s3 = lambda a: a.reshape(db, rows_s, a.shape[-1])
        o_nsa, *win_next = _nsa_sample(pool_slc, win_state, l, page_table, s3(qraw), s3(qrot), s3(gates),
                                       ckv.reshape(db, past // SLC_BLOCK, 4 * KV_WIDTH),
                                       s3(slc_kv), s3(win_kv), et_s, attn_seq_per_step, s_new, win_next)
        gw, gb = _pack_gmlp(gmlp_ws[l], gmlp_bs[l], rows_s)
        ys = _post(ys, o_nsa.reshape(db * rows_s, NSA_WIDTH), u, v, gw, gb, *tail, rows_s, alpha, tm_s)
        new = lambda a: a.reshape((db, rows_s) + a.shape[1:])[:, :s_new]
        outs["s_cmp"].append(new(cmp_kv).reshape((db, s_new) + kv_tail))
        outs["s_slc"].append(new(slc_kv).reshape((db, s_new) + kv_tail))
        outs["s_v"].append(new(v))

    y_sample = ys.reshape(db, rows_s, d)[:, :s_new]
    p_cmp, p_slc, p_win = (_position_major(a) for a in new_kv_prompt)
    return (yp.reshape(b, t, d), y_sample, p_cmp, p_slc, p_win,
            jnp.stack(outs["s_cmp"]), jnp.stack(outs["s_slc"]),
            _position_major(win_next[0]), jnp.stack(outs["s_v"]))
```

```python
import functools
import math

import numpy as np
import jax
import jax.numpy as jnp
from jax import lax
from jax.experimental import pallas as pl
from jax.experimental.pallas import tpu as pltpu

F32 = jnp.float32
BF16 = jnp.bfloat16

LANES = 128
SUBLANES = 8
HEAD_DIM = 64
NSA_HEADS = 8
KV_HEADS = 2
GROUP = NSA_HEADS // KV_HEADS
NSA_WIDTH = NSA_HEADS * HEAD_DIM
KV_WIDTH = KV_HEADS * HEAD_DIM
GMLP_WIDTH = 512
GMLP_GROUPS = 8
GMLP_CHUNK = 128
CMP_BLOCK = 32
CMP_HIDDEN = 256
CMP_PITCH = CMP_BLOCK + SUBLANES
SLC_BLOCK = 64
N_SELECT = 16
WINDOW = 512
QUERY_BLOCK = 128
RANK_TOKENS = LANES
RANK_TILE_GROUP = 4
PAGE_SIZE = 128
FORCE_BONUS = 1.0e4
ROPE_THETA = 10000.0
LN_EPS = 1e-5
NEG_INF = -1e30
MAX_SLC_BLOCKS = LANES
SAMPLE_ROWS = SUBLANES
SLC_KEY_TILE = 1024
VMEM_LIMIT = 56 * 1024 * 1024
Q_SCALE = HEAD_DIM ** -0.5 * math.log2(math.e)

C_Q = 0
C_CMP = C_Q + NSA_WIDTH
C_SLC = C_CMP + 2 * KV_WIDTH
C_WIN = C_SLC + 2 * KV_WIDTH
C_GATE = C_WIN + 2 * KV_WIDTH
C_U = C_GATE + LANES
C_V = C_U + GMLP_WIDTH
C_END = C_V + GMLP_WIDTH


def _dot(a, b):
    return jnp.dot(a, b, preferred_element_type=F32)


def _dot_nt(a, b):
    return lax.dot_general(a, b, (((1,), (1,)), ((), ())), preferred_element_type=F32)


def _layer_norm(x, g, b):
    mu = jnp.mean(x, axis=-1, keepdims=True)
    xc = x - mu
    var = jnp.mean(xc * xc, axis=-1, keepdims=True)
    return xc * lax.rsqrt(var + LN_EPS) * g + b


def _softmax_masked(s, mask):
    s = jnp.where(mask, s, NEG_INF)
    m = jnp.max(s, axis=-1, keepdims=True)
    e = jnp.where(mask, jnp.exp2(s - m), 0.0)
    return e * (1.0 / jnp.maximum(jnp.sum(e, axis=-1, keepdims=True), 1e-30))


def _rope_lanes(c, cos, sin, first_half):
    partner = jnp.where(first_half, pltpu.roll(c, LANES - HEAD_DIM // 2, 1),
                        pltpu.roll(c, HEAD_DIM // 2, 1))
    return c * cos + partner * sin


def _inproj_common(xb, wt_ref, cos, sin, first_half, lng_ref, lnb_ref,
                   qraw_ref, qrot_ref, gate_ref, u_ref, v_ref):
    zq = _dot_nt(xb, wt_ref[C_Q:C_CMP, :])
    qraw_ref[...] = (zq * Q_SCALE).astype(qraw_ref.dtype)
    for c in range(NSA_WIDTH // LANES):
        qrot_ref[:, c * LANES:(c + 1) * LANES] = (_rope_lanes(
            zq[:, c * LANES:(c + 1) * LANES], cos, sin, first_half) * Q_SCALE).astype(qrot_ref.dtype)
    gate_ref[...] = jax.nn.sigmoid(_dot_nt(xb, wt_ref[C_GATE:C_U, :]))
    u_ref[...] = jax.nn.gelu(_dot_nt(xb, wt_ref[C_U:C_V, :])).astype(u_ref.dtype)
    zv = jax.nn.gelu(_dot_nt(xb, wt_ref[C_V:C_END, :]))
    v_ref[...] = _layer_norm(zv, lng_ref[...], lnb_ref[...]).astype(v_ref.dtype)


def _inproj_prompt_kernel(n_carried, x_ref, wt_ref, cos_ref, sin_ref, cost_ref, sint_ref, lng_ref,
                          lnb_ref, *refs):
    (qraw_ref, qrot_ref, gate_ref, u_ref, v_ref,
     cmp_ref, cmpt_ref, slct_ref, wint_ref, slcb_ref, winb_ref) = refs[n_carried:]
    xb = x_ref[...].astype(BF16)
    cos = cos_ref[...]
    lane = lax.broadcasted_iota(jnp.int32, cos.shape, 1)
    first_half = (lane % HEAD_DIM) < (HEAD_DIM // 2)
    _inproj_common(xb, wt_ref, cos, sin_ref[...], first_half, lng_ref, lnb_ref,
                   qraw_ref, qrot_ref, gate_ref, u_ref, v_ref)
    cmp_ref[...] = _dot_nt(xb, wt_ref[C_CMP:C_SLC, :])
    cmpt_ref[...] = _dot_nt(wt_ref[C_CMP:C_SLC, :], xb)
    cost = cost_ref[...]
    sint = sint_ref[...]
    half = HEAD_DIM // 2
    for row0, f_ref, b_ref in ((C_SLC, slct_ref, slcb_ref), (C_WIN, wint_ref, winb_ref)):
        zt = _dot_nt(wt_ref[row0:row0 + 2 * KV_WIDTH, :], xb)
        for g in range(KV_HEADS):
            kg = zt[g * HEAD_DIM:(g + 1) * HEAD_DIM, :]
            partner = jnp.concatenate([kg[half:, :], kg[:half, :]], axis=0)
            kr = kg * cost + partner * sint
            f_ref[g * HEAD_DIM:(g + 1) * HEAD_DIM, :] = kr
            b_ref[g * HEAD_DIM:(g + 1) * HEAD_DIM, :] = kr.astype(BF16)
        f_ref[KV_WIDTH:, :] = zt[KV_WIDTH:, :]
        b_ref[KV_WIDTH:, :] = zt[KV_WIDTH:, :].astype(BF16)


def _inproj_sample_kernel(x_ref, wt_ref, cos_ref, sin_ref, lng_ref, lnb_ref,
                          qraw_ref, qrot_ref, gate_ref, u_ref, v_ref, cmp_ref, slc_ref, win_ref):
    xb = x_ref[...].astype(BF16)
    cos = cos_ref[...]
    sin = sin_ref[...]
    lane = lax.broadcasted_iota(jnp.int32, cos.shape, 1)
    first_half = (lane % HEAD_DIM) < (HEAD_DIM // 2)
    _inproj_common(xb, wt_ref, cos, sin, first_half, lng_ref, lnb_ref,
                   qraw_ref, qrot_ref, gate_ref, u_ref, v_ref)
    cmp_ref[...] = _dot_nt(xb, wt_ref[C_CMP:C_SLC, :])
    for row0, f_ref in ((C_SLC, slc_ref), (C_WIN, win_ref)):
        z = _dot_nt(xb, wt_ref[row0:row0 + 2 * KV_WIDTH, :])
        f_ref[:, :KV_WIDTH] = _rope_lanes(z[:, :KV_WIDTH], cos, sin, first_half)
        f_ref[:, KV_WIDTH:] = z[:, KV_WIDTH:]


def _inproj_prompt(x, wt, cos, sin, cost, sint, ln_g, ln_b, tm, b, t, depth, layer, carried):
    n, d = x.shape
    n_tab = t // tm
    row = lambda width: pl.BlockSpec((tm, width), lambda i: (i, 0))
    const = lambda a: pl.BlockSpec(a.shape, lambda i: (0,) * a.ndim)
    tab = pl.BlockSpec((tm, LANES), lambda i: (i % n_tab, 0))
    tabt = pl.BlockSpec((HEAD_DIM, tm), lambda i: (0, i % n_tab))
    fm = pl.BlockSpec((None, 2 * KV_WIDTH, tm), lambda i: (i // n_tab, 0, i % n_tab))
    fm_layer = pl.BlockSpec((None, None, 2 * KV_WIDTH, tm), lambda i: (layer, i // n_tab, 0, i % n_tab))
    keep_layer = pl.BlockSpec((None, None, 2 * KV_WIDTH, tm), lambda i: (layer, i // n_tab, 0, 0))
    row_out = ((NSA_WIDTH, BF16), (NSA_WIDTH, BF16), (LANES, F32), (GMLP_WIDTH, BF16), (GMLP_WIDTH, BF16),
               (2 * KV_WIDTH, F32))
    n_in = 8
    n_row = len(row_out)
    stacked = lambda width: jax.ShapeDtypeStruct((depth, b, 2 * KV_WIDTH, width), F32)
    return pl.pallas_call(
        functools.partial(_inproj_prompt_kernel, len(carried)),
        grid=(n // tm,),
        in_specs=[row(d), const(wt), tab, tab, tabt, tabt, const(ln_g), const(ln_b)]
        + [pl.BlockSpec(memory_space=pl.ANY)] * len(carried),
        out_specs=[row(wd) for wd, _ in row_out] + [fm_layer, fm_layer, keep_layer, fm, fm],
        out_shape=[jax.ShapeDtypeStruct((n, wd), dt) for wd, dt in row_out]
        + [stacked(t), stacked(t), stacked(tm)]
        + [jax.ShapeDtypeStruct((b, 2 * KV_WIDTH, t), BF16)] * 2,
        input_output_aliases={n_in + k: n_row + k for k in range(len(carried))},
        compiler_params=pltpu.CompilerParams(dimension_semantics=("arbitrary",),
                                             vmem_limit_bytes=VMEM_LIMIT),
        name="inproj_prompt",
    )(x, wt, cos, sin, cost, sint, ln_g, ln_b, *carried)


def _inproj_sample(x, wt, cos, sin, ln_g, ln_b, tm):
    n, d = x.shape
    row = lambda width: pl.BlockSpec((tm, width), lambda i: (i, 0))
    const = lambda a: pl.BlockSpec(a.shape, lambda i: (0,) * a.ndim)
    widths = (NSA_WIDTH, NSA_WIDTH, LANES, GMLP_WIDTH, GMLP_WIDTH,
              2 * KV_WIDTH, 2 * KV_WIDTH, 2 * KV_WIDTH)
    return pl.pallas_call(
        _inproj_sample_kernel,
        grid=(n // tm,),
        in_specs=[row(d), const(wt), const(cos), const(sin), const(ln_g), const(ln_b)],
        out_specs=[row(wd) for wd in widths],
        out_shape=[jax.ShapeDtypeStruct((n, wd), F32) for wd in widths],
        compiler_params=pltpu.CompilerParams(dimension_semantics=("parallel",),
                                             vmem_limit_bytes=VMEM_LIMIT),
        name="inproj_sample",
    )(x, wt, cos, sin, ln_g, ln_b)


def _compress_rows(chunk, pe_ref, w1_ref, b1_ref, w2_ref):
    outs = []
    for k in range(2):
        a = jnp.concatenate([chunk(l, k) for l in range(CMP_BLOCK)], axis=1)
        a = (a + pe_ref[k]).astype(BF16)
        h = jax.nn.silu(_dot(a, w1_ref[k]) + b1_ref[k])
        outs.append(_dot(h.astype(BF16), w2_ref[k]))
    return jnp.concatenate(outs, axis=1)


def _compress_prompt_kernel(x_ref, pe_ref, w1_ref, b1_ref, w2_ref, o_ref, posk_ref, posv_ref):
    tm = o_ref.shape[0]
    gap = jnp.zeros((tm, CMP_PITCH - CMP_BLOCK, KV_WIDTH), F32)
    for k, dst in enumerate((posk_ref, posv_ref)):
        rows = x_ref[:, k * KV_WIDTH:(k + 1) * KV_WIDTH].reshape(tm, CMP_BLOCK, KV_WIDTH)
        dst[...] = jnp.concatenate([rows, gap], axis=1).reshape(tm * CMP_PITCH, KV_WIDTH)
    chunk = lambda l, k: (posk_ref, posv_ref)[k][pl.ds(l, tm, stride=CMP_PITCH), :]
    o_ref[...] = _compress_rows(chunk, pe_ref, w1_ref, b1_ref, w2_ref)


def _compress_prompt(x, pe, w1, b1, w2, tm):
    n = x.shape[0] // CMP_BLOCK
    const = lambda a: pl.BlockSpec(a.shape, lambda i: (0,) * a.ndim)
    return pl.pallas_call(
        _compress_prompt_kernel,
        grid=(n // tm,),
        in_specs=[pl.BlockSpec((tm * CMP_BLOCK, x.shape[1]), lambda i: (i, 0)),
                  const(pe), const(w1), const(b1), const(w2)],
        out_specs=pl.BlockSpec((tm, 2 * KV_WIDTH), lambda i: (i, 0)),
        out_shape=jax.ShapeDtypeStruct((n, 2 * KV_WIDTH), F32),
        scratch_shapes=[pltpu.VMEM((tm * CMP_PITCH, KV_WIDTH), F32)] * 2,
        compiler_params=pltpu.CompilerParams(dimension_semantics=("parallel",),
                                             vmem_limit_bytes=VMEM_LIMIT),
        name="compress_prompt",
    )(x, pe, w1, b1, w2)


def _compress_sample_kernel(n_seq, n_pages, pt_ref, *refs):
    page_refs = refs[:n_seq * n_pages]
    pe_ref, w1_ref, b1_ref, w2_ref, o_ref, posk_ref, posv_ref = refs[n_seq * n_pages:]
    blocks_per_page = PAGE_SIZE // CMP_BLOCK
    for j, page_ref in enumerate(page_refs):
        for dst, rows in ((posk_ref, page_ref[:KV_WIDTH, :].T), (posv_ref, page_ref[KV_WIDTH:, :].T)):
            for n in range(blocks_per_page):
                r0 = (j * blocks_per_page + n) * CMP_PITCH
                dst[r0:r0 + CMP_BLOCK, :] = rows[n * CMP_BLOCK:(n + 1) * CMP_BLOCK, :]
    m = n_seq * n_pages * blocks_per_page
    chunk = lambda l, k: (posk_ref, posv_ref)[k][pl.ds(l, m, stride=CMP_PITCH), :]
    o_ref[...] = _compress_rows(chunk, pe_ref, w1_ref, b1_ref, w2_ref)


def _compress_sample(pool, layer, page_table, pe, w1, b1, w2, n_seq):
    db, n_pages = page_table.shape
    rows = n_seq * n_pages * (PAGE_SIZE // CMP_BLOCK)
    const = lambda a: pl.BlockSpec(a.shape, lambda i, pt: (0,) * a.ndim)

    def page_spec(s, p):
        return pl.BlockSpec((None, None) + pool.shape[2:],
                            lambda i, pt: (layer, pt[i * n_seq + s, p], 0, 0))

    grid_spec = pltpu.PrefetchScalarGridSpec(
        num_scalar_prefetch=1,
        grid=(db // n_seq,),
        in_specs=[page_spec(s, p) for s in range(n_seq) for p in range(n_pages)]
        + [const(pe), const(w1), const(b1), const(w2)],
        out_specs=pl.BlockSpec((rows, 2 * KV_WIDTH), lambda i, pt: (i, 0)),
        scratch_shapes=[pltpu.VMEM((rows * CMP_PITCH, KV_WIDTH), F32)] * 2,
    )
    return pl.pallas_call(
        functools.partial(_compress_sample_kernel, n_seq, n_pages),
        grid_spec=grid_spec,
        out_shape=jax.ShapeDtypeStruct((db * n_pages * (PAGE_SIZE // CMP_BLOCK), 2 * KV_WIDTH), F32),
        compiler_params=pltpu.CompilerParams(dimension_semantics=("parallel",),
                                             vmem_limit_bytes=VMEM_LIMIT),
        name="compress_sample",
    )(page_table, *([pool] * (n_seq * n_pages)), pe, w1, b1, w2)


def _stack_heads(q, tq):
    lane_lo = lax.broadcasted_iota(jnp.int32, (tq, LANES), 1) < HEAD_DIM
    pieces = []
    for hh in range(NSA_HEADS):
        g = hh // GROUP
        c = hh // 2
        chunk = q[:, c * LANES:(c + 1) * LANES]
        if hh % 2 != g:
            chunk = pltpu.roll(chunk, HEAD_DIM, 1)
        pieces.append(jnp.where(lane_lo if g == 0 else jnp.logical_not(lane_lo), chunk, 0.0))
    return jnp.concatenate(pieces, axis=0).astype(BF16)


def _cmp_branch(qraw, ckv, qpos, tq):
    nb = MAX_SLC_BLOCKS
    kc = jnp.concatenate([ckv[:, 0:KV_WIDTH], ckv[:, 2 * KV_WIDTH:3 * KV_WIDTH]], axis=0).astype(BF16)
    vc = jnp.concatenate([ckv[:, KV_WIDTH:2 * KV_WIDTH], ckv[:, 3 * KV_WIDTH:]], axis=0).astype(BF16)
    s = _dot_nt(qraw, kc).reshape(NSA_HEADS, tq, 2 * nb)
    lane = lax.broadcasted_iota(jnp.int32, (tq, 2 * nb), 1)
    blk = jnp.where(lane < nb, 2 * lane, 2 * (lane - nb) + 1)
    mask = ((blk + 1) * CMP_BLOCK - 1) <= qpos
    p = _softmax_masked(s, mask[None])
    o_cmp = _dot(p.reshape(NSA_HEADS * tq, 2 * nb).astype(BF16), vc)
    scores = []
    sblk = lax.broadcasted_iota(jnp.int32, (tq, nb), 1)
    cur = qpos // SLC_BLOCK
    valid = sblk <= cur
    forced = ((sblk == 0) | (sblk == cur) | (sblk == cur - 1)).astype(F32)
    for g in range(KV_HEADS):
        imp = p[g * GROUP]
        for r in range(1, GROUP):
            imp = imp + p[g * GROUP + r]
        imp = imp[:, :nb] + imp[:, nb:]
        scores.append(jnp.where(valid, imp + FORCE_BONUS * forced, -1.0))
    return o_cmp, scores, valid


def _rank_blocks_wide(score, st_ref, cnt_ref, n_chunks):
    st_ref[...] = score.T
    cnt_ref[...] = jnp.zeros_like(cnt_ref)
    sub = lax.broadcasted_iota(jnp.int32, (SUBLANES, LANES), 0)
    n_vregs = MAX_SLC_BLOCKS // SUBLANES
    def count(c, v_lo, v_hi):
        cand = st_ref[c * SUBLANES:(c + 1) * SUBLANES, :]
        tiles = {v: st_ref[v * SUBLANES:(v + 1) * SUBLANES, :] for v in range(v_lo, v_hi)}
        cnts = {v: cnt_ref[v * SUBLANES:(v + 1) * SUBLANES, :] for v in range(v_lo, v_hi)}
        for ii in range(SUBLANES):
            row = cand[ii:ii + 1, :]
            for v in range(v_lo, v_hi):
                if v < c:
                    inc = jnp.where(row > tiles[v], 1.0, 0.0)
                elif v > c:
                    inc = jnp.where(row >= tiles[v], 1.0, 0.0)
                else:
                    inc = jnp.where(sub > ii, jnp.where(row >= tiles[v], 1.0, 0.0),
                                    jnp.where(row > tiles[v], 1.0, 0.0))
                cnts[v] = cnts[v] + inc
        for v in range(v_lo, v_hi):
            cnt_ref[v * SUBLANES:(v + 1) * SUBLANES, :] = cnts[v]

    for c in range(n_vregs):
        @pl.when(c < n_chunks)
        def _():
            for v_lo in range(0, n_vregs, RANK_TILE_GROUP):
                v_hi = v_lo + RANK_TILE_GROUP
                if v_lo <= c:
                    count(c, v_lo, v_hi)
                else:
                    pl.when(v_lo < n_chunks)(functools.partial(count, c, v_lo, v_hi))
    return cnt_ref[...].T


def _rank_blocks_narrow(score, n_blocks):
    lane = lax.broadcasted_iota(jnp.int32, score.shape, 1)
    cnt = jnp.zeros(score.shape, F32)
    for i in range(n_blocks):
        col = score[:, i:i + 1]
        cnt = cnt + jnp.where(lane > i, jnp.where(col >= score, 1.0, 0.0),
                              jnp.where(col > score, 1.0, 0.0))
    return cnt


def _slc_branch(q_aug, kv_ref, et_ref, n_full, qpos, tq, tk):
    ones_rows = jnp.where(lax.broadcasted_iota(jnp.int32, (HEAD_DIM, tk), 0) == 0, 1.0, 0.0).astype(BF16)

    def tile(kt, carry, diagonal):
        ms, accs = carry
        off = pl.multiple_of(kt * tk, tk)
        ka = jnp.concatenate([et_ref[:, pl.ds(off, tk)], kv_ref[:KV_WIDTH, pl.ds(off, tk)]], axis=0)
        if diagonal:
            visible = (off + lax.broadcasted_iota(jnp.int32, (tq, tk), 1)) <= qpos
        m_new, acc_new = [], []
        for g in range(KV_HEADS):
            s = _dot(q_aug[g * GROUP * tq:(g + 1) * GROUP * tq, :], ka)
            ps, alphas = [], []
            for r in range(GROUP):
                hh = g * GROUP + r
                s_h = s[r * tq:(r + 1) * tq, :]
                if diagonal:
                    s_h = jnp.where(visible, s_h, NEG_INF)
                m_h = jnp.maximum(ms[hh], jnp.max(s_h, axis=-1, keepdims=True))
                ps.append(jnp.exp2(s_h - m_h).astype(BF16))
                alphas.append(jnp.exp2(ms[hh] - m_h))
                m_new.append(m_h)
            v0 = KV_WIDTH + g * HEAD_DIM
            va = jnp.concatenate([kv_ref[v0:v0 + HEAD_DIM, pl.ds(off, tk)], ones_rows], axis=0)
            pv = _dot_nt(jnp.concatenate(ps, axis=0), va)
            acc_new.append(jnp.concatenate(alphas, axis=0) * accs[g] + pv)
        return tuple(m_new), tuple(acc_new)

    init = (tuple(jnp.full((tq, 1), NEG_INF, F32) for _ in range(NSA_HEADS)),
            tuple(jnp.zeros((GROUP * tq, KV_WIDTH), F32) for _ in range(KV_HEADS)))
    def first_group(count):
        def run(c):
            c = tile(n_full, c, True)
            for u in range(count - 1):
                c = tile(u, c, False)
            return c
        return run

    first = (n_full + 1) % 4
    carry = lax.switch(first, [first_group(4), first_group(1), first_group(2), first_group(3)], init)
    taken = jnp.where(first == 0, 3, first - 1)

    def quad(j, c):
        kt = taken + 4 * j
        for u in range(4):
            c = tile(kt + u, c, False)
        return c

    _, accs = lax.fori_loop(0, (n_full - taken) // 4, quad, carry)
    outs = [a * (1.0 / jnp.maximum(a[:, HEAD_DIM:HEAD_DIM + 1], 1e-30)) for a in accs]
    return jnp.concatenate([outs[0], pltpu.roll(outs[1], HEAD_DIM, 1)], axis=0)


def _softmax_pv(s, mask, vt, tq):
    nk = s.shape[1]
    rows = NSA_HEADS * tq
    if rows <= LANES:
        s3 = jnp.where(mask[None], s.reshape(NSA_HEADS, tq, nk), NEG_INF)
        p = jnp.exp2(s3 - jnp.max(s3, axis=-1, keepdims=True)).reshape(rows, nk).astype(BF16)
        if rows < LANES:
            p = jnp.concatenate([p, jnp.zeros((LANES - rows, nk), BF16)], axis=0)
        ones_rows = jnp.where(lax.broadcasted_iota(jnp.int32, (2 * SUBLANES, nk), 0) == 0,
                              1.0, 0.0).astype(BF16)
        ot = _dot_nt(jnp.concatenate([vt, ones_rows], axis=0), p)
        return (ot[:KV_WIDTH] * (1.0 / jnp.maximum(ot[KV_WIDTH:KV_WIDTH + 1], 1e-30))).T[:rows]
    ones_rows = jnp.where(lax.broadcasted_iota(jnp.int32, (HEAD_DIM, nk), 0) == 0, 1.0, 0.0).astype(BF16)
    outs = []
    for g in range(KV_HEADS):
        ps = []
        for hh in range(g * GROUP, (g + 1) * GROUP):
            s_h = jnp.where(mask, s[hh * tq:(hh + 1) * tq, :], NEG_INF)
            ps.append(jnp.exp2(s_h - jnp.max(s_h, axis=-1, keepdims=True)).astype(BF16))
        va = jnp.concatenate([vt[g * HEAD_DIM:(g + 1) * HEAD_DIM, :], ones_rows], axis=0)
        pv = _dot_nt(jnp.concatenate(ps, axis=0), va)
        outs.append(pv * (1.0 / jnp.maximum(pv[:, HEAD_DIM:HEAD_DIM + 1], 1e-30)))
    return jnp.concatenate([outs[0], pltpu.roll(outs[1], HEAD_DIM, 1)], axis=0)


def _slc_single(bias_rows, qrot, kt, vt, et, qpos, tq):
    nk = kt.shape[1]
    s = _dot(bias_rows, et) + _dot(qrot, kt)
    visible = lax.broadcasted_iota(jnp.int32, (tq, nk), 1) <= qpos
    return _softmax_pv(s, visible, vt, tq)


def _win_branch(qrot, kt, vt, kpos, qpos, tq):
    diff = qpos - kpos
    return _softmax_pv(_dot(qrot, kt), (diff >= 0) & (diff < WINDOW), vt, tq)


def _combine(o_cmp, o_slc, o_win, gates, tq):
    lane_lo = lax.broadcasted_iota(jnp.int32, (tq, LANES), 1) < HEAD_DIM
    heads = []
    for hh in range(NSA_HEADS):
        tot = None
        for br, o in enumerate((o_cmp, o_slc, o_win)):
            col = br * NSA_HEADS + hh
            term = gates[:, col:col + 1] * o[hh * tq:(hh + 1) * tq, :]
            tot = term if tot is None else tot + term
        heads.append(tot)
    chunks = []
    for c in range(NSA_HEADS // 2):
        a, b = heads[2 * c], heads[2 * c + 1]
        if c // 2 == 0:
            chunks.append(jnp.where(lane_lo, a, pltpu.roll(b, HEAD_DIM, 1)))
        else:
            chunks.append(jnp.where(lane_lo, pltpu.roll(a, HEAD_DIM, 1), b))
    return jnp.concatenate(chunks, axis=1)


def _selection_bias(rank, valid):
    return jnp.where((rank < float(N_SELECT)) & valid, 0.0, NEG_INF).astype(BF16)


def _pad_ckv(ckv):
    n = ckv.shape[0]
    if n == MAX_SLC_BLOCKS:
        return ckv
    return jnp.concatenate([ckv, jnp.zeros((MAX_SLC_BLOCKS - n, ckv.shape[1]), F32)], axis=0)


def _nsa_prompt_kernel(qraw_ref, qrot_ref, gate_ref, ckv_ref, slc_ref, win_ref, et_ref,
                       o_ref, st_ref, cnt_ref):
    tq = QUERY_BLOCK
    i = pl.program_id(1)
    qpos0 = i * tq
    qpos = qpos0 + lax.broadcasted_iota(jnp.int32, (tq, 1), 0)
    qraw = _stack_heads(qraw_ref[...].astype(F32), tq)
    qrot = _stack_heads(qrot_ref[...].astype(F32), tq)

    o_cmp, scores, valid = _cmp_branch(qraw, _pad_ckv(ckv_ref[...]), qpos, tq)
    n_chunks = (qpos0 + tq - 1) // (SLC_BLOCK * SUBLANES) + 1
    bias = []
    for g in range(KV_HEADS):
        parts = []
        for t0 in range(0, tq, RANK_TOKENS):
            rank = _rank_blocks_wide(scores[g][t0:t0 + RANK_TOKENS], st_ref, cnt_ref, n_chunks)
            parts.append(_selection_bias(rank, valid[t0:t0 + RANK_TOKENS]))
        bias.append(jnp.concatenate(parts, axis=0))
    bias_rows = jnp.concatenate([bias[0]] * GROUP + [bias[1]] * GROUP, axis=0)
    q_aug = jnp.concatenate([bias_rows, qrot], axis=1)
    o_slc = _slc_branch(q_aug, slc_ref, et_ref, qpos0 // SLC_KEY_TILE, qpos, tq, SLC_KEY_TILE)

    nk = WINDOW + tq
    start = pl.multiple_of(jnp.maximum(qpos0 - WINDOW, 0), tq)
    kpos = start + lax.broadcasted_iota(jnp.int32, (1, nk), 1)
    o_win = _win_branch(qrot, win_ref[:KV_WIDTH, pl.ds(start, nk)], win_ref[KV_WIDTH:, pl.ds(start, nk)],
                        kpos, qpos, tq)

    o_ref[...] = _combine(o_cmp, o_slc, o_win, gate_ref[...], tq).astype(o_ref.dtype)


def _nsa_prompt(qraw, qrot, gates, ckv, slcb, winb, et):
    b, t, _ = qraw.shape
    tq = QUERY_BLOCK
    blk = lambda width: pl.BlockSpec((None, tq, width), lambda bi, i: (bi, i, 0))
    per_batch = lambda a: pl.BlockSpec((None,) + a.shape[1:], lambda bi, i: (bi, 0, 0))
    return pl.pallas_call(
        _nsa_prompt_kernel,
        grid=(b, t // tq),
        in_specs=[blk(NSA_WIDTH), blk(NSA_WIDTH), blk(LANES), per_batch(ckv), per_batch(slcb),
                  per_batch(winb), pl.BlockSpec(et.shape, lambda bi, i: (0, 0))],
        out_specs=blk(NSA_WIDTH),
        out_shape=jax.ShapeDtypeStruct((b, t, NSA_WIDTH), BF16),
        scratch_shapes=[pltpu.VMEM((MAX_SLC_BLOCKS, RANK_TOKENS), F32)] * 2,
        compiler_params=pltpu.CompilerParams(dimension_semantics=("parallel", "arbitrary"),
                                             vmem_limit_bytes=VMEM_LIMIT),
        name="nsa_prompt",
    )(qraw, qrot, gates, ckv, slcb, winb, et)


def _nsa_sample_kernel(n_seq, n_pages, past, s_new, pt_ref, *refs):
    n_in = n_seq * n_pages
    page_refs = refs[:n_in]
    (qraw_ref, qrot_ref, gate_ref, ckv_ref, slcn_ref, wins_ref, winn_ref, et_ref) = refs[n_in:n_in + 8]
    o_ref, wout_ref = refs[-2:]
    tq = SAMPLE_ROWS
    qpos = past + lax.broadcasted_iota(jnp.int32, (tq, 1), 0)
    n_buf = wins_ref.shape[2]
    n_blocks = (past + tq - 1) // SLC_BLOCK + 1
    pad_rows = jnp.zeros((LANES - tq, 2 * KV_WIDTH), F32)
    wpos = past - n_buf + lax.broadcasted_iota(jnp.int32, (1, n_buf + LANES), 1)
    lane = lax.broadcasted_iota(jnp.int32, (2 * KV_WIDTH, LANES), 1)
    for s in range(n_seq):
        slc_new = jnp.concatenate([slcn_ref[s], pad_rows], axis=0).T
        tiles = [page_refs[s * n_pages + p][...].astype(BF16) for p in range(n_pages)]
        tiles.append(slc_new.astype(BF16))
        kt = jnp.concatenate([tl[:KV_WIDTH] for tl in tiles], axis=1)
        vt = jnp.concatenate([tl[KV_WIDTH:] for tl in tiles], axis=1)

        qraw = _stack_heads(qraw_ref[s], tq)
        qrot = _stack_heads(qrot_ref[s], tq)
        o_cmp, scores, valid = _cmp_branch(qraw, _pad_ckv(ckv_ref[s]), qpos, tq)
        bias = [_selection_bias(_rank_blocks_narrow(scores[g], n_blocks), valid)
                for g in range(KV_HEADS)]
        bias_rows = jnp.concatenate([bias[0]] * GROUP + [bias[1]] * GROUP, axis=0)
        o_slc = _slc_single(bias_rows, qrot, kt, vt, et_ref[...], qpos, tq)

        state = wins_ref[s]
        win_new = jnp.concatenate([winn_ref[s], pad_rows], axis=0).T
        kt = jnp.concatenate([state[:KV_WIDTH], win_new[:KV_WIDTH]], axis=1).astype(BF16)
        vt = jnp.concatenate([state[KV_WIDTH:], win_new[KV_WIDTH:]], axis=1).astype(BF16)
        o_win = _win_branch(qrot, kt, vt, wpos, qpos, tq)
        o_ref[s] = _combine(o_cmp, o_slc, o_win, gate_ref[s], tq)

        shifted = pltpu.roll(state, n_buf - s_new, 1)
        tail = jnp.where(lane >= LANES - s_new, pltpu.roll(win_new, LANES - s_new, 1),
                         shifted[:, n_buf - LANES:])
        wout_ref[s, :, :n_buf - LANES] = shifted[:, :n_buf - LANES]
        wout_ref[s, :, n_buf - LANES:] = tail


def _nsa_sample(pool, win_state, layer, page_table, qraw, qrot, gates, ckv, slc_new, win_new, et,
                n_seq, s_new, carried):
    db, n_pages = page_table.shape
    past = n_pages * PAGE_SIZE
    tq = SAMPLE_ROWS
    n_buf = win_state.shape[3]
    seq = lambda a: pl.BlockSpec((n_seq,) + a.shape[1:], lambda i, pt: (i, 0, 0))

    def page_spec(s, p):
        return pl.BlockSpec((None, None) + pool.shape[2:],
                            lambda i, pt: (layer, pt[i * n_seq + s, p], 0, 0))

    state_spec = pl.BlockSpec((None, n_seq) + win_state.shape[2:], lambda i, pt: (layer, i, 0, 0))
    grid_spec = pltpu.PrefetchScalarGridSpec(
        num_scalar_prefetch=1,
        grid=(db // n_seq,),
        in_specs=[page_spec(s, p) for s in range(n_seq) for p in range(n_pages)]
        + [seq(qraw), seq(qrot), seq(gates), seq(ckv), seq(slc_new), state_spec, seq(win_new),
           pl.BlockSpec(et.shape, lambda i, pt: (0, 0))]
        + [pl.BlockSpec(memory_space=pl.ANY)] * len(carried),
        out_specs=[pl.BlockSpec((n_seq, tq, NSA_WIDTH), lambda i, pt: (i, 0, 0)),
                   pl.BlockSpec((None, n_seq) + win_state.shape[2:], lambda i, pt: (layer, i, 0, 0))],
    )
    n_operands = 1 + n_seq * n_pages + 8
    return pl.pallas_call(
        functools.partial(_nsa_sample_kernel, n_seq, n_pages, past, s_new),
        grid_spec=grid_spec,
        out_shape=[jax.ShapeDtypeStruct((db, tq, NSA_WIDTH), F32),
                   jax.ShapeDtypeStruct(win_state.shape, F32)],
        input_output_aliases={n_operands + k: 1 + k for k in range(len(carried))},
        compiler_params=pltpu.CompilerParams(dimension_semantics=("parallel",),
                                             vmem_limit_bytes=VMEM_LIMIT),
        name="nsa_sample",
    )(page_table, *([pool] * (n_seq * n_pages)), qraw, qrot, gates, ckv, slc_new, win_state, win_new, et,
      *carried)


def _post_kernel(chunk, alpha, n_ffn_chunks, x_ref, o_ref, u_ref, v_ref, gw_ref, gb_ref,
                 wout_ref, ln1g_ref, ln1b_ref, wg_ref, wu_ref, wd_ref, ln2g_ref, ln2b_ref, y_ref):
    tm = x_ref.shape[0]
    c = GMLP_CHUNK
    row = lax.broadcasted_iota(jnp.int32, (c, c), 0)
    col = lax.broadcasted_iota(jnp.int32, (c, c), 1)
    causal = (col <= row) & ((row // chunk) == (col // chunk))
    w_cat = jnp.concatenate([jnp.where(causal, gw_ref[h], 0.0) for h in range(GMLP_GROUPS)],
                            axis=1).astype(BF16)
    lane_group = lax.broadcasted_iota(jnp.int32, (c, GMLP_WIDTH), 1) // (GMLP_WIDTH // GMLP_GROUPS)
    mixed = []
    for sub in range(tm // c):
        v = v_ref[sub * c:(sub + 1) * c, :]
        v_diag = jnp.concatenate([jnp.where(lane_group == h, v, 0.0) for h in range(GMLP_GROUPS)],
                                 axis=0).astype(BF16)
        s = _dot(w_cat, v_diag) + gb_ref[...]
        mixed.append(u_ref[sub * c:(sub + 1) * c, :].astype(F32) * s)
    o_gmlp = jnp.concatenate(mixed, axis=0)
    mix = jnp.concatenate([o_ref[...], o_gmlp], axis=1).astype(BF16)
    h = _dot(mix, wout_ref[...])
    x1 = _layer_norm(alpha * x_ref[...] + h, ln1g_ref[...], ln1b_ref[...])
    x1b = x1.astype(BF16)
    f = None
    fc = wg_ref.shape[1] // n_ffn_chunks
    for j in range(n_ffn_chunks):
        gate = _dot(x1b, wg_ref[:, j * fc:(j + 1) * fc])
        up = _dot(x1b, wu_ref[:, j * fc:(j + 1) * fc])
        a = (jax.nn.silu(gate) * up).astype(BF16)
        part = _dot(a, wd_ref[j * fc:(j + 1) * fc, :])
        f = part if f is None else f + part
    y_ref[...] = _layer_norm(alpha * x1 + f, ln2g_ref[...], ln2b_ref[...])


def _post(x, o_nsa, u, v, gw, gb, wout, ln1g, ln1b, wg, wu, wd, ln2g, ln2b, chunk, alpha, tm):
    n, d = x.shape
    ffn = wg.shape[1]
    n_ffn_chunks = next(k for k in (4, 2, 1) if ffn % (k * LANES) == 0)
    row = lambda width: pl.BlockSpec((tm, width), lambda i: (i, 0))
    const = lambda a: pl.BlockSpec(a.shape, lambda i: (0,) * a.ndim, pipeline_mode=pl.Buffered(1))
    consts = (gw, gb, wout, ln1g, ln1b, wg, wu, wd, ln2g, ln2b)
    return pl.pallas_call(
        functools.partial(_post_kernel, chunk, alpha, n_ffn_chunks),
        grid=(n // tm,),
        in_specs=[row(d), row(NSA_WIDTH), row(GMLP_WIDTH), row(GMLP_WIDTH)] + [const(a) for a in consts],
        out_specs=row(d),
        out_shape=jax.ShapeDtypeStruct((n, d), F32),
        compiler_params=pltpu.CompilerParams(dimension_semantics=("parallel",),
                                             vmem_limit_bytes=VMEM_LIMIT),
        name="post",
    )(x, o_nsa, u, v, *consts)


def _pack_w_in_t(w_in):
    wt = w_in.T
    o_gate = NSA_WIDTH + 3 * 2 * KV_WIDTH
    o_gmlp = o_gate + 3 * NSA_HEADS
    gate = jnp.pad(wt[o_gate:o_gmlp], ((0, LANES - 3 * NSA_HEADS), (0, 0)))
    return jnp.concatenate([wt[:o_gate], gate, wt[o_gmlp:]], axis=0).astype(BF16)


def _pack_compress(pe, w1, b1, w2):
    eye = jnp.eye(KV_HEADS, dtype=F32)
    zero = jnp.zeros_like(w1)
    w1p = jnp.stack([jnp.concatenate([w1, zero], axis=-1), jnp.concatenate([zero, w1], axis=-1)], axis=2)
    w1p = w1p.reshape(2, CMP_BLOCK * KV_WIDTH, KV_HEADS * CMP_HIDDEN)
    pep = jnp.broadcast_to(pe[:, :, None, :], (2, CMP_BLOCK, KV_HEADS, HEAD_DIM)).reshape(2, 1, -1)
    b1p = jnp.tile(b1, (1, KV_HEADS))[:, None, :]
    w2p = jnp.einsum('khd,gG->kghGd', w2, eye).reshape(2, KV_HEADS * CMP_HIDDEN, KV_WIDTH)
    return pep, w1p.astype(BF16), b1p, w2p.astype(BF16)


def _pack_gmlp(ws, bs, chunk):
    reps = GMLP_CHUNK // chunk
    gw = jnp.tile(ws[:, :chunk, :chunk], (1, reps, reps))
    gb = jnp.tile(jnp.repeat(bs[:, :chunk].T, GMLP_WIDTH // GMLP_GROUPS, axis=1), (reps, 1))
    return gw, gb


def _rope_tables(pos):
    half = HEAD_DIM // 2
    inv_freq = ROPE_THETA ** (-jnp.arange(half, dtype=F32) / half)
    ang = pos.astype(F32)[:, None] * inv_freq[None, :]
    cos, sin = jnp.cos(ang), jnp.sin(ang)
    return jnp.concatenate([cos, cos], axis=1), jnp.concatenate([-sin, sin], axis=1)


def _block_indicator(n_keys):
    key_block = np.arange(n_keys)[None, :] // SLC_BLOCK
    return jnp.asarray(key_block == np.arange(MAX_SLC_BLOCKS)[:, None], dtype=BF16)


def _feature_major(a):
    lead = a.shape[:-4]
    n = len(lead)
    perm = tuple(range(n)) + (n + 1, n + 2, n + 3, n)
    return jnp.transpose(a, perm).reshape(lead + (2 * KV_WIDTH, a.shape[-4]))


def _position_major(a):
    lead = a.shape[:-2]
    n = len(lead)
    a = a.reshape(lead + (2, KV_HEADS, HEAD_DIM, a.shape[-1]))
    return jnp.transpose(a, tuple(range(n)) + (n + 3, n, n + 1, n + 2))


def kernel(x_prompt, x_sample, cache_cmp_kv, cache_slc_kv, state_win_kv, page_table,
           w_in, cmp_pe, cmp_w1, cmp_b1, cmp_w2, gmlp_ln_g, gmlp_ln_b, gmlp_ws, gmlp_bs,
           w_out, ln1_g, ln1_b, w_gate, w_up, w_down, ln2_g, ln2_b):
    depth = w_in.shape[0]
    b, t, d = x_prompt.shape
    db, s_new, _ = x_sample.shape
    n_pages = page_table.shape[1]
    past = n_pages * PAGE_SIZE
    n_buf = state_win_kv.shape[2]
    kv_tail = (2, KV_HEADS, HEAD_DIM)
    rows_s = SAMPLE_ROWS
    assert t % SLC_KEY_TILE == 0 and t >= WINDOW + QUERY_BLOCK and t // SLC_BLOCK <= MAX_SLC_BLOCKS
    assert s_new <= rows_s and (past + rows_s - 1) // SLC_BLOCK < MAX_SLC_BLOCKS
    assert past % CMP_BLOCK == 0 and past % CMP_BLOCK + s_new < CMP_BLOCK
    assert n_buf % LANES == 0 and n_buf >= LANES
    alpha = (2 * depth) ** 0.25

    tm_p = 512
    tm_s = min(512, db * rows_s)
    seq_per_step = 2 if db % 2 == 0 else 1
    attn_seq_per_step = 4 if db % 4 == 0 else seq_per_step
    cos_p, sin_p = _rope_tables(jnp.arange(t, dtype=jnp.int32))
    cos_s, sin_s = _rope_tables(past + jnp.arange(rows_s, dtype=jnp.int32))
    lane_reps = LANES // HEAD_DIM
    cos_pl, sin_pl = jnp.tile(cos_p, (1, lane_reps)), jnp.tile(sin_p, (1, lane_reps))
    cos_pt, sin_pt = cos_p.T, sin_p.T
    cos_s = jnp.tile(cos_s, (tm_s // rows_s, lane_reps))
    sin_s = jnp.tile(sin_s, (tm_s // rows_s, lane_reps))
    et_p = _block_indicator(t)
    et_s = _block_indicator(past + LANES)
    pool_cmp = _feature_major(cache_cmp_kv)
    pool_slc = _feature_major(cache_slc_kv)
    win_state = _feature_major(state_win_kv)

    yp = x_prompt.reshape(b * t, d)
    ys = jnp.pad(x_sample, ((0, 0), (0, rows_s - s_new), (0, 0))).reshape(db * rows_s, d)
    outs = {k: [] for k in ("s_cmp", "s_slc", "s_v")}
    assert min(WINDOW, t) == tm_p
    new_kv_prompt = ()
    win_next = ()
    cmp_tile = min(128, b * t // CMP_BLOCK)

    for l in range(depth):
        wt = _pack_w_in_t(w_in[l])
        pe_p, w1_p, b1_p, w2_p = _pack_compress(cmp_pe[l], cmp_w1[l], cmp_b1[l], cmp_w2[l])
        ln_g = gmlp_ln_g[l][None, :]
        ln_b = gmlp_ln_b[l][None, :]
        tail = (w_out[l].astype(BF16), ln1_g[l][None, :], ln1_b[l][None, :], w_gate[l].astype(BF16),
                w_up[l].astype(BF16), w_down[l].astype(BF16), ln2_g[l][None, :], ln2_b[l][None, :])

        qraw, qrot, gates, u, v, cmp_kv, *new_kv_prompt, slcb, winb = _inproj_prompt(
            yp, wt, cos_pl, sin_pl, cos_pt, sin_pt, ln_g, ln_b, tm_p, b, t, depth, l, new_kv_prompt)
        ckv = _compress_prompt(cmp_kv, pe_p, w1_p, b1_p, w2_p, cmp_tile)
        r3 = lambda a: a.reshape(b, t, a.shape[-1])
        o_nsa = _nsa_prompt(r3(qraw), r3(qrot), r3(gates), ckv.reshape(b, t // SLC_BLOCK, 4 * KV_WIDTH),
                            slcb, winb, et_p)
        gw, gb = _pack_gmlp(gmlp_ws[l], gmlp_bs[l], GMLP_CHUNK)
        yp = _post(yp, o_nsa.reshape(b * t, NSA_WIDTH), u, v, gw, gb, *tail, GMLP_CHUNK, alpha, tm_p)

        qraw, qrot, gates, u, v, cmp_kv, slc_kv, win_kv = _inproj_sample(
            ys, wt, cos_s, sin_s, ln_g, ln_b, tm_s)
        ckv = _compress_sample(pool_cmp, l, page_table, pe_p, w1_p, b1_p, w2_p, attn_seq_per_step)
        s3 = lambda a: a.reshape(db, rows_s, a.shape[-1])
        o_nsa, *win_next = _nsa_sample(pool_slc, win_state, l, page_table, s3(qraw), s3(qrot), s3(gates),
                                       ckv.reshape(db, past // SLC_BLOCK, 4 * KV_WIDTH),
                                       s3(slc_kv), s3(win_kv), et_s, attn_seq_per_step, s_new, win_next)
        gw, gb = _pack_gmlp(gmlp_ws[l], gmlp_bs[l], rows_s)
        ys = _post(ys, o_nsa.reshape(db * rows_s, NSA_WIDTH), u, v, gw, gb, *tail, rows_s, alpha, tm_s)
        new = lambda a: a.reshape((db, rows_s) + a.shape[1:])[:, :s_new]
        outs["s_cmp"].append(new(cmp_kv).reshape((db, s_new) + kv_tail))
        outs["s_slc"].append(new(slc_kv).reshape((db, s_new) + kv_tail))
        outs["s_v"].append(new(v))

    y_sample = ys.reshape(db, rows_s, d)[:, :s_new]
    p_cmp, p_slc, p_win = (_position_major(a) for a in new_kv_prompt)
    return (yp.reshape(b, t, d), y_sample, p_cmp, p_slc, p_win,
            jnp.stack(outs["s_cmp"]), jnp.stack(outs["s_slc"]),
            _position_major(win_next[0]), jnp.stack(outs["s_v"]))
```
